```python
import math
import jax, jax.numpy as jnp
from jax import lax
import numpy as np

D_MODEL = 2048
BATCH = 8
SEQ = 4096
DEPTH = 2

HEAD_DIM = 128
GRID_W = 64
Q_BLOCK = 128
ROPE_THETA = 10000.0
EPS = 1e-6
NEG_INF = -1e30

A_HEADS = 8
A_KV_HEADS = 2
B_CONFIGS = ((128, 1), (512, 4), (2048, 16))
B_OUT_HEADS = 4
B_HEADS = B_OUT_HEADS * len(B_CONFIGS)
C_HEADS = 4
C_Q_RANK = 512
C_KV_RANK = 256
C_NOPE = 128
C_ROPE = 64
C_V = 128
D_FF = 5632

A_Q = A_HEADS * HEAD_DIM
A_KV = A_KV_HEADS * HEAD_DIM
A_IN = A_Q + 2 * A_KV
B_QKV = B_HEADS * HEAD_DIM
B_IN = 3 * B_QKV
C_IN = C_Q_RANK + C_KV_RANK + C_ROPE
W_IN = A_IN + B_IN + C_IN
A_OUT = A_HEADS * HEAD_DIM
B_OUT = B_OUT_HEADS * HEAD_DIM
C_OUT = C_HEADS * C_V
MIX_W = A_OUT + B_OUT + C_OUT

kernel_name = "hybrid_parallel_gqa_dilated_mla_macaron"


def rms_norm(x, g):
    xf = x.astype(jnp.float32)
    y = xf * lax.rsqrt(jnp.mean(xf * xf, axis=-1, keepdims=True) + EPS)
    return (y * g.astype(jnp.float32)).astype(x.dtype)


def rope(x, pos):
    half = x.shape[-1] // 2
    inv = ROPE_THETA ** (-jnp.arange(half, dtype=jnp.float32) / half)
    ang = pos.astype(jnp.float32)[:, None] * inv[None, :]
    cos = jnp.cos(ang).astype(x.dtype)
    sin = jnp.sin(ang).astype(x.dtype)
    x1, x2 = x[..., :half], x[..., half:]
    return jnp.concatenate([x1 * cos - x2 * sin, x1 * sin + x2 * cos], axis=-1)


def axial_rope(x, row, col):
    h = x.shape[-1] // 2
    return jnp.concatenate([rope(x[..., :h], row), rope(x[..., h:], col)], axis=-1)


def split_heads(x, n):
    b, s, _ = x.shape
    return x.reshape(b, s, n, -1).transpose(0, 2, 1, 3)


def merge_heads(x):
    b, n, s, d = x.shape
    return x.transpose(0, 2, 1, 3).reshape(b, s, n * d)


def swiglu(x, w_gu, w_down):
    g, u = jnp.split(x @ w_gu, 2, axis=-1)
    return (jax.nn.silu(g) * u) @ w_down


def block_attention(q, k, v, scale):
    b, hk, g, s, dq = q.shape
    nb = s // Q_BLOCK
    qb = jnp.moveaxis(q.reshape(b, hk, g, nb, Q_BLOCK, dq), 3, 0)

    def one(qi):
        sc = jnp.einsum('bkgqd,bksd->bkgqs', qi, k, preferred_element_type=jnp.float32) * scale
        p = jax.nn.softmax(sc, axis=-1)
        return jnp.einsum('bkgqs,bksd->bkgqd', p.astype(v.dtype), v)

    o = lax.map(one, qb)
    return jnp.moveaxis(o, 0, 3).reshape(b, hk, g, s, v.shape[-1])


def dilated_window_attention(q, k, v, dilation, half_span, slopes):
    b, h, s, dh = q.shape
    n = half_span
    L = s // dilation
    nbq = -(-L // n)
    Lp = nbq * n

    def to_sub(t):
        return jnp.swapaxes(t.reshape(b, h, L, dilation, dh), 2, 3)

    qs = jnp.pad(to_sub(q), ((0, 0), (0, 0), (0, 0), (0, Lp - L), (0, 0)))
    qs = qs.reshape(b, h, dilation, nbq, n, dh)

    def key_blocks(t):
        tp = jnp.pad(to_sub(t), ((0, 0), (0, 0), (0, 0), (n, Lp - L + n), (0, 0)))
        tp = tp.reshape(b, h, dilation, nbq + 2, n, dh)
        return jnp.concatenate([tp[:, :, :, :-2], tp[:, :, :, 1:-1], tp[:, :, :, 2:]], axis=4)

    kb, vb = key_blocks(k), key_blocks(v)
    qi = jnp.arange(n)[:, None]
    kc = jnp.arange(3 * n)[None, :]
    step = kc - n - qi
    kpos = (jnp.arange(nbq)[:, None, None] - 1) * n + kc[None]
    mask = (jnp.abs(step) <= n)[None] & (kpos >= 0) & (kpos < L)
    dist = (jnp.abs(step) * dilation).astype(jnp.float32)
    bias = -slopes.astype(jnp.float32)[:, None, None, None, None] * dist
    sc = jnp.einsum('bhrnqd,bhrnkd->bhrnqk', qs, kb, preferred_element_type=jnp.float32) * (dh ** -0.5) + bias
    sc = jnp.where(mask, sc, NEG_INF)
    lse = jax.nn.logsumexp(sc, axis=-1)
    p = jnp.exp(sc - lse[..., None])
    o = jnp.einsum('bhrnqk,bhrnkd->bhrnqd', p.astype(v.dtype), vb)
    o = jnp.swapaxes(o.reshape(b, h, dilation, Lp, dh)[:, :, :, :L], 2, 3).reshape(b, h, s, dh)
    lse = jnp.swapaxes(lse.reshape(b, h, dilation, Lp)[:, :, :, :L], 2, 3).reshape(b, h, s)
    return o, lse


def token_mix(h, t, row, col, w_in, a_q_norm, a_k_norm, b_q_norm, b_k_norm,
              c_q_a_norm, c_q_up, c_kv_a_norm, c_kv_up, c_q_norm, c_k_norm, out_norm, w_out):
    b, s, _ = h.shape
    z = h @ w_in
    za = z[..., :A_IN]
    zb = z[..., A_IN:A_IN + B_IN]
    zc = z[..., A_IN + B_IN:]

    qa = axial_rope(rms_norm(split_heads(za[..., :A_Q], A_HEADS), a_q_norm), row, col)
    ka = axial_rope(rms_norm(split_heads(za[..., A_Q:A_Q + A_KV], A_KV_HEADS), a_k_norm), row, col)
    va = split_heads(za[..., A_Q + A_KV:], A_KV_HEADS)
    qa = qa.reshape(b, A_KV_HEADS, A_HEADS // A_KV_HEADS, s, HEAD_DIM)
    oa = block_attention(qa, ka, va, HEAD_DIM ** -0.5)
    oa = merge_heads(oa.reshape(b, A_HEADS, s, HEAD_DIM))

    qb = rms_norm(split_heads(zb[..., :B_QKV], B_HEADS), b_q_norm)
    kb = rms_norm(split_heads(zb[..., B_QKV:2 * B_QKV], B_HEADS), b_k_norm)
    vb = split_heads(zb[..., 2 * B_QKV:], B_HEADS)
    slopes = 2.0 ** (-8.0 * jnp.arange(1, B_HEADS + 1, dtype=jnp.float32) / B_HEADS)
    outs, lses = [], []
    for g, (win, dil) in enumerate(B_CONFIGS):
        sl = slice(g * B_OUT_HEADS, (g + 1) * B_OUT_HEADS)
        o, lse = dilated_window_attention(qb[:, sl], kb[:, sl], vb[:, sl], dil, win // (2 * dil), slopes[sl])
        outs.append(o)
        lses.append(lse)
    wts = jax.nn.softmax(jnp.stack(lses), axis=0)
    ob = jnp.sum(wts[..., None] * jnp.stack(outs).astype(jnp.float32), axis=0).astype(h.dtype)
    ob = merge_heads(ob)

    cq = rms_norm(zc[..., :C_Q_RANK], c_q_a_norm) @ c_q_up
    ckv = rms_norm(zc[..., C_Q_RANK:C_Q_RANK + C_KV_RANK], c_kv_a_norm) @ c_kv_up
    k_rope = zc[..., C_Q_RANK + C_KV_RANK:]
    qc = rms_norm(split_heads(cq, C_HEADS), c_q_norm)
    kvc = split_heads(ckv, C_HEADS)
    kr = jnp.broadcast_to(k_rope[:, None], (b, C_HEADS, s, C_ROPE))
    kc = rms_norm(jnp.concatenate([kvc[..., :C_NOPE], kr], axis=-1), c_k_norm)
    vc = kvc[..., C_NOPE:]
    qc = jnp.concatenate([qc[..., :C_NOPE], rope(qc[..., C_NOPE:], t)], axis=-1)
    kc = jnp.concatenate([kc[..., :C_NOPE], rope(kc[..., C_NOPE:], t)], axis=-1)
    oc = block_attention(qc[:, :, None], kc, vc, (C_NOPE + C_ROPE) ** -0.5)
    oc = merge_heads(oc[:, :, 0])

    y = jnp.concatenate([
        rms_norm(oa, out_norm[:A_OUT]),
        rms_norm(ob, out_norm[A_OUT:A_OUT + B_OUT]),
        rms_norm(oc, out_norm[A_OUT + B_OUT:]),
    ], axis=-1)
    return y @ w_out


def setup_inputs(seed: int = 0) -> dict:
    key = jax.random.key(seed)
    ks = jax.random.split(key, 24)
    f32 = jnp.float32
    L = DEPTH

    def w(k, shape, fan_in):
        return jax.random.normal(k, shape, f32) * (fan_in ** -0.5)

    def g(k, shape):
        return 1.0 + 0.02 * jax.random.normal(k, shape, f32)

    return {
        "x": jax.random.normal(ks[0], (BATCH, SEQ, D_MODEL), f32),
        "ffn1_norm": g(ks[1], (L, D_MODEL)),
        "ffn1_w_gu": w(ks[2], (L, D_MODEL, 2 * D_FF), D_MODEL),
        "ffn1_w_down": w(ks[3], (L, D_FF, D_MODEL), D_FF),
        "mix_norm": g(ks[4], (L, D_MODEL)),
        "w_in": w(ks[5], (L, D_MODEL, W_IN), D_MODEL),
        "a_q_norm": g(ks[6], (L, HEAD_DIM)),
        "a_k_norm": g(ks[7], (L, HEAD_DIM)),
        "b_q_norm": g(ks[8], (L, HEAD_DIM)),
        "b_k_norm": g(ks[9], (L, HEAD_DIM)),
        "c_q_a_norm": g(ks[10], (L, C_Q_RANK)),
        "c_q_up": w(ks[11], (L, C_Q_RANK, C_HEADS * (C_NOPE + C_ROPE)), C_Q_RANK),
        "c_kv_a_norm": g(ks[12], (L, C_KV_RANK)),
        "c_kv_up": w(ks[13], (L, C_KV_RANK, C_HEADS * (C_NOPE + C_V)), C_KV_RANK),
        "c_q_norm": g(ks[14], (L, C_NOPE + C_ROPE)),
        "c_k_norm": g(ks[15], (L, C_NOPE + C_ROPE)),
        "out_norm": g(ks[16], (L, MIX_W)),
        "w_out": w(ks[17], (L, MIX_W, D_MODEL), MIX_W),
        "ffn2_norm": g(ks[18], (L, D_MODEL)),
        "ffn2_w_gu": w(ks[19], (L, D_MODEL, 2 * D_FF), D_MODEL),
        "ffn2_w_down": w(ks[20], (L, D_FF, D_MODEL), D_FF),
    }


def reference(x, ffn1_norm, ffn1_w_gu, ffn1_w_down, mix_norm, w_in, a_q_norm, a_k_norm,
              b_q_norm, b_k_norm, c_q_a_norm, c_q_up, c_kv_a_norm, c_kv_up, c_q_norm, c_k_norm,
              out_norm, w_out, ffn2_norm, ffn2_w_gu, ffn2_w_down):
    s = x.shape[1]
    rows = s // GRID_W
    t = jnp.arange(s, dtype=jnp.int32)
    row = jnp.repeat(jnp.arange(rows, dtype=jnp.int32), GRID_W)
    col = jnp.tile(jnp.arange(GRID_W, dtype=jnp.int32), rows)
    for l in range(DEPTH):
        x = x + 0.5 * swiglu(rms_norm(x, ffn1_norm[l]), ffn1_w_gu[l], ffn1_w_down[l])
        x = x + token_mix(rms_norm(x, mix_norm[l]), t, row, col, w_in[l], a_q_norm[l], a_k_norm[l],
                          b_q_norm[l], b_k_norm[l], c_q_a_norm[l], c_q_up[l], c_kv_a_norm[l], c_kv_up[l],
                          c_q_norm[l], c_k_norm[l], out_norm[l], w_out[l])
        x = x + 0.5 * swiglu(rms_norm(x, ffn2_norm[l]), ffn2_w_gu[l], ffn2_w_down[l])
    return x
```

```python
import functools

import jax
import jax.numpy as jnp
from jax import lax
from jax.experimental import pallas as pl
from jax.experimental.pallas import tpu as pltpu

F32 = jnp.float32
BF16 = jnp.bfloat16

D_MODEL = 2048
SEQ = 4096
GRID_W = 64
HEAD_DIM = 128
ROPE_THETA = 10000.0
EPS = 1e-6
NEG_BIG = -1e30

A_HEADS = 8
A_KV_HEADS = 2
B_CONFIGS = ((128, 1), (512, 4), (2048, 16))
B_SLOTS = 4
B_HEADS = B_SLOTS * len(B_CONFIGS)
C_HEADS = 4
C_Q_RANK = 512
C_KV_RANK = 256
C_NOPE = 128
C_ROPE = 64
C_QK = C_NOPE + C_ROPE
D_FF = 5632

A_Q = A_HEADS * HEAD_DIM
A_KV = A_KV_HEADS * HEAD_DIM
A_IN = A_Q + 2 * A_KV
B_QKV = B_HEADS * HEAD_DIM
B_IN = 3 * B_QKV
AB_IN = A_IN + B_IN
AB_HEADS = AB_IN // HEAD_DIM
A_OUT = A_HEADS * HEAD_DIM
B_OUT = B_SLOTS * HEAD_DIM
C_OUT = C_HEADS * HEAD_DIM

V7X_LANES = 128
V7X_VMEM_LIMIT = 56 * 1024 * 1024

TM_FFN = 512
TF_FFN = 512
TM_PROJ = 512
TN_PROJ = 512
TM_OUT = 256
TQ_B = 256
B_SPAN = 64
TK_ATT = 512


def _params(*sem):
    return pltpu.CompilerParams(dimension_semantics=sem, vmem_limit_bytes=V7X_VMEM_LIMIT)


def _rms(x, gain, width):
    ms = jnp.sum(x * x, axis=-1, keepdims=True) * (1.0 / width)
    return (x * lax.rsqrt(ms + EPS)) * gain


def _rotate_half_pairs(y):
    lane = lax.broadcasted_iota(jnp.int32, y.shape, 1)
    first = (lane % 64) < 32
    return jnp.where(first, pltpu.roll(y, V7X_LANES - 32, 1), pltpu.roll(y, 32, 1))


def _rope(y, cos, sin_signed):
    return y * cos + _rotate_half_pairs(y) * sin_signed


def _norm_kernel(x_ref, g_ref, h_ref):
    h_ref[...] = _rms(x_ref[...], g_ref[...], D_MODEL).astype(BF16)


def _norm(x, gain):
    t = x.shape[0]
    return pl.pallas_call(
        _norm_kernel,
        grid=(t // TM_FFN,),
        in_specs=[pl.BlockSpec((TM_FFN, D_MODEL), lambda i: (i, 0)),
                  pl.BlockSpec((1, D_MODEL), lambda i: (0, 0))],
        out_specs=pl.BlockSpec((TM_FFN, D_MODEL), lambda i: (i, 0)),
        out_shape=jax.ShapeDtypeStruct((t, D_MODEL), BF16),
        compiler_params=_params("parallel"),
        name="norm",
    )(x, gain)


def _ffn_kernel(x_ref, h_ref, wg_ref, wu_ref, wd_ref, *rest, emit_next):
    if emit_next:
        gn_ref, xo_ref, ho_ref, acc_ref = rest
    else:
        xo_ref, acc_ref = rest
    j = pl.program_id(1)

    @pl.when(j == 0)
    def _():
        acc_ref[...] = jnp.zeros_like(acc_ref)

    h = h_ref[...]
    g = jnp.dot(h, wg_ref[...], preferred_element_type=F32)
    u = jnp.dot(h, wu_ref[...], preferred_element_type=F32)
    a = (g * (1.0 / (1.0 + jnp.exp(-g))) * u).astype(BF16)
    acc_ref[...] += jnp.dot(a, wd_ref[...], preferred_element_type=F32)

    @pl.when(j == pl.num_programs(1) - 1)
    def _():
        xn = x_ref[...] + 0.5 * acc_ref[...]
        xo_ref[...] = xn
        if emit_next:
            ho_ref[...] = _rms(xn, gn_ref[...], D_MODEL).astype(BF16)


def _ffn(x, h, w_gu, w_down, gain_next=None):
    t = x.shape[0]
    nf = D_FF // TF_FFN
    emit_next = gain_next is not None
    tok = pl.BlockSpec((TM_FFN, D_MODEL), lambda i, j: (i, 0))
    in_specs = [tok, tok,
                pl.BlockSpec((D_MODEL, TF_FFN), lambda i, j: (0, j)),
                pl.BlockSpec((D_MODEL, TF_FFN), lambda i, j: (0, j + nf)),
                pl.BlockSpec((TF_FFN, D_MODEL), lambda i, j: (j, 0))]
    args = [x, h, w_gu, w_gu, w_down]
    out_specs = [tok]
    out_shape = [jax.ShapeDtypeStruct((t, D_MODEL), F32)]
    if emit_next:
        in_specs.append(pl.BlockSpec((1, D_MODEL), lambda i, j: (0, 0)))
        args.append(gain_next)
        out_specs.append(tok)
        out_shape.append(jax.ShapeDtypeStruct((t, D_MODEL), BF16))
    return pl.pallas_call(
        functools.partial(_ffn_kernel, emit_next=emit_next),
        grid=(t // TM_FFN, nf),
        in_specs=in_specs,
        out_specs=out_specs,
        out_shape=out_shape,
        scratch_shapes=[pltpu.VMEM((TM_FFN, D_MODEL), F32)],
        compiler_params=_params("parallel", "arbitrary"),
        name="ffn",
    )(*args)


def _proj_heads_kernel(h_ref, w_ref, p_ref, cos_ref, sin_ref, o_ref, *, rope_steps):
    j = pl.program_id(1)
    z = jnp.dot(h_ref[...], w_ref[...], preferred_element_type=F32)
    heads = TN_PROJ // HEAD_DIM

    def epilogue(rope):
        for hh in range(heads):
            zz = z[:, hh * HEAD_DIM:(hh + 1) * HEAD_DIM]
            gain = p_ref[hh, 0:1, :]
            norm_on = p_ref[hh, 1:2, :] > 0.0
            ms = jnp.sum(zz * zz, axis=-1, keepdims=True) * (1.0 / HEAD_DIM)
            r = jnp.where(norm_on, lax.rsqrt(ms + EPS), 1.0)
            y = (zz * r) * gain
            if rope:
                rope_on = p_ref[hh, 2:3, :] > 0.0
                y = jnp.where(rope_on, _rope(y, cos_ref[...], sin_ref[...]), y)
            o_ref[hh] = y.astype(BF16)

    @pl.when(j < rope_steps)
    def _():
        epilogue(True)

    @pl.when(j >= rope_steps)
    def _():
        epilogue(False)


def _proj_heads(h, w_ab, head_params, cos, sin):
    t = h.shape[0]
    heads = TN_PROJ // HEAD_DIM
    seq_tiles = SEQ // TM_PROJ
    rope_steps = -(-(A_Q + A_KV) // TN_PROJ)
    return pl.pallas_call(
        functools.partial(_proj_heads_kernel, rope_steps=rope_steps),
        grid=(t // TM_PROJ, AB_IN // TN_PROJ),
        in_specs=[pl.BlockSpec((TM_PROJ, D_MODEL), lambda i, j: (i, 0)),
                  pl.BlockSpec((D_MODEL, TN_PROJ), lambda i, j: (0, j)),
                  pl.BlockSpec((heads, 8, HEAD_DIM), lambda i, j: (j, 0, 0)),
                  pl.BlockSpec((TM_PROJ, HEAD_DIM), lambda i, j: (i % seq_tiles, 0)),
                  pl.BlockSpec((TM_PROJ, HEAD_DIM), lambda i, j: (i % seq_tiles, 0))],
        out_specs=pl.BlockSpec((heads, TM_PROJ, HEAD_DIM), lambda i, j: (j, i, 0)),
        out_shape=jax.ShapeDtypeStruct((AB_HEADS, t, HEAD_DIM), BF16),
        compiler_params=_params("parallel", "arbitrary"),
        name="proj_heads",
    )(h, w_ab, head_params, cos, sin)


def _proj_c_kernel(h_ref, wc_ref, gqa_ref, gkva_ref, qup_ref, kvup_ref, gq_ref, gk_ref,
                   cos_ref, sin_ref, q_ref, k_ref, v_ref):
    z = jnp.dot(h_ref[...], wc_ref[...], preferred_element_type=F32)
    q_lat = _rms(z[:, :C_Q_RANK], gqa_ref[...], C_Q_RANK).astype(BF16)
    kv_lat = _rms(z[:, C_Q_RANK:C_Q_RANK + C_KV_RANK], gkva_ref[...], C_KV_RANK).astype(BF16)
    k_rope = z[:, C_Q_RANK + C_KV_RANK:]
    cq = jnp.dot(q_lat, qup_ref[...], preferred_element_type=F32)
    ckv = jnp.dot(kv_lat, kvup_ref[...], preferred_element_type=F32)
    cos = cos_ref[...]
    sin = sin_ref[...]
    nope_w = C_HEADS * C_NOPE
    for hh in range(C_HEADS):
        qn = cq[:, hh * C_NOPE:(hh + 1) * C_NOPE]
        qr = cq[:, nope_w + hh * V7X_LANES:nope_w + (hh + 1) * V7X_LANES]
        ms = (jnp.sum(qn * qn, axis=-1, keepdims=True)
              + jnp.sum(qr * qr, axis=-1, keepdims=True)) * (1.0 / C_QK)
        r = lax.rsqrt(ms + EPS)
        q_ref[hh, :, 0:C_NOPE] = ((qn * r) * gq_ref[0:1, :]).astype(BF16)
        q_ref[hh, :, C_NOPE:] = _rope((qr * r) * gq_ref[1:2, :], cos, sin).astype(BF16)

        kn = ckv[:, hh * 2 * C_NOPE:hh * 2 * C_NOPE + C_NOPE]
        ms = (jnp.sum(kn * kn, axis=-1, keepdims=True)
              + jnp.sum(k_rope * k_rope, axis=-1, keepdims=True)) * (1.0 / C_QK)
        r = lax.rsqrt(ms + EPS)
        k_ref[hh, :, 0:C_NOPE] = ((kn * r) * gk_ref[0:1, :]).astype(BF16)
        k_ref[hh, :, C_NOPE:] = _rope((k_rope * r) * gk_ref[1:2, :], cos, sin).astype(BF16)
        v_ref[hh] = ckv[:, hh * 2 * C_NOPE + C_NOPE:(hh + 1) * 2 * C_NOPE].astype(BF16)


def _proj_c(h, w_c, gqa, gkva, q_up, kv_up, gq, gk, cos, sin):
    t = h.shape[0]
    seq_tiles = SEQ // TM_PROJ
    full = lambda a: pl.BlockSpec(a.shape, lambda i: (0,) * a.ndim)
    qk_w = 2 * V7X_LANES
    return pl.pallas_call(
        _proj_c_kernel,
        grid=(t // TM_PROJ,),
        in_specs=[pl.BlockSpec((TM_PROJ, D_MODEL), lambda i: (i, 0)),
                  full(w_c), full(gqa), full(gkva), full(q_up), full(kv_up), full(gq), full(gk),
                  pl.BlockSpec((TM_PROJ, V7X_LANES), lambda i: (i % seq_tiles, 0)),
                  pl.BlockSpec((TM_PROJ, V7X_LANES), lambda i: (i % seq_tiles, 0))],
        out_specs=[pl.BlockSpec((C_HEADS, TM_PROJ, qk_w), lambda i: (0, i, 0)),
                   pl.BlockSpec((C_HEADS, TM_PROJ, qk_w), lambda i: (0, i, 0)),
                   pl.BlockSpec((C_HEADS, TM_PROJ, C_NOPE), lambda i: (0, i, 0))],
        out_shape=[jax.ShapeDtypeStruct((C_HEADS, t, qk_w), BF16),
                   jax.ShapeDtypeStruct((C_HEADS, t, qk_w), BF16),
                   jax.ShapeDtypeStruct((C_HEADS, t, C_NOPE), BF16)],
        compiler_params=_params("parallel"),
        name="proj_c",
    )(h, w_c, gqa, gkva, q_up, kv_up, gq, gk, cos, sin)


def _attn_kernel(q_ref, k_ref, v_ref, o_ref, m_ref, l_ref, acc_ref, *, scale):
    g, tq, dq = q_ref.shape
    q = q_ref[...].reshape(g * tq, dq)
    m_ref[...] = jnp.full_like(m_ref, NEG_BIG)
    l_ref[...] = jnp.zeros_like(l_ref)
    acc_ref[...] = jnp.zeros_like(acc_ref)

    def body(c, carry):
        start = pl.multiple_of(c * TK_ATT, TK_ATT)
        k = k_ref[0, pl.ds(start, TK_ATT), :]
        v = v_ref[0, pl.ds(start, TK_ATT), :]
        s = lax.dot_general(q, k, (((1,), (1,)), ((), ())), preferred_element_type=F32) * scale
        m_prev = m_ref[...]
        m_new = jnp.maximum(m_prev, jnp.max(s, axis=-1, keepdims=True))
        alpha = jnp.exp(m_prev - m_new)
        p = jnp.exp(s - m_new)
        l_ref[...] = alpha * l_ref[...] + jnp.sum(p, axis=-1, keepdims=True)
        acc_ref[...] = alpha * acc_ref[...] + jnp.dot(p.astype(BF16), v, preferred_element_type=F32)
        m_ref[...] = m_new
        return carry

    lax.fori_loop(0, SEQ // TK_ATT, body, 0)
    o = acc_ref[...] / l_ref[...]
    o_ref[...] = o.reshape(g, tq, o.shape[-1]).astype(o_ref.dtype)


def _attention(q, k, v, *, kv_heads, group, k_head0, v_head0, tq, scale):
    _, t, dq = q.shape
    dv = v.shape[-1]
    nb = t // SEQ
    qt = SEQ // tq
    rows = group * tq
    return pl.pallas_call(
        functools.partial(_attn_kernel, scale=scale),
        grid=(nb, kv_heads, qt),
        in_specs=[pl.BlockSpec((group, tq, dq), lambda b, h, i: (h, b * qt + i, 0)),
                  pl.BlockSpec((1, SEQ, dq), lambda b, h, i: (k_head0 + h, b, 0)),
                  pl.BlockSpec((1, SEQ, dv), lambda b, h, i: (v_head0 + h, b, 0))],
        out_specs=pl.BlockSpec((group, tq, dv), lambda b, h, i: (h, b * qt + i, 0)),
        out_shape=jax.ShapeDtypeStruct((kv_heads * group, t, dv), BF16),
        scratch_shapes=[pltpu.VMEM((rows, 1), F32), pltpu.VMEM((rows, 1), F32),
                        pltpu.VMEM((rows, dv), F32)],
        compiler_params=_params("parallel", "parallel", "arbitrary"),
        name="attention",
    )(q, k, v)


def _attn_b_kernel(q0, k0, v0, q1, k1, v1, q2, k2, v2, coef_ref, tab_ref,
                   o0, o1, o2, e0, e1, e2):
    scale = HEAD_DIM ** -0.5
    groups = ((q0, k0, v0, o0, e0), (q1, k1, v1, o1, e1), (q2, k2, v2, o2, e2))

    def tile(qr, kr, vr, orf, erf, cols, l0, ks, width, steps, coef):
        q = qr[0, 0, pl.ds(l0, TQ_B), cols]
        k = kr[0, 0, pl.ds(ks, width), cols]
        v = vr[0, 0, pl.ds(ks, width), cols]
        s = lax.dot_general(q, k, (((1,), (1,)), ((), ())), preferred_element_type=F32)
        s = s * scale - coef * steps
        m = jnp.max(s, axis=-1, keepdims=True)
        p = jnp.exp(s - m)
        l = jnp.sum(p, axis=-1, keepdims=True)
        o = jnp.dot(p.astype(BF16), v, preferred_element_type=F32) / l
        orf[0, 0, pl.ds(l0, TQ_B), cols] = o.astype(orf.dtype)
        erf[0, 0, pl.ds(l0, TQ_B), cols] = jnp.broadcast_to(m + jnp.log(l), (TQ_B, HEAD_DIM))

    for g, (_, dil) in enumerate(B_CONFIGS):
        qr, kr, vr, orf, erf = groups[g]
        length = SEQ // dil
        nt = length // TQ_B
        coef = coef_ref[0, g:g + 1, 0:1]
        for r in range(dil):
            cols = slice(r * HEAD_DIM, (r + 1) * HEAD_DIM)
            if nt == 1:
                tile(qr, kr, vr, orf, erf, cols, 0, 0, TQ_B, tab_ref[0, :, 0:TQ_B], coef)
            else:
                width = 2 * TQ_B

                def body(ti, carry, qr=qr, kr=kr, vr=vr, orf=orf, erf=erf, cols=cols,
                         length=length, nt=nt, width=width, coef=coef):
                    l0 = pl.multiple_of(ti * TQ_B, TQ_B)
                    ks = pl.multiple_of(jnp.clip(l0 - TQ_B // 2, 0, length - width), TQ_B // 2)
                    case = jnp.where(ti == 0, 0, jnp.where(ti == nt - 1, 2, 1))
                    tile(qr, kr, vr, orf, erf, cols, l0, ks, width, tab_ref[case], coef)
                    return carry

                lax.fori_loop(0, nt, body, 0)


def _attn_b(qkv_heads, coef, tab):
    t = qkv_heads.shape[1]
    nb = t // SEQ
    base = A_IN // HEAD_DIM
    in_specs, args, out_specs, out_shapes = [], [], [], []
    for g, (_, dil) in enumerate(B_CONFIGS):
        length = SEQ // dil
        view = qkv_heads.reshape(AB_HEADS, nb, length, dil * HEAD_DIM)
        for kind in range(3):
            head0 = base + kind * B_HEADS + g * B_SLOTS
            in_specs.append(pl.BlockSpec((1, 1, length, dil * HEAD_DIM),
                                         lambda b, j, head0=head0: (head0 + j, b, 0, 0)))
            args.append(view)
    for dtype in (BF16, F32):
        for g, (_, dil) in enumerate(B_CONFIGS):
            length = SEQ // dil
            out_specs.append(pl.BlockSpec((1, 1, length, dil * HEAD_DIM), lambda b, j: (j, b, 0, 0)))
            out_shapes.append(jax.ShapeDtypeStruct((B_SLOTS, nb, length, dil * HEAD_DIM), dtype))
    in_specs.append(pl.BlockSpec((1, 8, V7X_LANES), lambda b, j: (j, 0, 0)))
    in_specs.append(pl.BlockSpec(tab.shape, lambda b, j: (0, 0, 0)))
    outs = pl.pallas_call(
        _attn_b_kernel,
        grid=(nb, B_SLOTS),
        in_specs=in_specs,
        out_specs=out_specs,
        out_shape=out_shapes,
        compiler_params=_params("parallel", "parallel"),
        name="attn_b",
    )(*args, coef, tab)
    return [o.reshape(B_SLOTS, t, HEAD_DIM) for o in outs]


def _mix_out_kernel(x_ref, oa_ref, ob0, ob1, ob2, e0, e1, e2, oc_ref, gout_ref, w_ref, gn_ref,
                    xo_ref, ho_ref, y_ref):
    def group_norm(tiles, col0):
        width = len(tiles) * HEAD_DIM
        ssq = sum(jnp.sum(tl * tl, axis=-1, keepdims=True) for tl in tiles)
        r = lax.rsqrt(ssq * (1.0 / width) + EPS)
        for n, tl in enumerate(tiles):
            c = col0 + n * HEAD_DIM
            y_ref[:, c:c + HEAD_DIM] = ((tl * r) * gout_ref[:, c:c + HEAD_DIM]).astype(BF16)

    group_norm([oa_ref[n].astype(F32) for n in range(A_HEADS)], 0)

    merged = []
    for n in range(B_SLOTS):
        lse = (e0[n], e1[n], e2[n])
        mx = jnp.maximum(jnp.maximum(lse[0], lse[1]), lse[2])
        w = [jnp.exp(e - mx) for e in lse]
        den = w[0] + w[1] + w[2]
        outs = (ob0[n], ob1[n], ob2[n])
        merged.append(sum((w[g] / den) * outs[g].astype(F32) for g in range(3)))
    group_norm(merged, A_OUT)

    group_norm([oc_ref[n].astype(F32) for n in range(C_HEADS)], A_OUT + B_OUT)

    xn = x_ref[...] + jnp.dot(y_ref[...], w_ref[...], preferred_element_type=F32)
    xo_ref[...] = xn
    ho_ref[...] = _rms(xn, gn_ref[...], D_MODEL).astype(BF16)


def _mix_out(x, oa, ob, eb, oc, gain_out, w_out, gain_next):
    t = x.shape[0]
    heads = lambda n: pl.BlockSpec((n, TM_OUT, HEAD_DIM), lambda i: (0, i, 0))
    row = pl.BlockSpec((1, D_MODEL), lambda i: (0, 0))
    tok = pl.BlockSpec((TM_OUT, D_MODEL), lambda i: (i, 0))
    return pl.pallas_call(
        _mix_out_kernel,
        grid=(t // TM_OUT,),
        in_specs=[tok, heads(A_HEADS)] + [heads(B_SLOTS)] * 6 + [heads(C_HEADS), row,
                  pl.BlockSpec((D_MODEL, D_MODEL), lambda i: (0, 0)), row],
        out_specs=[tok, tok],
        out_shape=[jax.ShapeDtypeStruct((t, D_MODEL), F32),
                   jax.ShapeDtypeStruct((t, D_MODEL), BF16)],
        scratch_shapes=[pltpu.VMEM((TM_OUT, D_MODEL), BF16)],
        compiler_params=_params("parallel"),
        name="mix_out",
    )(x, oa, *ob, *eb, oc, gain_out, w_out, gain_next)


def _rope_tables(pos_a, pos_b):
    half = 32
    inv = ROPE_THETA ** (-jnp.arange(half, dtype=F32) / half)

    def one(pos):
        ang = pos.astype(F32)[:, None] * inv[None, :]
        c, s = jnp.cos(ang), jnp.sin(ang)
        return jnp.concatenate([c, c], axis=-1), jnp.concatenate([-s, s], axis=-1)

    ca, sa = one(pos_a)
    cb, sb = one(pos_b)
    return jnp.concatenate([ca, cb], axis=-1), jnp.concatenate([sa, sb], axis=-1)


def _band_steps_table():
    i = jnp.arange(TQ_B)[:, None]
    jj = jnp.arange(2 * TQ_B)[None, :]
    tabs = []
    for off in (0, TQ_B // 2, TQ_B):
        rel = jnp.abs(off + i - jj)
        tabs.append(jnp.where(rel <= B_SPAN, rel.astype(F32), -NEG_BIG))
    return jnp.stack(tabs)


def _head_params(gq_a, gk_a, gq_b, gk_b):
    ones = jnp.ones((HEAD_DIM,), F32)
    rows = ([(gq_a, 1.0, 1.0)] * A_HEADS + [(gk_a, 1.0, 1.0)] * A_KV_HEADS + [(ones, 0.0, 0.0)] * A_KV_HEADS
            + [(gq_b, 1.0, 0.0)] * B_HEADS + [(gk_b, 1.0, 0.0)] * B_HEADS + [(ones, 0.0, 0.0)] * B_HEADS)
    out = []
    for gain, norm_on, rope_on in rows:
        blk = jnp.zeros((8, HEAD_DIM), F32)
        blk = blk.at[0].set(gain.astype(F32)).at[1].set(norm_on).at[2].set(rope_on)
        out.append(blk)
    return jnp.stack(out)


def kernel(x, ffn1_norm, ffn1_w_gu, ffn1_w_down, mix_norm, w_in, a_q_norm, a_k_norm, b_q_norm, b_k_norm, c_q_a_norm, c_q_up, c_kv_a_norm, c_kv_up, c_q_norm, c_k_norm, out_norm, w_out, ffn2_norm, ffn2_w_gu, ffn2_w_down):
    nb, s, d = x.shape
    assert (s, d) == (SEQ, D_MODEL)
    depth = w_in.shape[0]
    t = nb * s
    x = x.reshape(t, d)
    row = lambda v: v.reshape(1, -1).astype(F32)

    pos = jnp.arange(s, dtype=jnp.int32)
    cos_a, sin_a = _rope_tables(pos // GRID_W, pos % GRID_W)
    cos_c, sin_c = _rope_tables(pos, pos)
    tab = _band_steps_table()
    slopes = 2.0 ** (-8.0 * jnp.arange(1, B_HEADS + 1, dtype=F32) / B_HEADS)
    dil = jnp.array([c[1] for c in B_CONFIGS], F32)
    coef = (slopes.reshape(len(B_CONFIGS), B_SLOTS) * dil[:, None]).T
    coef = jnp.zeros((B_SLOTS, 8, V7X_LANES), F32).at[:, :len(B_CONFIGS), :].set(coef[:, :, None])

    h = _norm(x, row(ffn1_norm[0]))
    for l in range(depth):
        x, h = _ffn(x, h, ffn1_w_gu[l].astype(BF16), ffn1_w_down[l].astype(BF16), row(mix_norm[l]))

        w = w_in[l]
        qkv = _proj_heads(h, w[:, :AB_IN].astype(BF16),
                          _head_params(a_q_norm[l], a_k_norm[l], b_q_norm[l], b_k_norm[l]), cos_a, sin_a)
        w_c = jnp.pad(w[:, AB_IN:], ((0, 0), (0, C_ROPE))).astype(BF16)
        q_up = c_q_up[l].reshape(C_Q_RANK, C_HEADS, C_QK)
        q_up = jnp.concatenate([
            q_up[:, :, :C_NOPE].reshape(C_Q_RANK, C_HEADS * C_NOPE),
            jnp.pad(q_up[:, :, C_NOPE:], ((0, 0), (0, 0), (0, V7X_LANES - C_ROPE))).reshape(C_Q_RANK, -1),
        ], axis=-1).astype(BF16)
        split = lambda gvec: jnp.stack([gvec[:C_NOPE], jnp.pad(gvec[C_NOPE:], (0, V7X_LANES - C_ROPE))]).astype(F32)
        qc, kc, vc = _proj_c(h, w_c, row(c_q_a_norm[l]), row(c_kv_a_norm[l]), q_up, c_kv_up[l].astype(BF16),
                             split(c_q_norm[l]), split(c_k_norm[l]), cos_c, sin_c)

        oa = _attention(qkv, qkv, qkv, kv_heads=A_KV_HEADS, group=A_HEADS // A_KV_HEADS,
                        k_head0=A_HEADS, v_head0=A_HEADS + A_KV_HEADS, tq=128, scale=HEAD_DIM ** -0.5)
        oc = _attention(qc, kc, vc, kv_heads=C_HEADS, group=1, k_head0=0, v_head0=0, tq=512,
                        scale=C_QK ** -0.5)
        b_out = _attn_b(qkv, coef, tab)

        x, h = _mix_out(x, oa, b_out[:3], b_out[3:], oc, row(out_norm[l]), w_out[l].astype(BF16),
                        row(ffn2_norm[l]))
        w_gu, w_down = ffn2_w_gu[l].astype(BF16), ffn2_w_down[l].astype(BF16)
        if l + 1 < depth:
            x, h = _ffn(x, h, w_gu, w_down, row(ffn1_norm[l + 1]))
        else:
            x, = _ffn(x, h, w_gu, w_down)
    return x.reshape(nb, s, d)
```

```python
import functools

import jax
import jax.numpy as jnp
from jax import lax
from jax.experimental import pallas as pl
from jax.experimental.pallas import tpu as pltpu

F32 = jnp.float32
BF16 = jnp.bfloat16

D_MODEL = 2048
SEQ = 4096
GRID_W = 64
HEAD_DIM = 128
ROPE_THETA = 10000.0
EPS = 1e-6
NEG_BIG = -1e30
LOG2_E = 1.4426950408889634

A_HEADS = 8
A_KV_HEADS = 2
B_CONFIGS = ((128, 1), (512, 4), (2048, 16))
B_SLOTS = 4
B_HEADS = B_SLOTS * len(B_CONFIGS)
C_HEADS = 4
C_Q_RANK = 512
C_KV_RANK = 256
C_NOPE = 128
C_ROPE = 64
C_QK = C_NOPE + C_ROPE
D_FF = 5632

A_Q = A_HEADS * HEAD_DIM
A_KV = A_KV_HEADS * HEAD_DIM
A_IN = A_Q + 2 * A_KV
B_QKV = B_HEADS * HEAD_DIM
B_IN = 3 * B_QKV
AB_IN = A_IN + B_IN
AB_HEADS = AB_IN // HEAD_DIM
A_OUT = A_HEADS * HEAD_DIM
B_OUT = B_SLOTS * HEAD_DIM
C_OUT = C_HEADS * HEAD_DIM

V7X_LANES = 128
V7X_VMEM_LIMIT = 56 * 1024 * 1024

TM_FFN = 512
TF_FFN = 512
TM_PROJ = 512
TN_PROJ = 512
TM_OUT = 256
TQ_B = 256
B_SPAN = 64
TK_ATT = 512


def _params(*sem):
    return pltpu.CompilerParams(dimension_semantics=sem, vmem_limit_bytes=V7X_VMEM_LIMIT)


def _rms(x, gain, width):
    ms = jnp.sum(x * x, axis=-1, keepdims=True) * (1.0 / width)
    return (x * lax.rsqrt(ms + EPS)) * gain


def _rotate_half_pairs(y):
    lane = lax.broadcasted_iota(jnp.int32, y.shape, 1)
    first = (lane % 64) < 32
    return jnp.where(first, pltpu.roll(y, V7X_LANES - 32, 1), pltpu.roll(y, 32, 1))


def _rope(y, cos, sin_signed):
    return y * cos + _rotate_half_pairs(y) * sin_signed


def _norm_kernel(x_ref, g_ref, h_ref):
    h_ref[...] = _rms(x_ref[...], g_ref[...], D_MODEL).astype(BF16)


def _norm(x, gain):
    t = x.shape[0]
    return pl.pallas_call(
        _norm_kernel,
        grid=(t // TM_FFN,),
        in_specs=[pl.BlockSpec((TM_FFN, D_MODEL), lambda i: (i, 0)),
                  pl.BlockSpec((1, D_MODEL), lambda i: (0, 0))],
        out_specs=pl.BlockSpec((TM_FFN, D_MODEL), lambda i: (i, 0)),
        out_shape=jax.ShapeDtypeStruct((t, D_MODEL), BF16),
        compiler_params=_params("parallel"),
        name="norm",
    )(x, gain)


def _ffn_kernel(x_ref, h_ref, wg_ref, wu_ref, wd_ref, *rest, emit_next):
    if emit_next:
        gn_ref, xo_ref, ho_ref, acc_ref = rest
    else:
        xo_ref, acc_ref = rest
    j = pl.program_id(1)

    @pl.when(j == 0)
    def _():
        acc_ref[...] = jnp.zeros_like(acc_ref)

    h = h_ref[...]
    g = jnp.dot(h, wg_ref[...], preferred_element_type=F32)
    u = jnp.dot(h, wu_ref[...], preferred_element_type=F32)
    a = (g * (1.0 / (1.0 + jnp.exp(-g))) * u).astype(BF16)
    acc_ref[...] += jnp.dot(a, wd_ref[...], preferred_element_type=F32)

    @pl.when(j == pl.num_programs(1) - 1)
    def _():
        xn = x_ref[...] + 0.5 * acc_ref[...]
        xo_ref[...] = xn
        if emit_next:
            ho_ref[...] = _rms(xn, gn_ref[...], D_MODEL).astype(BF16)


def _ffn(x, h, w_gu, w_down, gain_next=None):
    t = x.shape[0]
    nf = D_FF // TF_FFN
    emit_next = gain_next is not None
    tok = pl.BlockSpec((TM_FFN, D_MODEL), lambda i, j: (i, 0))
    in_specs = [tok, tok,
                pl.BlockSpec((D_MODEL, TF_FFN), lambda i, j: (0, j)),
                pl.BlockSpec((D_MODEL, TF_FFN), lambda i, j: (0, j + nf)),
                pl.BlockSpec((TF_FFN, D_MODEL), lambda i, j: (j, 0))]
    args = [x, h, w_gu, w_gu, w_down]
    out_specs = [tok]
    out_shape = [jax.ShapeDtypeStruct((t, D_MODEL), F32)]
    if emit_next:
        in_specs.append(pl.BlockSpec((1, D_MODEL), lambda i, j: (0, 0)))
        args.append(gain_next)
        out_specs.append(tok)
        out_shape.append(jax.ShapeDtypeStruct((t, D_MODEL), BF16))
    return pl.pallas_call(
        functools.partial(_ffn_kernel, emit_next=emit_next),
        grid=(t // TM_FFN, nf),
        in_specs=in_specs,
        out_specs=out_specs,
        out_shape=out_shape,
        scratch_shapes=[pltpu.VMEM((TM_FFN, D_MODEL), F32)],
        compiler_params=_params("parallel", "arbitrary"),
        name="ffn",
    )(*args)


def _proj_heads_kernel(h_ref, w_ref, p_ref, cos_ref, sin_ref, o_ref, *, rope_steps):
    j = pl.program_id(1)
    z = jnp.dot(h_ref[...], w_ref[...], preferred_element_type=F32)
    heads = TN_PROJ // HEAD_DIM

    def epilogue(rope):
        for hh in range(heads):
            zz = z[:, hh * HEAD_DIM:(hh + 1) * HEAD_DIM]
            gain = p_ref[hh, 0:1, :]
            norm_on = p_ref[hh, 1:2, :] > 0.0
            ms = jnp.sum(zz * zz, axis=-1, keepdims=True) * (1.0 / HEAD_DIM)
            r = jnp.where(norm_on, lax.rsqrt(ms + EPS), 1.0)
            y = (zz * r) * gain
            if rope:
                rope_on = p_ref[hh, 2:3, :] > 0.0
                y = jnp.where(rope_on, _rope(y, cos_ref[...], sin_ref[...]), y)
            o_ref[hh] = y.astype(BF16)

    @pl.when(j < rope_steps)
    def _():
        epilogue(True)

    @pl.when(j >= rope_steps)
    def _():
        epilogue(False)


def _proj_heads(h, w_ab, head_params, cos, sin):
    t = h.shape[0]
    heads = TN_PROJ // HEAD_DIM
    seq_tiles = SEQ // TM_PROJ
    rope_steps = -(-(A_Q + A_KV) // TN_PROJ)
    return pl.pallas_call(
        functools.partial(_proj_heads_kernel, rope_steps=rope_steps),
        grid=(t // TM_PROJ, AB_IN // TN_PROJ),
        in_specs=[pl.BlockSpec((TM_PROJ, D_MODEL), lambda i, j: (i, 0)),
                  pl.BlockSpec((D_MODEL, TN_PROJ), lambda i, j: (0, j)),
                  pl.BlockSpec((heads, 8, HEAD_DIM), lambda i, j: (j, 0, 0)),
                  pl.BlockSpec((TM_PROJ, HEAD_DIM), lambda i, j: (i % seq_tiles, 0)),
                  pl.BlockSpec((TM_PROJ, HEAD_DIM), lambda i, j: (i % seq_tiles, 0))],
        out_specs=pl.BlockSpec((heads, TM_PROJ, HEAD_DIM), lambda i, j: (j, i, 0)),
        out_shape=jax.ShapeDtypeStruct((AB_HEADS, t, HEAD_DIM), BF16),
        compiler_params=_params("parallel", "arbitrary"),
        name="proj_heads",
    )(h, w_ab, head_params, cos, sin)


def _proj_c_kernel(h_ref, wc_ref, gqa_ref, gkva_ref, qup_ref, kvup_ref, gq_ref, gk_ref,
                   cos_ref, sin_ref, q_ref, k_ref, v_ref):
    z = jnp.dot(h_ref[...], wc_ref[...], preferred_element_type=F32)
    q_lat = _rms(z[:, :C_Q_RANK], gqa_ref[...], C_Q_RANK).astype(BF16)
    kv_lat = _rms(z[:, C_Q_RANK:C_Q_RANK + C_KV_RANK], gkva_ref[...], C_KV_RANK).astype(BF16)
    k_rope = z[:, C_Q_RANK + C_KV_RANK:]
    cq = jnp.dot(q_lat, qup_ref[...], preferred_element_type=F32)
    ckv = jnp.dot(kv_lat, kvup_ref[...], preferred_element_type=F32)
    cos = cos_ref[...]
    sin = sin_ref[...]
    nope_w = C_HEADS * C_NOPE
    for hh in range(C_HEADS):
        qn = cq[:, hh * C_NOPE:(hh + 1) * C_NOPE]
        qr = cq[:, nope_w + hh * V7X_LANES:nope_w + (hh + 1) * V7X_LANES]
        ms = (jnp.sum(qn * qn, axis=-1, keepdims=True)
              + jnp.sum(qr * qr, axis=-1, keepdims=True)) * (1.0 / C_QK)
        r = lax.rsqrt(ms + EPS)
        q_ref[hh, :, 0:C_NOPE] = ((qn * r) * gq_ref[0:1, :]).astype(BF16)
        q_ref[hh, :, C_NOPE:] = _rope((qr * r) * gq_ref[1:2, :], cos, sin).astype(BF16)

        kn = ckv[:, hh * 2 * C_NOPE:hh * 2 * C_NOPE + C_NOPE]
        ms = (jnp.sum(kn * kn, axis=-1, keepdims=True)
              + jnp.sum(k_rope * k_rope, axis=-1, keepdims=True)) * (1.0 / C_QK)
        r = lax.rsqrt(ms + EPS)
        k_ref[hh, :, 0:C_NOPE] = ((kn * r) * gk_ref[0:1, :]).astype(BF16)
        k_ref[hh, :, C_NOPE:] = _rope((k_rope * r) * gk_ref[1:2, :], cos, sin).astype(BF16)
        v_ref[hh] = ckv[:, hh * 2 * C_NOPE + C_NOPE:(hh + 1) * 2 * C_NOPE].astype(BF16)


def _proj_c(h, w_c, gqa, gkva, q_up, kv_up, gq, gk, cos, sin):
    t = h.shape[0]
    seq_tiles = SEQ // TM_PROJ
    full = lambda a: pl.BlockSpec(a.shape, lambda i: (0,) * a.ndim)
    qk_w = 2 * V7X_LANES
    return pl.pallas_call(
        _proj_c_kernel,
        grid=(t // TM_PROJ,),
        in_specs=[pl.BlockSpec((TM_PROJ, D_MODEL), lambda i: (i, 0)),
                  full(w_c), full(gqa), full(gkva), full(q_up), full(kv_up), full(gq), full(gk),
                  pl.BlockSpec((TM_PROJ, V7X_LANES), lambda i: (i % seq_tiles, 0)),
                  pl.BlockSpec((TM_PROJ, V7X_LANES), lambda i: (i % seq_tiles, 0))],
        out_specs=[pl.BlockSpec((C_HEADS, TM_PROJ, qk_w), lambda i: (0, i, 0)),
                   pl.BlockSpec((C_HEADS, TM_PROJ, qk_w), lambda i: (0, i, 0)),
                   pl.BlockSpec((C_HEADS, TM_PROJ, C_NOPE), lambda i: (0, i, 0))],
        out_shape=[jax.ShapeDtypeStruct((C_HEADS, t, qk_w), BF16),
                   jax.ShapeDtypeStruct((C_HEADS, t, qk_w), BF16),
                   jax.ShapeDtypeStruct((C_HEADS, t, C_NOPE), BF16)],
        compiler_params=_params("parallel"),
        name="proj_c",
    )(h, w_c, gqa, gkva, q_up, kv_up, gq, gk, cos, sin)


def _attn_kernel(q_ref, k_ref, v_ref, o_ref, s_ref, *, scale):
    g, tq, dq = q_ref.shape
    rows = g * tq
    dv = v_ref.shape[-1]
    q = q_ref[...].reshape(rows, dq)
    contract_last = (((1,), (1,)), ((), ()))
    contract_first = (((0,), (0,)), ((), ()))
    m = None
    for c in range(SEQ // TK_ATT):
        keys = slice(c * TK_ATT, (c + 1) * TK_ATT)
        s = lax.dot_general(k_ref[0, keys, :], q, contract_last, preferred_element_type=F32)
        s_ref[keys, :] = s
        part = jnp.max(s, axis=0, keepdims=True)
        m = part if m is None else jnp.maximum(m, part)
    factor = scale * LOG2_E
    lsum = jnp.zeros((1, rows), F32)
    acc = jnp.zeros((dv, rows), F32)
    for c in range(SEQ // TK_ATT):
        keys = slice(c * TK_ATT, (c + 1) * TK_ATT)
        p = jnp.exp2((s_ref[keys, :] - m) * factor)
        lsum = lsum + jnp.sum(p, axis=0, keepdims=True)
        acc = acc + lax.dot_general(v_ref[0, keys, :], p.astype(BF16), contract_first,
                                    preferred_element_type=F32)
    o = (acc / lsum).T
    o_ref[...] = o.reshape(g, tq, dv).astype(o_ref.dtype)


def _attention(q, k, v, *, kv_heads, group, k_head0, v_head0, tq, scale):
    _, t, dq = q.shape
    dv = v.shape[-1]
    nb = t // SEQ
    qt = SEQ // tq
    rows = group * tq
    return pl.pallas_call(
        functools.partial(_attn_kernel, scale=scale),
        grid=(nb, kv_heads, qt),
        in_specs=[pl.BlockSpec((group, tq, dq), lambda b, h, i: (h, b * qt + i, 0)),
                  pl.BlockSpec((1, SEQ, dq), lambda b, h, i: (k_head0 + h, b, 0)),
                  pl.BlockSpec((1, SEQ, dv), lambda b, h, i: (v_head0 + h, b, 0))],
        out_specs=pl.BlockSpec((group, tq, dv), lambda b, h, i: (h, b * qt + i, 0)),
        out_shape=jax.ShapeDtypeStruct((kv_heads * group, t, dv), BF16),
        scratch_shapes=[pltpu.VMEM((SEQ, rows), F32)],
        compiler_params=_params("parallel", "parallel", "arbitrary"),
        name="attention",
    )(q, k, v)


def _attn_b_kernel(q0, k0, v0, q1, k1, v1, q2, k2, v2, coef_ref, tab_ref, o_ref,
                   stage_ref, qd_ref, kd_ref, vd_ref, og_ref, eg_ref):
    scale = HEAD_DIM ** -0.5
    inputs = ((q0, k0, v0), (q1, k1, v1), (q2, k2, v2))
    chunk = 64

    def gather_classes(src_ref, dst_ref, dil):
        length = SEQ // dil
        stage_ref[...] = src_ref[0].astype(F32)

        def body(c, carry):
            for r in range(dil):
                src = pl.ds(pl.multiple_of(c * chunk * dil, chunk * dil) + r, chunk, stride=dil)
                dst = pl.ds(pl.multiple_of(r * length + c * chunk, chunk), chunk)
                dst_ref[dst, :] = stage_ref[src, :].astype(BF16)
            return carry

        lax.fori_loop(0, length // chunk, body, 0)

    def tile(g, q, k, v, steps, coef, out_rows):
        s = lax.dot_general(q, k, (((1,), (1,)), ((), ())), preferred_element_type=F32)
        s = s * scale - coef * steps
        m = jnp.max(s, axis=-1, keepdims=True)
        p = jnp.exp(s - m)
        l = jnp.sum(p, axis=-1, keepdims=True)
        o = jnp.dot(p.astype(BF16), v, preferred_element_type=F32) / l
        og_ref[g, out_rows, :] = o
        eg_ref[g, out_rows, :] = jnp.broadcast_to(m + jnp.log(l), (TQ_B, HEAD_DIM))

    for g, (_, dil) in enumerate(B_CONFIGS):
        length = SEQ // dil
        nt = length // TQ_B
        coef = coef_ref[0, g:g + 1, 0:1]
        if dil == 1:
            qs, ks, vs = (lambda rows, ref=ref: ref[0, rows, :] for ref in inputs[g])
        else:
            for src, dst in zip(inputs[g], (qd_ref, kd_ref, vd_ref)):
                gather_classes(src, dst, dil)
            qs, ks, vs = (lambda rows, ref=ref: ref[rows, :] for ref in (qd_ref, kd_ref, vd_ref))

        def residue(r, g=g, dil=dil, length=length, nt=nt, coef=coef, qs=qs, ks=ks, vs=vs):
            base = r * length
            for ti in range(nt):
                l0 = ti * TQ_B
                if nt == 1:
                    k0_, width, steps = 0, TQ_B, tab_ref[0, :, 0:TQ_B]
                else:
                    width = 2 * TQ_B
                    k0_ = min(max(l0 - TQ_B // 2, 0), length - width)
                    steps = tab_ref[(l0 - k0_) // (TQ_B // 2)]
                if dil == 1:
                    q_rows, k_rows, out_rows = pl.ds(l0, TQ_B), pl.ds(k0_, width), pl.ds(l0, TQ_B)
                else:
                    q_rows = pl.ds(pl.multiple_of(base + l0, TQ_B), TQ_B)
                    k_rows = pl.ds(pl.multiple_of(base + k0_, TQ_B // 2), width)
                    out_rows = pl.ds(l0 * dil + r, TQ_B, stride=dil)
                tile(g, qs(q_rows), ks(k_rows), vs(k_rows), steps, coef, out_rows)

        if dil == 1:
            residue(0)
        else:
            per_step = max(1, 4 // nt)

            def step(it, carry, residue=residue, per_step=per_step):
                for rr in range(per_step):
                    residue(it * per_step + rr)
                return carry

            lax.fori_loop(0, dil // per_step, step, 0)

    def merge(c, carry):
        rows = pl.ds(pl.multiple_of(c * TQ_B, TQ_B), TQ_B)
        lse = [eg_ref[g, rows, :] for g in range(len(B_CONFIGS))]
        mx = jnp.maximum(jnp.maximum(lse[0], lse[1]), lse[2])
        w = [jnp.exp(e - mx) for e in lse]
        den = w[0] + w[1] + w[2]
        o_ref[0, rows, :] = sum((w[g] / den) * og_ref[g, rows, :] for g in range(len(B_CONFIGS))).astype(o_ref.dtype)
        return carry

    lax.fori_loop(0, SEQ // TQ_B, merge, 0)


def _attn_b(qkv_heads, coef, tab):
    t = qkv_heads.shape[1]
    nb = t // SEQ
    base = A_IN // HEAD_DIM
    in_specs = []
    for g in range(len(B_CONFIGS)):
        for kind in range(3):
            head0 = base + kind * B_HEADS + g * B_SLOTS
            in_specs.append(pl.BlockSpec((1, SEQ, HEAD_DIM), lambda b, j, head0=head0: (head0 + j, b, 0)))
    in_specs.append(pl.BlockSpec((1, 8, V7X_LANES), lambda b, j: (j, 0, 0)))
    in_specs.append(pl.BlockSpec(tab.shape, lambda b, j: (0, 0, 0)))
    return pl.pallas_call(
        _attn_b_kernel,
        grid=(nb, B_SLOTS),
        in_specs=in_specs,
        out_specs=pl.BlockSpec((1, SEQ, HEAD_DIM), lambda b, j: (j, b, 0)),
        out_shape=jax.ShapeDtypeStruct((B_SLOTS, t, HEAD_DIM), BF16),
        scratch_shapes=[pltpu.VMEM((SEQ, HEAD_DIM), F32)]
                       + [pltpu.VMEM((SEQ, HEAD_DIM), BF16)] * 3
                       + [pltpu.VMEM((len(B_CONFIGS), SEQ, HEAD_DIM), F32)] * 2,
        compiler_params=_params("parallel", "parallel"),
        name="attn_b",
    )(*([qkv_heads] * 9), coef, tab)


def _mix_out_kernel(x_ref, oa_ref, ob_ref, oc_ref, gout_ref, w_ref, gn_ref, xo_ref, ho_ref, y_ref):
    def group_norm(tiles, col0):
        width = len(tiles) * HEAD_DIM
        ssq = sum(jnp.sum(tl * tl, axis=-1, keepdims=True) for tl in tiles)
        r = lax.rsqrt(ssq * (1.0 / width) + EPS)
        for n, tl in enumerate(tiles):
            c = col0 + n * HEAD_DIM
            y_ref[:, c:c + HEAD_DIM] = ((tl * r) * gout_ref[:, c:c + HEAD_DIM]).astype(BF16)

    group_norm([oa_ref[n].astype(F32) for n in range(A_HEADS)], 0)
    group_norm([ob_ref[n].astype(F32) for n in range(B_SLOTS)], A_OUT)
    group_norm([oc_ref[n].astype(F32) for n in range(C_HEADS)], A_OUT + B_OUT)

    xn = x_ref[...] + jnp.dot(y_ref[...], w_ref[...], preferred_element_type=F32)
    xo_ref[...] = xn
    ho_ref[...] = _rms(xn, gn_ref[...], D_MODEL).astype(BF16)


def _mix_out(x, oa, ob, oc, gain_out, w_out, gain_next):
    t = x.shape[0]
    heads = lambda n: pl.BlockSpec((n, TM_OUT, HEAD_DIM), lambda i: (0, i, 0))
    row = pl.BlockSpec((1, D_MODEL), lambda i: (0, 0))
    tok = pl.BlockSpec((TM_OUT, D_MODEL), lambda i: (i, 0))
    return pl.pallas_call(
        _mix_out_kernel,
        grid=(t // TM_OUT,),
        in_specs=[tok, heads(A_HEADS), heads(B_SLOTS), heads(C_HEADS), row,
                  pl.BlockSpec((D_MODEL, D_MODEL), lambda i: (0, 0)), row],
        out_specs=[tok, tok],
        out_shape=[jax.ShapeDtypeStruct((t, D_MODEL), F32),
                   jax.ShapeDtypeStruct((t, D_MODEL), BF16)],
        scratch_shapes=[pltpu.VMEM((TM_OUT, D_MODEL), BF16)],
        compiler_params=_params("parallel"),
        name="mix_out",
    )(x, oa, ob, oc, gain_out, w_out, gain_next)


def _rope_tables(pos_a, pos_b):
    half = 32
    inv = ROPE_THETA ** (-jnp.arange(half, dtype=F32) / half)

    def one(pos):
        ang = pos.astype(F32)[:, None] * inv[None, :]
        c, s = jnp.cos(ang), jnp.sin(ang)
        return jnp.concatenate([c, c], axis=-1), jnp.concatenate([-s, s], axis=-1)

    ca, sa = one(pos_a)
    cb, sb = one(pos_b)
    return jnp.concatenate([ca, cb], axis=-1), jnp.concatenate([sa, sb], axis=-1)


def _band_steps_table():
    i = jnp.arange(TQ_B)[:, None]
    jj = jnp.arange(2 * TQ_B)[None, :]
    tabs = []
    for off in (0, TQ_B // 2, TQ_B):
        rel = jnp.abs(off + i - jj)
        tabs.append(jnp.where(rel <= B_SPAN, rel.astype(F32), -NEG_BIG))
    return jnp.stack(tabs)


def _head_params(gq_a, gk_a, gq_b, gk_b):
    ones = jnp.ones((HEAD_DIM,), F32)
    rows = ([(gq_a, 1.0, 1.0)] * A_HEADS + [(gk_a, 1.0, 1.0)] * A_KV_HEADS + [(ones, 0.0, 0.0)] * A_KV_HEADS
            + [(gq_b, 1.0, 0.0)] * B_HEADS + [(gk_b, 1.0, 0.0)] * B_HEADS + [(ones, 0.0, 0.0)] * B_HEADS)
    out = []
    for gain, norm_on, rope_on in rows:
        blk = jnp.zeros((8, HEAD_DIM), F32)
        blk = blk.at[0].set(gain.astype(F32)).at[1].set(norm_on).at[2].set(rope_on)
        out.append(blk)
    return jnp.stack(out)


def kernel(x, ffn1_norm, ffn1_w_gu, ffn1_w_down, mix_norm, w_in, a_q_norm, a_k_norm, b_q_norm, b_k_norm, c_q_a_norm, c_q_up, c_kv_a_norm, c_kv_up, c_q_norm, c_k_norm, out_norm, w_out, ffn2_norm, ffn2_w_gu, ffn2_w_down):
    nb, s, d = x.shape
    assert (s, d) == (SEQ, D_MODEL)
    depth = w_in.shape[0]
    t = nb * s
    x = x.reshape(t, d)
    row = lambda v: v.reshape(1, -1).astype(F32)

    pos = jnp.arange(s, dtype=jnp.int32)
    cos_a, sin_a = _rope_tables(pos // GRID_W, pos % GRID_W)
    cos_c, sin_c = _rope_tables(pos, pos)
    tab = _band_steps_table()
    slopes = 2.0 ** (-8.0 * jnp.arange(1, B_HEADS + 1, dtype=F32) / B_HEADS)
    dil = jnp.array([c[1] for c in B_CONFIGS], F32)
    coef = (slopes.reshape(len(B_CONFIGS), B_SLOTS) * dil[:, None]).T
    coef = jnp.zeros((B_SLOTS, 8, V7X_LANES), F32).at[:, :len(B_CONFIGS), :].set(coef[:, :, None])

    h = _norm(x, row(ffn1_norm[0]))
    for l in range(depth):
        x, h = _ffn(x, h, ffn1_w_gu[l].astype(BF16), ffn1_w_down[l].astype(BF16), row(mix_norm[l]))

        w = w_in[l]
        qkv = _proj_heads(h, w[:, :AB_IN].astype(BF16),
                          _head_params(a_q_norm[l], a_k_norm[l], b_q_norm[l], b_k_norm[l]), cos_a, sin_a)
        w_c = jnp.pad(w[:, AB_IN:], ((0, 0), (0, C_ROPE))).astype(BF16)
        q_up = c_q_up[l].reshape(C_Q_RANK, C_HEADS, C_QK)
        q_up = jnp.concatenate([
            q_up[:, :, :C_NOPE].reshape(C_Q_RANK, C_HEADS * C_NOPE),
            jnp.pad(q_up[:, :, C_NOPE:], ((0, 0), (0, 0), (0, V7X_LANES - C_ROPE))).reshape(C_Q_RANK, -1),
        ], axis=-1).astype(BF16)
        split = lambda gvec: jnp.stack([gvec[:C_NOPE], jnp.pad(gvec[C_NOPE:], (0, V7X_LANES - C_ROPE))]).astype(F32)
        qc, kc, vc = _proj_c(h, w_c, row(c_q_a_norm[l]), row(c_kv_a_norm[l]), q_up, c_kv_up[l].astype(BF16),
                             split(c_q_norm[l]), split(c_k_norm[l]), cos_c, sin_c)

        oa = _attention(qkv, qkv, qkv, kv_heads=A_KV_HEADS, group=A_HEADS // A_KV_HEADS,
                        k_head0=A_HEADS, v_head0=A_HEADS + A_KV_HEADS, tq=128, scale=HEAD_DIM ** -0.5)
        oc = _attention(qc, kc, vc, kv_heads=C_HEADS, group=1, k_head0=0, v_head0=0, tq=512,
                        scale=C_QK ** -0.5)
        ob = _attn_b(qkv, coef, tab)

        x, h = _mix_out(x, oa, ob, oc, row(out_norm[l]), w_out[l].astype(BF16), row(ffn2_norm[l]))
        w_gu, w_down = ffn2_w_gu[l].astype(BF16), ffn2_w_down[l].astype(BF16)
        if l + 1 < depth:
            x, h = _ffn(x, h, w_gu, w_down, row(ffn1_norm[l + 1]))
        else:
            x, = _ffn(x, h, w_gu, w_down)
    return x.reshape(nb, s, d)
```

```python
import functools

import jax
import jax.numpy as jnp
from jax import lax
from jax.experimental import pallas as pl
from jax.experimental.pallas import tpu as pltpu

F32 = jnp.float32
BF16 = jnp.bfloat16

D_MODEL = 2048
SEQ = 4096
GRID_W = 64
HEAD_DIM = 128
ROPE_THETA = 10000.0
EPS = 1e-6
NEG_BIG = -1e30
LOG2_E = 1.4426950408889634

A_HEADS = 8
A_KV_HEADS = 2
B_CONFIGS = ((128, 1), (512, 4), (2048, 16))
B_SLOTS = 4
B_HEADS = B_SLOTS * len(B_CONFIGS)
C_HEADS = 4
C_Q_RANK = 512
C_KV_RANK = 256
C_NOPE = 128
C_ROPE = 64
C_QK = C_NOPE + C_ROPE
D_FF = 5632

A_Q = A_HEADS * HEAD_DIM
A_KV = A_KV_HEADS * HEAD_DIM
A_IN = A_Q + 2 * A_KV
B_QKV = B_HEADS * HEAD_DIM
B_IN = 3 * B_QKV
AB_IN = A_IN + B_IN
W_IN = AB_IN + C_Q_RANK + C_KV_RANK + C_ROPE
AB_HEADS = AB_IN // HEAD_DIM
A_OUT = A_HEADS * HEAD_DIM
B_OUT = B_SLOTS * HEAD_DIM
C_OUT = C_HEADS * HEAD_DIM

V7X_LANES = 128
V7X_VMEM_LIMIT = 56 * 1024 * 1024

TM_FFN = 512
TF_FFN = 512
TM_PROJ = 512
TN_PROJ = 512
TM_OUT = 256
TQ_B = 256
B_SPAN = 64
TK_ATT = 512


def _params(*sem):
    return pltpu.CompilerParams(dimension_semantics=sem, vmem_limit_bytes=V7X_VMEM_LIMIT)


def _rms(x, gain, width):
    ms = jnp.sum(x * x, axis=-1, keepdims=True) * (1.0 / width)
    return (x * lax.rsqrt(ms + EPS)) * gain


ROPE_HALF = 32
ROPE_LANE_ORDER = tuple(list(range(0, 32)) + list(range(64, 96)) + list(range(32, 64)) + list(range(96, 128)))


def _rope(y, cos, sin_signed):
    return y * cos + pltpu.roll(y, V7X_LANES // 2, 1) * sin_signed


def _norm_kernel(x_ref, g_ref, h_ref):
    h_ref[...] = _rms(x_ref[...], g_ref[...], D_MODEL).astype(BF16)


def _norm(x, gain):
    t = x.shape[0]
    return pl.pallas_call(
        _norm_kernel,
        grid=(t // TM_FFN,),
        in_specs=[pl.BlockSpec((TM_FFN, D_MODEL), lambda i: (i, 0)),
                  pl.BlockSpec((1, D_MODEL), lambda i: (0, 0))],
        out_specs=pl.BlockSpec((TM_FFN, D_MODEL), lambda i: (i, 0)),
        out_shape=jax.ShapeDtypeStruct((t, D_MODEL), BF16),
        compiler_params=_params("parallel"),
        name="norm",
    )(x, gain)


def _ffn_kernel(x_ref, h_ref, wg_ref, wu_ref, wd_ref, *rest, emit_next):
    if emit_next:
        gn_ref, xo_ref, ho_ref, acc_ref = rest
    else:
        xo_ref, acc_ref = rest
    j = pl.program_id(1)

    @pl.when(j == 0)
    def _():
        acc_ref[...] = jnp.zeros_like(acc_ref)

    h = h_ref[...]
    g = jnp.dot(h, wg_ref[...], preferred_element_type=F32)
    u = jnp.dot(h, wu_ref[...], preferred_element_type=F32)
    a = (g * (1.0 / (1.0 + jnp.exp(-g))) * u).astype(BF16)
    acc_ref[...] += jnp.dot(a, wd_ref[...], preferred_element_type=F32)

    @pl.when(j == pl.num_programs(1) - 1)
    def _():
        xn = x_ref[...] + 0.5 * acc_ref[...]
        xo_ref[...] = xn
        if emit_next:
            ho_ref[...] = _rms(xn, gn_ref[...], D_MODEL).astype(BF16)


def _ffn(x, h, w_gu, w_down, gain_next=None):
    t = x.shape[0]
    nf = D_FF // TF_FFN
    emit_next = gain_next is not None
    tok = pl.BlockSpec((TM_FFN, D_MODEL), lambda i, j: (i, 0))
    in_specs = [tok, tok,
                pl.BlockSpec((D_MODEL, TF_FFN), lambda i, j: (0, j)),
                pl.BlockSpec((D_MODEL, TF_FFN), lambda i, j: (0, j + nf)),
                pl.BlockSpec((TF_FFN, D_MODEL), lambda i, j: (j, 0))]
    args = [x, h, w_gu, w_gu, w_down]
    out_specs = [tok]
    out_shape = [jax.ShapeDtypeStruct((t, D_MODEL), F32)]
    if emit_next:
        in_specs.append(pl.BlockSpec((1, D_MODEL), lambda i, j: (0, 0)))
        args.append(gain_next)
        out_specs.append(tok)
        out_shape.append(jax.ShapeDtypeStruct((t, D_MODEL), BF16))
    return pl.pallas_call(
        functools.partial(_ffn_kernel, emit_next=emit_next),
        grid=(t // TM_FFN, nf),
        in_specs=in_specs,
        out_specs=out_specs,
        out_shape=out_shape,
        scratch_shapes=[pltpu.VMEM((TM_FFN, D_MODEL), F32)],
        compiler_params=_params("parallel", "arbitrary"),
        name="ffn",
    )(*args)


HEAD_PLAIN, HEAD_NORM, HEAD_NORM_ROPE = 0, 1, 2
A_Q_SCALE = HEAD_DIM ** -0.5 * LOG2_E
C_Q_SCALE = C_QK ** -0.5 * LOG2_E
AB_HEAD_KINDS = (((HEAD_NORM_ROPE, A_Q_SCALE),) * A_HEADS + ((HEAD_NORM_ROPE, 1.0),) * A_KV_HEADS
                 + ((HEAD_PLAIN, 1.0),) * A_KV_HEADS
                 + ((HEAD_NORM, 1.0),) * (2 * B_HEADS) + ((HEAD_PLAIN, 1.0),) * B_HEADS)


def _proj_heads_kernel(h_ref, w_ref, g_ref, cos_ref, sin_ref, o_ref, *, step_kinds):
    j = pl.program_id(1)
    pair_w = 2 * HEAD_DIM

    def step(kinds):
        for pair in range(len(kinds) // 2):
            z = jnp.dot(h_ref[...], w_ref[:, pair * pair_w:(pair + 1) * pair_w], preferred_element_type=F32)
            for n in range(2):
                hh = 2 * pair + n
                kind, out_scale = kinds[hh]
                y = z[:, n * HEAD_DIM:(n + 1) * HEAD_DIM]
                if kind != HEAD_PLAIN:
                    y = _rms(y, g_ref[hh], HEAD_DIM)
                if kind == HEAD_NORM_ROPE:
                    y = _rope(y, cos_ref[...], sin_ref[...])
                if out_scale != 1.0:
                    y = y * out_scale
                o_ref[hh] = y.astype(BF16)

    lo = 0
    while lo < len(step_kinds):
        hi = lo
        while hi < len(step_kinds) and step_kinds[hi] == step_kinds[lo]:
            hi += 1
        pl.when((j >= lo) & (j < hi))(functools.partial(step, step_kinds[lo]))
        lo = hi


def _proj_heads(h, w_ab, gains, cos, sin):
    t = h.shape[0]
    heads = TN_PROJ // HEAD_DIM
    seq_tiles = SEQ // TM_PROJ
    step_kinds = tuple(AB_HEAD_KINDS[n:n + heads] for n in range(0, AB_HEADS, heads))
    return pl.pallas_call(
        functools.partial(_proj_heads_kernel, step_kinds=step_kinds),
        grid=(t // TM_PROJ, AB_IN // TN_PROJ),
        in_specs=[pl.BlockSpec((TM_PROJ, D_MODEL), lambda i, j: (i, 0)),
                  pl.BlockSpec((D_MODEL, TN_PROJ), lambda i, j: (0, j)),
                  pl.BlockSpec((heads, 1, HEAD_DIM), lambda i, j: (j, 0, 0)),
                  pl.BlockSpec((TM_PROJ, HEAD_DIM), lambda i, j: (i % seq_tiles, 0)),
                  pl.BlockSpec((TM_PROJ, HEAD_DIM), lambda i, j: (i % seq_tiles, 0))],
        out_specs=pl.BlockSpec((heads, TM_PROJ, HEAD_DIM), lambda i, j: (j, i, 0)),
        out_shape=jax.ShapeDtypeStruct((AB_HEADS, t, HEAD_DIM), BF16),
        compiler_params=_params("parallel", "arbitrary"),
        name="proj_heads",
    )(h, w_ab, gains, cos, sin)


def _proj_c_kernel(h_ref, wc_ref, gqa_ref, gkva_ref, qup_ref, kvup_ref, gq_ref, gk_ref,
                   cos_ref, sin_ref, q_ref, k_ref, v_ref):
    nope_w = C_HEADS * C_NOPE
    half = h_ref.shape[0] // 2
    for part in range(2):
        rows = slice(part * half, (part + 1) * half)
        h = h_ref[rows, :]
        cos = cos_ref[rows, :]
        sin = sin_ref[rows, :]

        zq = jnp.dot(h, wc_ref[:, :C_Q_RANK], preferred_element_type=F32)
        zkv = jnp.dot(h, wc_ref[:, C_Q_RANK:], preferred_element_type=F32)
        q_lat = _rms(zq, gqa_ref[...], C_Q_RANK).astype(BF16)
        cq = jnp.dot(q_lat, qup_ref[...], preferred_element_type=F32)
        kv_lat = _rms(zkv[:, :C_KV_RANK], gkva_ref[...], C_KV_RANK).astype(BF16)
        k_rope = zkv[:, C_KV_RANK:]
        ckv = jnp.dot(kv_lat, kvup_ref[...], preferred_element_type=F32)

        for hh in range(C_HEADS):
            qn = cq[:, hh * C_NOPE:(hh + 1) * C_NOPE]
            qr = cq[:, nope_w + hh * V7X_LANES:nope_w + (hh + 1) * V7X_LANES]
            ms = (jnp.sum(qn * qn, axis=-1, keepdims=True)
                  + jnp.sum(qr * qr, axis=-1, keepdims=True)) * (1.0 / C_QK)
            r = lax.rsqrt(ms + EPS)
            q_ref[hh, rows, 0:C_NOPE] = (((qn * r) * gq_ref[0:1, :]) * C_Q_SCALE).astype(BF16)
            q_ref[hh, rows, C_NOPE:] = (_rope((qr * r) * gq_ref[1:2, :], cos, sin) * C_Q_SCALE).astype(BF16)

        k_rope_sq = jnp.sum(k_rope * k_rope, axis=-1, keepdims=True)
        for hh in range(C_HEADS):
            kn = ckv[:, hh * 2 * C_NOPE:hh * 2 * C_NOPE + C_NOPE]
            ms = (jnp.sum(kn * kn, axis=-1, keepdims=True) + k_rope_sq) * (1.0 / C_QK)
            r = lax.rsqrt(ms + EPS)
            k_ref[hh, rows, 0:C_NOPE] = ((kn * r) * gk_ref[0:1, :]).astype(BF16)
            k_ref[hh, rows, C_NOPE:] = _rope((k_rope * r) * gk_ref[1:2, :], cos, sin).astype(BF16)
            v_ref[hh, rows, :] = ckv[:, hh * 2 * C_NOPE + C_NOPE:(hh + 1) * 2 * C_NOPE].astype(BF16)


def _proj_c(h, w_c, gqa, gkva, q_up, kv_up, gq, gk, cos, sin):
    t = h.shape[0]
    seq_tiles = SEQ // TM_PROJ
    full = lambda a: pl.BlockSpec(a.shape, lambda i: (0,) * a.ndim)
    qk_w = 2 * V7X_LANES
    return pl.pallas_call(
        _proj_c_kernel,
        grid=(t // TM_PROJ,),
        in_specs=[pl.BlockSpec((TM_PROJ, D_MODEL), lambda i: (i, 0)),
                  full(w_c), full(gqa), full(gkva), full(q_up), full(kv_up), full(gq), full(gk),
                  pl.BlockSpec((TM_PROJ, V7X_LANES), lambda i: (i % seq_tiles, 0)),
                  pl.BlockSpec((TM_PROJ, V7X_LANES), lambda i: (i % seq_tiles, 0))],
        out_specs=[pl.BlockSpec((C_HEADS, TM_PROJ, qk_w), lambda i: (0, i, 0)),
                   pl.BlockSpec((C_HEADS, TM_PROJ, qk_w), lambda i: (0, i, 0)),
                   pl.BlockSpec((C_HEADS, TM_PROJ, C_NOPE), lambda i: (0, i, 0))],
        out_shape=[jax.ShapeDtypeStruct((C_HEADS, t, qk_w), BF16),
                   jax.ShapeDtypeStruct((C_HEADS, t, qk_w), BF16),
                   jax.ShapeDtypeStruct((C_HEADS, t, C_NOPE), BF16)],
        compiler_params=_params("parallel"),
        name="proj_c",
    )(h, w_c, gqa, gkva, q_up, kv_up, gq, gk, cos, sin)


def _attn_kernel(q_ref, k_ref, v_ref, o_ref, s0_ref, s1_ref, m0_ref, m1_ref, *, tq):
    g, _, dq = q_ref.shape
    rows = g * tq
    dv = v_ref.shape[-1]
    tiles = SEQ // tq
    contract_last = (((1,), (1,)), ((), ()))
    contract_first = (((0,), (0,)), ((), ()))

    def q_rows(tile):
        start = tile * tq
        return pl.ds(start if isinstance(tile, int) else pl.multiple_of(start, tq), tq)

    def passes(score_tile, score_bufs, out_tile, out_bufs):
        if score_tile is not None:
            sw_ref, mw_ref = score_bufs
            q = q_ref[:, q_rows(score_tile), :].reshape(rows, dq)
            m_new = None
        if out_tile is not None:
            sr_ref, mr_ref = out_bufs
            m = mr_ref[...]
            lsum = jnp.zeros((1, rows), F32)
            acc = jnp.zeros((dv, rows), F32)
        for c in range(SEQ // TK_ATT):
            keys = slice(c * TK_ATT, (c + 1) * TK_ATT)
            if score_tile is not None:
                s = lax.dot_general(k_ref[0, keys, :], q, contract_last, preferred_element_type=F32)
                sw_ref[keys, :] = s
                part = jnp.max(s, axis=0, keepdims=True)
                m_new = part if m_new is None else jnp.maximum(m_new, part)
            if out_tile is not None:
                p = jnp.exp2(sr_ref[keys, :] - m)
                lsum = lsum + jnp.sum(p, axis=0, keepdims=True)
                acc = acc + lax.dot_general(v_ref[0, keys, :], p.astype(BF16), contract_first,
                                            preferred_element_type=F32)
        if score_tile is not None:
            mw_ref[...] = m_new
        if out_tile is not None:
            o = (acc / lsum).T
            o_ref[:, q_rows(out_tile), :] = o.reshape(g, tq, dv).astype(o_ref.dtype)

    buf0, buf1 = (s0_ref, m0_ref), (s1_ref, m1_ref)
    passes(0, buf0, None, None)

    def pair(n, carry):
        tile = 2 * n
        passes(tile + 1, buf1, tile, buf0)

        @pl.when(n < tiles // 2 - 1)
        def _():
            passes(tile + 2, buf0, tile + 1, buf1)

        @pl.when(n == tiles // 2 - 1)
        def _():
            passes(None, None, tile + 1, buf1)

        return carry

    lax.fori_loop(0, tiles // 2, pair, 0)


def _attention(q, k, v, *, kv_heads, group, k_head0, v_head0, tq):
    _, t, dq = q.shape
    dv = v.shape[-1]
    nb = t // SEQ
    rows = group * tq
    return pl.pallas_call(
        functools.partial(_attn_kernel, tq=tq),
        grid=(nb, kv_heads),
        in_specs=[pl.BlockSpec((group, SEQ, dq), lambda b, h: (h, b, 0)),
                  pl.BlockSpec((1, SEQ, dq), lambda b, h: (k_head0 + h, b, 0)),
                  pl.BlockSpec((1, SEQ, dv), lambda b, h: (v_head0 + h, b, 0))],
        out_specs=pl.BlockSpec((group, SEQ, dv), lambda b, h: (h, b, 0)),
        out_shape=jax.ShapeDtypeStruct((kv_heads * group, t, dv), BF16),
        scratch_shapes=[pltpu.VMEM((SEQ, rows), F32)] * 2 + [pltpu.VMEM((1, rows), F32)] * 2,
        compiler_params=_params("parallel", "parallel"),
        name="attention",
    )(q, k, v)


def _attn_b_kernel(q0, k0, v0, q1, k1, v1, q2, k2, v2, coef_ref, tab_ref, o_ref,
                   stage_ref, qd_ref, kd_ref, vd_ref, og_ref, eg_ref):
    scale = HEAD_DIM ** -0.5
    inputs = ((q0, k0, v0), (q1, k1, v1), (q2, k2, v2))
    chunk = 64

    def gather_classes(src_ref, dst_ref, dil):
        length = SEQ // dil
        stage_ref[...] = src_ref[0].astype(F32)

        def body(c, carry):
            for r in range(dil):
                src = pl.ds(pl.multiple_of(c * chunk * dil, chunk * dil) + r, chunk, stride=dil)
                dst = pl.ds(pl.multiple_of(r * length + c * chunk, chunk), chunk)
                dst_ref[dst, :] = stage_ref[src, :].astype(BF16)
            return carry

        lax.fori_loop(0, length // chunk, body, 0)

    def tile(g, q, k, v, steps, coef, out_rows):
        s = lax.dot_general(q, k, (((1,), (1,)), ((), ())), preferred_element_type=F32)
        s = s * scale - coef * steps
        m = jnp.max(s, axis=-1, keepdims=True)
        p = jnp.exp(s - m)
        l = jnp.sum(p, axis=-1, keepdims=True)
        o = jnp.dot(p.astype(BF16), v, preferred_element_type=F32) / l
        og_ref[g, out_rows, :] = o
        eg_ref[g, out_rows, :] = jnp.broadcast_to(m + jnp.log(l), (TQ_B, HEAD_DIM))

    for g, (_, dil) in enumerate(B_CONFIGS):
        length = SEQ // dil
        nt = length // TQ_B
        coef = coef_ref[0, g:g + 1, 0:1]
        if dil == 1:
            qs, ks, vs = (lambda rows, ref=ref: ref[0, rows, :] for ref in inputs[g])
        else:
            for src, dst in zip(inputs[g], (qd_ref, kd_ref, vd_ref)):
                gather_classes(src, dst, dil)
            qs, ks, vs = (lambda rows, ref=ref: ref[rows, :] for ref in (qd_ref, kd_ref, vd_ref))

        def residue(r, g=g, dil=dil, length=length, nt=nt, coef=coef, qs=qs, ks=ks, vs=vs):
            base = r * length
            for ti in range(nt):
                l0 = ti * TQ_B
                if nt == 1:
                    k0_, width, steps = 0, TQ_B, tab_ref[0, :, 0:TQ_B]
                else:
                    width = 2 * TQ_B
                    k0_ = min(max(l0 - TQ_B // 2, 0), length - width)
                    steps = tab_ref[(l0 - k0_) // (TQ_B // 2)]
                if dil == 1:
                    q_rows, k_rows, out_rows = pl.ds(l0, TQ_B), pl.ds(k0_, width), pl.ds(l0, TQ_B)
                else:
                    q_rows = pl.ds(pl.multiple_of(base + l0, TQ_B), TQ_B)
                    k_rows = pl.ds(pl.multiple_of(base + k0_, TQ_B // 2), width)
                    out_rows = pl.ds(l0 * dil + r, TQ_B, stride=dil)
                tile(g, qs(q_rows), ks(k_rows), vs(k_rows), steps, coef, out_rows)

        if dil == 1:
            residue(0)
        else:
            per_step = max(1, 4 // nt)

            def step(it, carry, residue=residue, per_step=per_step):
                for rr in range(per_step):
                    residue(it * per_step + rr)
                return carry

            lax.fori_loop(0, dil // per_step, step, 0)

    def merge(c, carry):
        rows = pl.ds(pl.multiple_of(c * TQ_B, TQ_B), TQ_B)
        lse = [eg_ref[g, rows, :] for g in range(len(B_CONFIGS))]
        mx = jnp.maximum(jnp.maximum(lse[0], lse[1]), lse[2])
        w = [jnp.exp(e - mx) for e in lse]
        den = w[0] + w[1] + w[2]
        o_ref[0, rows, :] = sum((w[g] / den) * og_ref[g, rows, :] for g in range(len(B_CONFIGS))).astype(o_ref.dtype)
        return carry

    lax.fori_loop(0, SEQ // TQ_B, merge, 0)


def _attn_b(qkv_heads, coef, tab):
    t = qkv_heads.shape[1]
    nb = t // SEQ
    base = A_IN // HEAD_DIM
    in_specs = []
    for g in range(len(B_CONFIGS)):
        for kind in range(3):
            head0 = base + kind * B_HEADS + g * B_SLOTS
            in_specs.append(pl.BlockSpec((1, SEQ, HEAD_DIM), lambda b, j, head0=head0: (head0 + j, b, 0)))
    in_specs.append(pl.BlockSpec((1, 8, V7X_LANES), lambda b, j: (j, 0, 0)))
    in_specs.append(pl.BlockSpec(tab.shape, lambda b, j: (0, 0, 0)))
    return pl.pallas_call(
        _attn_b_kernel,
        grid=(nb, B_SLOTS),
        in_specs=in_specs,
        out_specs=pl.BlockSpec((1, SEQ, HEAD_DIM), lambda b, j: (j, b, 0)),
        out_shape=jax.ShapeDtypeStruct((B_SLOTS, t, HEAD_DIM), BF16),
        scratch_shapes=[pltpu.VMEM((SEQ, HEAD_DIM), F32)]
                       + [pltpu.VMEM((SEQ, HEAD_DIM), BF16)] * 3
                       + [pltpu.VMEM((len(B_CONFIGS), SEQ, HEAD_DIM), F32)] * 2,
        compiler_params=_params("parallel", "parallel"),
        name="attn_b",
    )(*([qkv_heads] * 9), coef, tab)


def _mix_out_kernel(x_ref, oa_ref, ob_ref, oc_ref, gout_ref, w_ref, gn_ref, xo_ref, ho_ref, y_ref):
    def group_norm(tiles, col0):
        width = len(tiles) * HEAD_DIM
        ssq = sum(jnp.sum(tl * tl, axis=-1, keepdims=True) for tl in tiles)
        r = lax.rsqrt(ssq * (1.0 / width) + EPS)
        for n, tl in enumerate(tiles):
            c = col0 + n * HEAD_DIM
            y_ref[:, c:c + HEAD_DIM] = ((tl * r) * gout_ref[:, c:c + HEAD_DIM]).astype(BF16)

    group_norm([oa_ref[n].astype(F32) for n in range(A_HEADS)], 0)
    group_norm([ob_ref[n].astype(F32) for n in range(B_SLOTS)], A_OUT)
    group_norm([oc_ref[n].astype(F32) for n in range(C_HEADS)], A_OUT + B_OUT)

    xn = x_ref[...] + jnp.dot(y_ref[...], w_ref[...], preferred_element_type=F32)
    xo_ref[...] = xn
    ho_ref[...] = _rms(xn, gn_ref[...], D_MODEL).astype(BF16)


def _mix_out(x, oa, ob, oc, gain_out, w_out, gain_next):
    t = x.shape[0]
    heads = lambda n: pl.BlockSpec((n, TM_OUT, HEAD_DIM), lambda i: (0, i, 0))
    row = pl.BlockSpec((1, D_MODEL), lambda i: (0, 0))
    tok = pl.BlockSpec((TM_OUT, D_MODEL), lambda i: (i, 0))
    return pl.pallas_call(
        _mix_out_kernel,
        grid=(t // TM_OUT,),
        in_specs=[tok, heads(A_HEADS), heads(B_SLOTS), heads(C_HEADS), row,
                  pl.BlockSpec((D_MODEL, D_MODEL), lambda i: (0, 0)), row],
        out_specs=[tok, tok],
        out_shape=[jax.ShapeDtypeStruct((t, D_MODEL), F32),
                   jax.ShapeDtypeStruct((t, D_MODEL), BF16)],
        scratch_shapes=[pltpu.VMEM((TM_OUT, D_MODEL), BF16)],
        compiler_params=_params("parallel"),
        name="mix_out",
    )(x, oa, ob, oc, gain_out, w_out, gain_next)


def _rope_tables(pos_a, pos_b):
    inv = ROPE_THETA ** (-jnp.arange(ROPE_HALF, dtype=F32) / ROPE_HALF)

    def one(pos):
        ang = pos.astype(F32)[:, None] * inv[None, :]
        return jnp.cos(ang), jnp.sin(ang)

    ca, sa = one(pos_a)
    cb, sb = one(pos_b)
    return jnp.concatenate([ca, cb, ca, cb], axis=-1), jnp.concatenate([-sa, -sb, sa, sb], axis=-1)


def _band_steps_table():
    i = jnp.arange(TQ_B)[:, None]
    jj = jnp.arange(2 * TQ_B)[None, :]
    tabs = []
    for off in (0, TQ_B // 2, TQ_B):
        rel = jnp.abs(off + i - jj)
        tabs.append(jnp.where(rel <= B_SPAN, rel.astype(F32), -NEG_BIG))
    return jnp.stack(tabs)


def _head_gains(gq_a, gk_a, gq_b, gk_b):
    ones = jnp.ones((HEAD_DIM,), F32)
    rows = ([gq_a] * A_HEADS + [gk_a] * A_KV_HEADS + [ones] * A_KV_HEADS
            + [gq_b] * B_HEADS + [gk_b] * B_HEADS + [ones] * B_HEADS)
    return jnp.stack(rows).astype(F32).reshape(AB_HEADS, 1, HEAD_DIM)


def kernel(x, ffn1_norm, ffn1_w_gu, ffn1_w_down, mix_norm, w_in, a_q_norm, a_k_norm, b_q_norm, b_k_norm, c_q_a_norm, c_q_up, c_kv_a_norm, c_kv_up, c_q_norm, c_k_norm, out_norm, w_out, ffn2_norm, ffn2_w_gu, ffn2_w_down):
    nb, s, d = x.shape
    assert (s, d) == (SEQ, D_MODEL)
    depth = w_in.shape[0]
    t = nb * s
    x = x.reshape(t, d)
    row = lambda v: v.reshape(1, -1).astype(F32)

    pos = jnp.arange(s, dtype=jnp.int32)
    cos_a, sin_a = _rope_tables(pos // GRID_W, pos % GRID_W)
    cos_c, sin_c = _rope_tables(pos, pos)
    lane_order = jnp.array(ROPE_LANE_ORDER)

    def rope_tile(v):
        zeros = jnp.zeros(v.shape[:-1] + (V7X_LANES // 2 - ROPE_HALF,), v.dtype)
        return jnp.concatenate([v[..., :ROPE_HALF], zeros, v[..., ROPE_HALF:], zeros], axis=-1)
    tab = _band_steps_table()
    slopes = 2.0 ** (-8.0 * jnp.arange(1, B_HEADS + 1, dtype=F32) / B_HEADS)
    dil = jnp.array([c[1] for c in B_CONFIGS], F32)
    coef = (slopes.reshape(len(B_CONFIGS), B_SLOTS) * dil[:, None]).T
    coef = jnp.zeros((B_SLOTS, 8, V7X_LANES), F32).at[:, :len(B_CONFIGS), :].set(coef[:, :, None])

    h = _norm(x, row(ffn1_norm[0]))
    for l in range(depth):
        x, h = _ffn(x, h, ffn1_w_gu[l].astype(BF16), ffn1_w_down[l].astype(BF16), row(mix_norm[l]))

        w = w_in[l]
        rot = (A_HEADS + A_KV_HEADS) * HEAD_DIM
        w_rot = w[:, :rot].reshape(D_MODEL, -1, HEAD_DIM)[:, :, lane_order].reshape(D_MODEL, rot)
        w_ab = jnp.concatenate([w_rot, w[:, rot:AB_IN]], axis=-1).astype(BF16)
        qkv = _proj_heads(h, w_ab, _head_gains(a_q_norm[l][lane_order], a_k_norm[l][lane_order],
                                               b_q_norm[l], b_k_norm[l]), cos_a, sin_a)
        w_c = jnp.concatenate([w[:, AB_IN:W_IN - C_ROPE], rope_tile(w[:, W_IN - C_ROPE:])], axis=-1).astype(BF16)
        q_up = c_q_up[l].reshape(C_Q_RANK, C_HEADS, C_QK)
        q_up = jnp.concatenate([q_up[:, :, :C_NOPE].reshape(C_Q_RANK, C_HEADS * C_NOPE),
                                rope_tile(q_up[:, :, C_NOPE:]).reshape(C_Q_RANK, C_HEADS * V7X_LANES)],
                               axis=-1).astype(BF16)
        split = lambda gvec: jnp.stack([gvec[:C_NOPE], rope_tile(gvec[C_NOPE:])]).astype(F32)
        qc, kc, vc = _proj_c(h, w_c, row(c_q_a_norm[l]), row(c_kv_a_norm[l]), q_up, c_kv_up[l].astype(BF16),
                             split(c_q_norm[l]), split(c_k_norm[l]), cos_c, sin_c)

        oa = _attention(qkv, qkv, qkv, kv_heads=A_KV_HEADS, group=A_HEADS // A_KV_HEADS,
                        k_head0=A_HEADS, v_head0=A_HEADS + A_KV_HEADS, tq=128)
        oc = _attention(qc, kc, vc, kv_heads=C_HEADS, group=1, k_head0=0, v_head0=0, tq=512)
        ob = _attn_b(qkv, coef, tab)

        x, h = _mix_out(x, oa, ob, oc, row(out_norm[l]), w_out[l].astype(BF16), row(ffn2_norm[l]))
        w_gu, w_down = ffn2_w_gu[l].astype(BF16), ffn2_w_down[l].astype(BF16)
        if l + 1 < depth:
            x, h = _ffn(x, h, w_gu, w_down, row(ffn1_norm[l + 1]))
        else:
            x, = _ffn(x, h, w_gu, w_down)
    return x.reshape(nb, s, d)
```

```python
import functools

import jax
import jax.numpy as jnp
from jax import lax
from jax.experimental import pallas as pl
from jax.experimental.pallas import tpu as pltpu

F32 = jnp.float32
BF16 = jnp.bfloat16

D_MODEL = 2048
SEQ = 4096
GRID_W = 64
HEAD_DIM = 128
ROPE_THETA = 10000.0
EPS = 1e-6
NEG_BIG = -1e30
LOG2_E = 1.4426950408889634

A_HEADS = 8
A_KV_HEADS = 2
B_CONFIGS = ((128, 1), (512, 4), (2048, 16))
B_SLOTS = 4
B_HEADS = B_SLOTS * len(B_CONFIGS)
C_HEADS = 4
C_Q_RANK = 512
C_KV_RANK = 256
C_NOPE = 128
C_ROPE = 64
C_QK = C_NOPE + C_ROPE
D_FF = 5632

A_Q = A_HEADS * HEAD_DIM
A_KV = A_KV_HEADS * HEAD_DIM
A_IN = A_Q + 2 * A_KV
B_QKV = B_HEADS * HEAD_DIM
B_IN = 3 * B_QKV
AB_IN = A_IN + B_IN
W_IN = AB_IN + C_Q_RANK + C_KV_RANK + C_ROPE
AB_HEADS = AB_IN // HEAD_DIM
A_OUT = A_HEADS * HEAD_DIM
B_OUT = B_SLOTS * HEAD_DIM
C_OUT = C_HEADS * HEAD_DIM

V7X_LANES = 128
V7X_VMEM_LIMIT = 56 * 1024 * 1024

TM_FFN = 512
TF_FFN = 512
TM_PROJ = 512
TN_PROJ = 512
TM_OUT = 256
TQ_B = 128
B_SPAN = 64
B_WIDTH = TQ_B + 2 * B_SPAN
B_MERGE_ROWS = 256
B_TILES_PER_BLOCK = 8
TK_ATT = 512


def _params(*sem):
    return pltpu.CompilerParams(dimension_semantics=sem, vmem_limit_bytes=V7X_VMEM_LIMIT)


def _rms(x, gain, width):
    ms = jnp.sum(x * x, axis=-1, keepdims=True) * (1.0 / width)
    return (x * lax.rsqrt(ms + EPS)) * gain


ROPE_HALF = 32
ROPE_LANE_ORDER = tuple(list(range(0, 32)) + list(range(64, 96)) + list(range(32, 64)) + list(range(96, 128)))


def _rope(y, cos, sin_signed):
    return y * cos + pltpu.roll(y, V7X_LANES // 2, 1) * sin_signed


def _norm_kernel(x_ref, g_ref, h_ref):
    h_ref[...] = _rms(x_ref[...], g_ref[...], D_MODEL).astype(BF16)


def _norm(x, gain):
    t = x.shape[0]
    return pl.pallas_call(
        _norm_kernel,
        grid=(t // TM_FFN,),
        in_specs=[pl.BlockSpec((TM_FFN, D_MODEL), lambda i: (i, 0)),
                  pl.BlockSpec((1, D_MODEL), lambda i: (0, 0))],
        out_specs=pl.BlockSpec((TM_FFN, D_MODEL), lambda i: (i, 0)),
        out_shape=jax.ShapeDtypeStruct((t, D_MODEL), BF16),
        compiler_params=_params("parallel"),
        name="norm",
    )(x, gain)


def _ffn_kernel(x_ref, h_ref, wg_ref, wu_ref, wd_ref, *rest, emit_next, nf):
    if emit_next:
        gn_ref, xo_ref, ho_ref, acc_ref, a0_ref, a1_ref = rest
    else:
        xo_ref, acc_ref, a0_ref, a1_ref = rest
    j = pl.program_id(1)

    def gate_up():
        h = h_ref[...]
        g = jnp.dot(h, wg_ref[...], preferred_element_type=F32)
        u = jnp.dot(h, wu_ref[...], preferred_element_type=F32)
        return g, u

    def swiglu(g, u, a_out):
        a_out[...] = (g * (1.0 / (1.0 + jnp.exp(-g))) * u).astype(BF16)

    def down(a_in):
        return jnp.dot(a_in[...], wd_ref[...], preferred_element_type=F32)

    @pl.when(j == 0)
    def _():
        swiglu(*gate_up(), a0_ref)

    def middle(a_in, a_out, first):
        g, u = gate_up()
        d = down(a_in)
        if first:
            acc_ref[...] = d
        else:
            acc_ref[...] += d
        swiglu(g, u, a_out)

    pl.when(j == 1)(functools.partial(middle, a0_ref, a1_ref, True))
    pl.when((j >= 3) & (j < nf) & (j % 2 == 1))(functools.partial(middle, a0_ref, a1_ref, False))
    pl.when((j >= 2) & (j < nf) & (j % 2 == 0))(functools.partial(middle, a1_ref, a0_ref, False))

    @pl.when(j == nf)
    def _():
        xn = x_ref[...] + 0.5 * (acc_ref[...] + down(a1_ref if nf % 2 == 0 else a0_ref))
        xo_ref[...] = xn
        if emit_next:
            ho_ref[...] = _rms(xn, gn_ref[...], D_MODEL).astype(BF16)


def _ffn(x, h, w_gu, w_down, gain_next=None):
    t = x.shape[0]
    nf = D_FF // TF_FFN
    assert nf >= 3
    emit_next = gain_next is not None
    tok = pl.BlockSpec((TM_FFN, D_MODEL), lambda i, j: (i, 0))
    cur = lambda j: jnp.minimum(j, nf - 1)
    in_specs = [tok, tok,
                pl.BlockSpec((D_MODEL, TF_FFN), lambda i, j: (0, cur(j))),
                pl.BlockSpec((D_MODEL, TF_FFN), lambda i, j: (0, cur(j) + nf)),
                pl.BlockSpec((TF_FFN, D_MODEL), lambda i, j: (jnp.maximum(j - 1, 0), 0))]
    args = [x, h, w_gu, w_gu, w_down]
    out_specs = [tok]
    out_shape = [jax.ShapeDtypeStruct((t, D_MODEL), F32)]
    if emit_next:
        in_specs.append(pl.BlockSpec((1, D_MODEL), lambda i, j: (0, 0)))
        args.append(gain_next)
        out_specs.append(tok)
        out_shape.append(jax.ShapeDtypeStruct((t, D_MODEL), BF16))
    return pl.pallas_call(
        functools.partial(_ffn_kernel, emit_next=emit_next, nf=nf),
        grid=(t // TM_FFN, nf + 1),
        in_specs=in_specs,
        out_specs=out_specs,
        out_shape=out_shape,
        scratch_shapes=[pltpu.VMEM((TM_FFN, D_MODEL), F32)] + [pltpu.VMEM((TM_FFN, TF_FFN), BF16)] * 2,
        compiler_params=_params("parallel", "arbitrary"),
        name="ffn",
    )(*args)


HEAD_PLAIN, HEAD_NORM, HEAD_NORM_ROPE = 0, 1, 2
A_Q_SCALE = HEAD_DIM ** -0.5 * LOG2_E
C_Q_SCALE = C_QK ** -0.5 * LOG2_E
AB_HEAD_KINDS = (((HEAD_NORM_ROPE, A_Q_SCALE),) * A_HEADS + ((HEAD_NORM_ROPE, 1.0),) * A_KV_HEADS
                 + ((HEAD_PLAIN, 1.0),) * A_KV_HEADS
                 + ((HEAD_NORM, 1.0),) * (2 * B_HEADS) + ((HEAD_PLAIN, 1.0),) * B_HEADS)


def _proj_heads_kernel(h_ref, w_ref, g_ref, cos_ref, sin_ref, o_ref, z_ref, *, step_kinds):
    j = pl.program_id(1)
    n_blocks = len(step_kinds)

    def step(kinds, with_dot):
        if kinds is not None:
            for hh, (kind, out_scale) in enumerate(kinds):
                y = z_ref[:, hh * HEAD_DIM:(hh + 1) * HEAD_DIM]
                if kind != HEAD_PLAIN:
                    y = _rms(y, g_ref[hh], HEAD_DIM)
                if kind == HEAD_NORM_ROPE:
                    y = _rope(y, cos_ref[...], sin_ref[...])
                if out_scale != 1.0:
                    y = y * out_scale
                o_ref[hh] = y.astype(BF16)
        if with_dot:
            z_ref[...] = jnp.dot(h_ref[...], w_ref[...], preferred_element_type=F32)

    pl.when(j == 0)(functools.partial(step, None, True))
    lo = 0
    while lo < n_blocks:
        hi = lo
        while hi < n_blocks and step_kinds[hi] == step_kinds[lo]:
            hi += 1
        last = min(hi, n_blocks - 1)
        if last > lo:
            pl.when((j > lo) & (j <= last))(functools.partial(step, step_kinds[lo], True))
        lo = hi
    pl.when(j == n_blocks)(functools.partial(step, step_kinds[-1], False))


def _proj_heads(h, w_ab, gains, cos, sin):
    t = h.shape[0]
    heads = TN_PROJ // HEAD_DIM
    n_blocks = AB_IN // TN_PROJ
    seq_tiles = SEQ // TM_PROJ
    step_kinds = tuple(AB_HEAD_KINDS[n:n + heads] for n in range(0, AB_HEADS, heads))
    prev = lambda j: jnp.maximum(j - 1, 0)
    return pl.pallas_call(
        functools.partial(_proj_heads_kernel, step_kinds=step_kinds),
        grid=(t // TM_PROJ, n_blocks + 1),
        in_specs=[pl.BlockSpec((TM_PROJ, D_MODEL), lambda i, j: (i, 0)),
                  pl.BlockSpec((D_MODEL, TN_PROJ), lambda i, j: (0, jnp.minimum(j, n_blocks - 1))),
                  pl.BlockSpec((heads, 1, HEAD_DIM), lambda i, j: (prev(j), 0, 0)),
                  pl.BlockSpec((TM_PROJ, HEAD_DIM), lambda i, j: (i % seq_tiles, 0)),
                  pl.BlockSpec((TM_PROJ, HEAD_DIM), lambda i, j: (i % seq_tiles, 0))],
        out_specs=pl.BlockSpec((heads, TM_PROJ, HEAD_DIM), lambda i, j: (prev(j), i, 0)),
        out_shape=jax.ShapeDtypeStruct((AB_HEADS, t, HEAD_DIM), BF16),
        scratch_shapes=[pltpu.VMEM((TM_PROJ, TN_PROJ), F32)],
        compiler_params=_params("parallel", "arbitrary"),
        name="proj_heads",
    )(h, w_ab, gains, cos, sin)


def _proj_c_kernel(h_ref, wc_ref, gqa_ref, gkva_ref, qup_ref, kvup_ref, gq_ref, gk_ref,
                   cos_ref, sin_ref, q_ref, k_ref, v_ref):
    nope_w = C_HEADS * C_NOPE
    half = h_ref.shape[0] // 2
    for part in range(2):
        rows = slice(part * half, (part + 1) * half)
        h = h_ref[rows, :]
        cos = cos_ref[rows, :]
        sin = sin_ref[rows, :]

        zq = jnp.dot(h, wc_ref[:, :C_Q_RANK], preferred_element_type=F32)
        zkv = jnp.dot(h, wc_ref[:, C_Q_RANK:], preferred_element_type=F32)
        q_lat = _rms(zq, gqa_ref[...], C_Q_RANK).astype(BF16)
        cq = jnp.dot(q_lat, qup_ref[...], preferred_element_type=F32)
        kv_lat = _rms(zkv[:, :C_KV_RANK], gkva_ref[...], C_KV_RANK).astype(BF16)
        k_rope = zkv[:, C_KV_RANK:]
        ckv = jnp.dot(kv_lat, kvup_ref[...], preferred_element_type=F32)

        for hh in range(C_HEADS):
            qn = cq[:, hh * C_NOPE:(hh + 1) * C_NOPE]
            qr = cq[:, nope_w + hh * V7X_LANES:nope_w + (hh + 1) * V7X_LANES]
            ms = (jnp.sum(qn * qn, axis=-1, keepdims=True)
                  + jnp.sum(qr * qr, axis=-1, keepdims=True)) * (1.0 / C_QK)
            r = lax.rsqrt(ms + EPS)
            q_ref[hh, rows, 0:C_NOPE] = (((qn * r) * gq_ref[0:1, :]) * C_Q_SCALE).astype(BF16)
            q_ref[hh, rows, C_NOPE:] = (_rope((qr * r) * gq_ref[1:2, :], cos, sin) * C_Q_SCALE).astype(BF16)

        k_rope_sq = jnp.sum(k_rope * k_rope, axis=-1, keepdims=True)
        for hh in range(C_HEADS):
            kn = ckv[:, hh * 2 * C_NOPE:hh * 2 * C_NOPE + C_NOPE]
            ms = (jnp.sum(kn * kn, axis=-1, keepdims=True) + k_rope_sq) * (1.0 / C_QK)
            r = lax.rsqrt(ms + EPS)
            k_ref[hh, rows, 0:C_NOPE] = ((kn * r) * gk_ref[0:1, :]).astype(BF16)
            k_ref[hh, rows, C_NOPE:] = _rope((k_rope * r) * gk_ref[1:2, :], cos, sin).astype(BF16)
            v_ref[hh, rows, :] = ckv[:, hh * 2 * C_NOPE + C_NOPE:(hh + 1) * 2 * C_NOPE].astype(BF16)


def _proj_c(h, w_c, gqa, gkva, q_up, kv_up, gq, gk, cos, sin):
    t = h.shape[0]
    seq_tiles = SEQ // TM_PROJ
    full = lambda a: pl.BlockSpec(a.shape, lambda i: (0,) * a.ndim)
    qk_w = 2 * V7X_LANES
    return pl.pallas_call(
        _proj_c_kernel,
        grid=(t // TM_PROJ,),
        in_specs=[pl.BlockSpec((TM_PROJ, D_MODEL), lambda i: (i, 0)),
                  full(w_c), full(gqa), full(gkva), full(q_up), full(kv_up), full(gq), full(gk),
                  pl.BlockSpec((TM_PROJ, V7X_LANES), lambda i: (i % seq_tiles, 0)),
                  pl.BlockSpec((TM_PROJ, V7X_LANES), lambda i: (i % seq_tiles, 0))],
        out_specs=[pl.BlockSpec((C_HEADS, TM_PROJ, qk_w), lambda i: (0, i, 0)),
                   pl.BlockSpec((C_HEADS, TM_PROJ, qk_w), lambda i: (0, i, 0)),
                   pl.BlockSpec((C_HEADS, TM_PROJ, C_NOPE), lambda i: (0, i, 0))],
        out_shape=[jax.ShapeDtypeStruct((C_HEADS, t, qk_w), BF16),
                   jax.ShapeDtypeStruct((C_HEADS, t, qk_w), BF16),
                   jax.ShapeDtypeStruct((C_HEADS, t, C_NOPE), BF16)],
        compiler_params=_params("parallel"),
        name="proj_c",
    )(h, w_c, gqa, gkva, q_up, kv_up, gq, gk, cos, sin)


def _attn_kernel(q_ref, k_ref, v_ref, o_ref, s0_ref, s1_ref, m0_ref, m1_ref, *, tq):
    g, _, dq = q_ref.shape
    rows = g * tq
    dv = v_ref.shape[-1]
    tiles = SEQ // tq
    contract_last = (((1,), (1,)), ((), ()))
    contract_first = (((0,), (0,)), ((), ()))

    def q_rows(tile):
        start = tile * tq
        return pl.ds(start if isinstance(tile, int) else pl.multiple_of(start, tq), tq)

    def passes(score_tile, score_bufs, out_tile, out_bufs):
        if score_tile is not None:
            sw_ref, mw_ref = score_bufs
            q = q_ref[:, q_rows(score_tile), :].reshape(rows, dq)
            m_new = None
        if out_tile is not None:
            sr_ref, mr_ref = out_bufs
            m = mr_ref[...]
            lsum = jnp.zeros((1, rows), F32)
            acc = jnp.zeros((dv, rows), F32)
        for c in range(SEQ // TK_ATT):
            keys = slice(c * TK_ATT, (c + 1) * TK_ATT)
            if score_tile is not None:
                s = lax.dot_general(k_ref[0, keys, :], q, contract_last, preferred_element_type=F32)
                sw_ref[keys, :] = s
                part = jnp.max(s, axis=0, keepdims=True)
                m_new = part if m_new is None else jnp.maximum(m_new, part)
            if out_tile is not None:
                p = jnp.exp2(sr_ref[keys, :] - m)
                lsum = lsum + jnp.sum(p, axis=0, keepdims=True)
                acc = acc + lax.dot_general(v_ref[0, keys, :], p.astype(BF16), contract_first,
                                            preferred_element_type=F32)
        if score_tile is not None:
            mw_ref[...] = m_new
        if out_tile is not None:
            o = (acc / lsum).T
            o_ref[:, q_rows(out_tile), :] = o.reshape(g, tq, dv).astype(o_ref.dtype)

    buf0, buf1 = (s0_ref, m0_ref), (s1_ref, m1_ref)
    passes(0, buf0, None, None)

    def pair(n, carry):
        tile = 2 * n
        passes(tile + 1, buf1, tile, buf0)

        @pl.when(n < tiles // 2 - 1)
        def _():
            passes(tile + 2, buf0, tile + 1, buf1)

        @pl.when(n == tiles // 2 - 1)
        def _():
            passes(None, None, tile + 1, buf1)

        return carry

    lax.fori_loop(0, tiles // 2, pair, 0)


def _attention(q, k, v, *, kv_heads, group, k_head0, v_head0, tq):
    _, t, dq = q.shape
    dv = v.shape[-1]
    nb = t // SEQ
    rows = group * tq
    return pl.pallas_call(
        functools.partial(_attn_kernel, tq=tq),
        grid=(nb, kv_heads),
        in_specs=[pl.BlockSpec((group, SEQ, dq), lambda b, h: (h, b, 0)),
                  pl.BlockSpec((1, SEQ, dq), lambda b, h: (k_head0 + h, b, 0)),
                  pl.BlockSpec((1, SEQ, dv), lambda b, h: (v_head0 + h, b, 0))],
        out_specs=pl.BlockSpec((group, SEQ, dv), lambda b, h: (h, b, 0)),
        out_shape=jax.ShapeDtypeStruct((kv_heads * group, t, dv), BF16),
        scratch_shapes=[pltpu.VMEM((SEQ, rows), F32)] * 2 + [pltpu.VMEM((1, rows), F32)] * 2,
        compiler_params=_params("parallel", "parallel"),
        name="attention",
    )(q, k, v)


def _attn_b_kernel(q0, k0, v0, q1, k1, v1, q2, k2, v2, coef_ref, tab_ref, o_ref,
                   stage_ref, qd_ref, kd_ref, vd_ref, og_ref, eg_ref):
    scale = HEAD_DIM ** -0.5
    inputs = ((q0, k0, v0), (q1, k1, v1), (q2, k2, v2))
    chunk = 64

    def gather_classes(src_ref, dst_ref, dil):
        length = SEQ // dil
        stage_ref[...] = src_ref[0].astype(F32)

        def body(c, carry):
            for r in range(dil):
                src = pl.ds(pl.multiple_of(c * chunk * dil, chunk * dil) + r, chunk, stride=dil)
                dst = pl.ds(pl.multiple_of(r * length + c * chunk, chunk), chunk)
                dst_ref[dst, :] = stage_ref[src, :].astype(BF16)
            return carry

        lax.fori_loop(0, length // chunk, body, 0)

    ones = jnp.ones((B_WIDTH, HEAD_DIM), BF16)

    def tile(g, q, k, v, steps, coef, out_rows):
        s = lax.dot_general(q, k, (((1,), (1,)), ((), ())), preferred_element_type=F32)
        s = s * scale - coef * steps
        m = jnp.max(s, axis=-1, keepdims=True)
        p = jnp.exp(s - m).astype(BF16)
        ol = jnp.dot(p, jnp.concatenate([v, ones], axis=-1), preferred_element_type=F32)
        l = ol[:, HEAD_DIM:]
        og_ref[g, out_rows, :] = ol[:, :HEAD_DIM] / l
        eg_ref[g, out_rows, :] = m + jnp.log(l)

    for g, (_, dil) in enumerate(B_CONFIGS):
        length = SEQ // dil
        nt = length // TQ_B
        coef = coef_ref[0, g:g + 1, 0:1]
        if dil == 1:
            qs, ks, vs = (lambda rows, ref=ref: ref[0, rows, :] for ref in inputs[g])
        else:
            for src, dst in zip(inputs[g], (qd_ref, kd_ref, vd_ref)):
                gather_classes(src, dst, dil)
            qs, ks, vs = (lambda rows, ref=ref: ref[rows, :] for ref in (qd_ref, kd_ref, vd_ref))

        def residue(r, g=g, dil=dil, length=length, nt=nt, coef=coef, qs=qs, ks=ks, vs=vs):
            base = r * length
            for ti in range(nt):
                l0 = ti * TQ_B
                k0_ = min(max(l0 - B_SPAN, 0), length - B_WIDTH)
                steps = tab_ref[(l0 - k0_) // B_SPAN]
                if dil == 1:
                    q_rows, k_rows, out_rows = pl.ds(l0, TQ_B), pl.ds(k0_, B_WIDTH), pl.ds(l0, TQ_B)
                else:
                    q_rows = pl.ds(pl.multiple_of(base + l0, TQ_B), TQ_B)
                    k_rows = pl.ds(pl.multiple_of(base + k0_, B_SPAN), B_WIDTH)
                    out_rows = pl.ds(l0 * dil + r, TQ_B, stride=dil)
                tile(g, qs(q_rows), ks(k_rows), vs(k_rows), steps, coef, out_rows)

        if dil == 1:
            residue(0)
        else:
            per_step = max(1, B_TILES_PER_BLOCK // nt)

            def step(it, carry, residue=residue, per_step=per_step):
                for rr in range(per_step):
                    residue(it * per_step + rr)
                return carry

            lax.fori_loop(0, dil // per_step, step, 0)

    def merge(c, carry):
        rows = pl.ds(pl.multiple_of(c * B_MERGE_ROWS, B_MERGE_ROWS), B_MERGE_ROWS)
        lse = [eg_ref[g, rows, :] for g in range(len(B_CONFIGS))]
        mx = jnp.maximum(jnp.maximum(lse[0], lse[1]), lse[2])
        w = [jnp.exp(e - mx) for e in lse]
        den = w[0] + w[1] + w[2]
        o_ref[0, rows, :] = sum((w[g] / den) * og_ref[g, rows, :] for g in range(len(B_CONFIGS))).astype(o_ref.dtype)
        return carry

    lax.fori_loop(0, SEQ // B_MERGE_ROWS, merge, 0)


def _attn_b(qkv_heads, coef, tab):
    t = qkv_heads.shape[1]
    nb = t // SEQ
    base = A_IN // HEAD_DIM
    in_specs = []
    for g in range(len(B_CONFIGS)):
        for kind in range(3):
            head0 = base + kind * B_HEADS + g * B_SLOTS
            in_specs.append(pl.BlockSpec((1, SEQ, HEAD_DIM), lambda b, j, head0=head0: (head0 + j, b, 0)))
    in_specs.append(pl.BlockSpec((1, 8, V7X_LANES), lambda b, j: (j, 0, 0)))
    in_specs.append(pl.BlockSpec(tab.shape, lambda b, j: (0, 0, 0)))
    return pl.pallas_call(
        _attn_b_kernel,
        grid=(nb, B_SLOTS),
        in_specs=in_specs,
        out_specs=pl.BlockSpec((1, SEQ, HEAD_DIM), lambda b, j: (j, b, 0)),
        out_shape=jax.ShapeDtypeStruct((B_SLOTS, t, HEAD_DIM), BF16),
        scratch_shapes=[pltpu.VMEM((SEQ, HEAD_DIM), F32)]
                       + [pltpu.VMEM((SEQ, HEAD_DIM), BF16)] * 3
                       + [pltpu.VMEM((len(B_CONFIGS), SEQ, HEAD_DIM), F32)] * 2,
        compiler_params=_params("parallel", "parallel"),
        name="attn_b",
    )(*([qkv_heads] * 9), coef, tab)


def _mix_out_kernel(x_ref, oa_ref, ob_ref, oc_ref, gout_ref, w_ref, gn_ref, xo_ref, ho_ref, y_ref):
    def group_norm(tiles, col0):
        width = len(tiles) * HEAD_DIM
        ssq = sum(jnp.sum(tl * tl, axis=-1, keepdims=True) for tl in tiles)
        r = lax.rsqrt(ssq * (1.0 / width) + EPS)
        for n, tl in enumerate(tiles):
            c = col0 + n * HEAD_DIM
            y_ref[:, c:c + HEAD_DIM] = ((tl * r) * gout_ref[:, c:c + HEAD_DIM]).astype(BF16)

    group_norm([oa_ref[n].astype(F32) for n in range(A_HEADS)], 0)
    group_norm([ob_ref[n].astype(F32) for n in range(B_SLOTS)], A_OUT)
    group_norm([oc_ref[n].astype(F32) for n in range(C_HEADS)], A_OUT + B_OUT)

    xn = x_ref[...] + jnp.dot(y_ref[...], w_ref[...], preferred_element_type=F32)
    xo_ref[...] = xn
    ho_ref[...] = _rms(xn, gn_ref[...], D_MODEL).astype(BF16)


def _mix_out(x, oa, ob, oc, gain_out, w_out, gain_next):
    t = x.shape[0]
    heads = lambda n: pl.BlockSpec((n, TM_OUT, HEAD_DIM), lambda i: (0, i, 0))
    row = pl.BlockSpec((1, D_MODEL), lambda i: (0, 0))
    tok = pl.BlockSpec((TM_OUT, D_MODEL), lambda i: (i, 0))
    return pl.pallas_call(
        _mix_out_kernel,
        grid=(t // TM_OUT,),
        in_specs=[tok, heads(A_HEADS), heads(B_SLOTS), heads(C_HEADS), row,
                  pl.BlockSpec((D_MODEL, D_MODEL), lambda i: (0, 0)), row],
        out_specs=[tok, tok],
        out_shape=[jax.ShapeDtypeStruct((t, D_MODEL), F32),
                   jax.ShapeDtypeStruct((t, D_MODEL), BF16)],
        scratch_shapes=[pltpu.VMEM((TM_OUT, D_MODEL), BF16)],
        compiler_params=_params("parallel"),
        name="mix_out",
    )(x, oa, ob, oc, gain_out, w_out, gain_next)


def _rope_tables(pos_a, pos_b):
    inv = ROPE_THETA ** (-jnp.arange(ROPE_HALF, dtype=F32) / ROPE_HALF)

    def one(pos):
        ang = pos.astype(F32)[:, None] * inv[None, :]
        return jnp.cos(ang), jnp.sin(ang)

    ca, sa = one(pos_a)
    cb, sb = one(pos_b)
    return jnp.concatenate([ca, cb, ca, cb], axis=-1), jnp.concatenate([-sa, -sb, sa, sb], axis=-1)


def _band_steps_table():
    i = jnp.arange(TQ_B)[:, None]
    jj = jnp.arange(B_WIDTH)[None, :]
    tabs = []
    for off in (0, B_SPAN, 2 * B_SPAN):
        rel = jnp.abs(off + i - jj)
        tabs.append(jnp.where(rel <= B_SPAN, rel.astype(F32), -NEG_BIG))
    return jnp.stack(tabs)


def _head_gains(gq_a, gk_a, gq_b, gk_b):
    ones = jnp.ones((HEAD_DIM,), F32)
    rows = ([gq_a] * A_HEADS + [gk_a] * A_KV_HEADS + [ones] * A_KV_HEADS
            + [gq_b] * B_HEADS + [gk_b] * B_HEADS + [ones] * B_HEADS)
    return jnp.stack(rows).astype(F32).reshape(AB_HEADS, 1, HEAD_DIM)


def kernel(x, ffn1_norm, ffn1_w_gu, ffn1_w_down, mix_norm, w_in, a_q_norm, a_k_norm, b_q_norm, b_k_norm, c_q_a_norm, c_q_up, c_kv_a_norm, c_kv_up, c_q_norm, c_k_norm, out_norm, w_out, ffn2_norm, ffn2_w_gu, ffn2_w_down):
    nb, s, d = x.shape
    assert (s, d) == (SEQ, D_MODEL)
    depth = w_in.shape[0]
    t = nb * s
    x = x.reshape(t, d)
    row = lambda v: v.reshape(1, -1).astype(F32)

    pos = jnp.arange(s, dtype=jnp.int32)
    cos_a, sin_a = _rope_tables(pos // GRID_W, pos % GRID_W)
    cos_c, sin_c = _rope_tables(pos, pos)
    lane_order = jnp.array(ROPE_LANE_ORDER)

    def rope_tile(v):
        zeros = jnp.zeros(v.shape[:-1] + (V7X_LANES // 2 - ROPE_HALF,), v.dtype)
        return jnp.concatenate([v[..., :ROPE_HALF], zeros, v[..., ROPE_HALF:], zeros], axis=-1)
    tab = _band_steps_table()
    slopes = 2.0 ** (-8.0 * jnp.arange(1, B_HEADS + 1, dtype=F32) / B_HEADS)
    dil = jnp.array([c[1] for c in B_CONFIGS], F32)
    coef = (slopes.reshape(len(B_CONFIGS), B_SLOTS) * dil[:, None]).T
    coef = jnp.zeros((B_SLOTS, 8, V7X_LANES), F32).at[:, :len(B_CONFIGS), :].set(coef[:, :, None])

    h = _norm(x, row(ffn1_norm[0]))
    for l in range(depth):
        x, h = _ffn(x, h, ffn1_w_gu[l].astype(BF16), ffn1_w_down[l].astype(BF16), row(mix_norm[l]))

        w = w_in[l]
        rot = (A_HEADS + A_KV_HEADS) * HEAD_DIM
        w_rot = w[:, :rot].reshape(D_MODEL, -1, HEAD_DIM)[:, :, lane_order].reshape(D_MODEL, rot)
        w_ab = jnp.concatenate([w_rot, w[:, rot:AB_IN]], axis=-1).astype(BF16)
        qkv = _proj_heads(h, w_ab, _head_gains(a_q_norm[l][lane_order], a_k_norm[l][lane_order],
                                               b_q_norm[l], b_k_norm[l]), cos_a, sin_a)
        w_c = jnp.concatenate([w[:, AB_IN:W_IN - C_ROPE], rope_tile(w[:, W_IN - C_ROPE:])], axis=-1).astype(BF16)
        q_up = c_q_up[l].reshape(C_Q_RANK, C_HEADS, C_QK)
        q_up = jnp.concatenate([q_up[:, :, :C_NOPE].reshape(C_Q_RANK, C_HEADS * C_NOPE),
                                rope_tile(q_up[:, :, C_NOPE:]).reshape(C_Q_RANK, C_HEADS * V7X_LANES)],
                               axis=-1).astype(BF16)
        split = lambda gvec: jnp.stack([gvec[:C_NOPE], rope_tile(gvec[C_NOPE:])]).astype(F32)
        qc, kc, vc = _proj_c(h, w_c, row(c_q_a_norm[l]), row(c_kv_a_norm[l]), q_up, c_kv_up[l].astype(BF16),
                             split(c_q_norm[l]), split(c_k_norm[l]), cos_c, sin_c)

        oa = _attention(qkv, qkv, qkv, kv_heads=A_KV_HEADS, group=A_HEADS // A_KV_HEADS,
                        k_head0=A_HEADS, v_head0=A_HEADS + A_KV_HEADS, tq=128)
        oc = _attention(qc, kc, vc, kv_heads=C_HEADS, group=1, k_head0=0, v_head0=0, tq=512)
        ob = _attn_b(qkv, coef, tab)

        x, h = _mix_out(x, oa, ob, oc, row(out_norm[l]), w_out[l].astype(BF16), row(ffn2_norm[l]))
        w_gu, w_down = ffn2_w_gu[l].astype(BF16), ffn2_w_down[l].astype(BF16)
        if l + 1 < depth:
            x, h = _ffn(x, h, w_gu, w_down, row(ffn1_norm[l + 1]))
        else:
            x, = _ffn(x, h, w_gu, w_down)
    return x.reshape(nb, s, d)
```

```python
import functools

import jax
import jax.numpy as jnp
from jax import lax
from jax.experimental import pallas as pl
from jax.experimental.pallas import tpu as pltpu

F32 = jnp.float32
BF16 = jnp.bfloat16

D_MODEL = 2048
SEQ = 4096
GRID_W = 64
HEAD_DIM = 128
ROPE_THETA = 10000.0
EPS = 1e-6
NEG_BIG = -1e30
LOG2_E = 1.4426950408889634

A_HEADS = 8
A_KV_HEADS = 2
B_CONFIGS = ((128, 1), (512, 4), (2048, 16))
B_SLOTS = 4
B_HEADS = B_SLOTS * len(B_CONFIGS)
C_HEADS = 4
C_Q_RANK = 512
C_KV_RANK = 256
C_NOPE = 128
C_ROPE = 64
C_QK = C_NOPE + C_ROPE
D_FF = 5632

A_Q = A_HEADS * HEAD_DIM
A_KV = A_KV_HEADS * HEAD_DIM
A_IN = A_Q + 2 * A_KV
B_QKV = B_HEADS * HEAD_DIM
B_IN = 3 * B_QKV
AB_IN = A_IN + B_IN
W_IN = AB_IN + C_Q_RANK + C_KV_RANK + C_ROPE
AB_HEADS = AB_IN // HEAD_DIM
A_OUT = A_HEADS * HEAD_DIM
B_OUT = B_SLOTS * HEAD_DIM
C_OUT = C_HEADS * HEAD_DIM

V7X_LANES = 128
V7X_VMEM_LIMIT = 56 * 1024 * 1024

TM_NORM = 512
TM_UP = 1024
TF_FFN = 512
TM_DOWN = 256
TM_PROJ = 1024
TN_PROJ = 512
TM_OUT = 256
TQ_B = 128
B_SPAN = 64
B_WIDTH = TQ_B + 2 * B_SPAN
B_MERGE_ROWS = 256
B_TILES_PER_BLOCK = 8
TK_ATT = 512


def _params(*sem):
    return pltpu.CompilerParams(dimension_semantics=sem, vmem_limit_bytes=V7X_VMEM_LIMIT)


def _rms(x, gain, width):
    ms = jnp.sum(x * x, axis=-1, keepdims=True) * (1.0 / width)
    return (x * lax.rsqrt(ms + EPS)) * gain


ROPE_HALF = 32
ROPE_LANE_ORDER = tuple(list(range(0, 32)) + list(range(64, 96)) + list(range(32, 64)) + list(range(96, 128)))


def _rope(y, cos, sin_signed):
    return y * cos + pltpu.roll(y, V7X_LANES // 2, 1) * sin_signed


def _norm_kernel(x_ref, g_ref, h_ref):
    h_ref[...] = _rms(x_ref[...], g_ref[...], D_MODEL).astype(BF16)


def _norm(x, gain):
    t = x.shape[0]
    return pl.pallas_call(
        _norm_kernel,
        grid=(t // TM_NORM,),
        in_specs=[pl.BlockSpec((TM_NORM, D_MODEL), lambda i: (i, 0)),
                  pl.BlockSpec((1, D_MODEL), lambda i: (0, 0))],
        out_specs=pl.BlockSpec((TM_NORM, D_MODEL), lambda i: (i, 0)),
        out_shape=jax.ShapeDtypeStruct((t, D_MODEL), BF16),
        compiler_params=_params("parallel"),
        name="norm",
    )(x, gain)


def _ffn_up_kernel(h_ref, wg_ref, wu_ref, a_ref, g_ref, u_ref, *, nf):
    j = pl.program_id(1)

    def swiglu():
        g = g_ref[...]
        a_ref[...] = (g * (1.0 / (1.0 + jnp.exp(-g))) * u_ref[...]).astype(BF16)

    def matmuls():
        h = h_ref[...]
        g_ref[...] = jnp.dot(h, wg_ref[...], preferred_element_type=F32)
        u_ref[...] = jnp.dot(h, wu_ref[...], preferred_element_type=F32)

    pl.when(j == 0)(matmuls)

    @pl.when((j > 0) & (j < nf))
    def _():
        swiglu()
        matmuls()

    pl.when(j == nf)(swiglu)


def _ffn_down_kernel(a_ref, wd_ref, x_ref, *rest, emit_next, n_tiles):
    if emit_next:
        gn_ref, xo_ref, ho_ref, d_ref = rest
    else:
        xo_ref, d_ref = rest
    i = pl.program_id(0)

    def finish():
        xn = x_ref[...] + 0.5 * d_ref[...]
        xo_ref[...] = xn
        if emit_next:
            ho_ref[...] = _rms(xn, gn_ref[...], D_MODEL).astype(BF16)

    def matmul():
        d_ref[...] = jnp.dot(a_ref[...], wd_ref[...], preferred_element_type=F32)

    pl.when(i == 0)(matmul)

    @pl.when((i > 0) & (i < n_tiles))
    def _():
        finish()
        matmul()

    pl.when(i == n_tiles)(finish)


def _ffn(x, h, w_gu, w_down, gain_next=None):
    t = x.shape[0]
    nf = D_FF // TF_FFN
    cur = lambda j: jnp.minimum(j, nf - 1)
    act = pl.pallas_call(
        functools.partial(_ffn_up_kernel, nf=nf),
        grid=(t // TM_UP, nf + 1),
        in_specs=[pl.BlockSpec((TM_UP, D_MODEL), lambda i, j: (i, 0)),
                  pl.BlockSpec((D_MODEL, TF_FFN), lambda i, j: (0, cur(j))),
                  pl.BlockSpec((D_MODEL, TF_FFN), lambda i, j: (0, cur(j) + nf))],
        out_specs=pl.BlockSpec((TM_UP, TF_FFN), lambda i, j: (i, jnp.maximum(j - 1, 0))),
        out_shape=jax.ShapeDtypeStruct((t, D_FF), BF16),
        scratch_shapes=[pltpu.VMEM((TM_UP, TF_FFN), F32)] * 2,
        compiler_params=_params("parallel", "arbitrary"),
        name="ffn_up",
    )(h, w_gu, w_gu)

    emit_next = gain_next is not None
    n_tiles = t // TM_DOWN
    tok = pl.BlockSpec((TM_DOWN, D_MODEL), lambda i: (jnp.maximum(i - 1, 0), 0))
    in_specs = [pl.BlockSpec((TM_DOWN, D_FF), lambda i: (jnp.minimum(i, n_tiles - 1), 0)),
                pl.BlockSpec((D_FF, D_MODEL), lambda i: (0, 0), pipeline_mode=pl.Buffered(1)),
                tok]
    args = [act, w_down, x]
    out_specs = [tok]
    out_shape = [jax.ShapeDtypeStruct((t, D_MODEL), F32)]
    if emit_next:
        in_specs.append(pl.BlockSpec((1, D_MODEL), lambda i: (0, 0)))
        args.append(gain_next)
        out_specs.append(tok)
        out_shape.append(jax.ShapeDtypeStruct((t, D_MODEL), BF16))
    return pl.pallas_call(
        functools.partial(_ffn_down_kernel, emit_next=emit_next, n_tiles=n_tiles),
        grid=(n_tiles + 1,),
        in_specs=in_specs,
        out_specs=out_specs,
        out_shape=out_shape,
        scratch_shapes=[pltpu.VMEM((TM_DOWN, D_MODEL), F32)],
        compiler_params=_params("arbitrary"),
        name="ffn_down",
    )(*args)


HEAD_PLAIN, HEAD_NORM, HEAD_NORM_ROPE = 0, 1, 2
A_Q_SCALE = HEAD_DIM ** -0.5 * LOG2_E
C_Q_SCALE = C_QK ** -0.5 * LOG2_E
AB_HEAD_KINDS = (((HEAD_NORM_ROPE, A_Q_SCALE),) * A_HEADS + ((HEAD_NORM_ROPE, 1.0),) * A_KV_HEADS
                 + ((HEAD_PLAIN, 1.0),) * A_KV_HEADS
                 + ((HEAD_NORM, 1.0),) * (2 * B_HEADS) + ((HEAD_PLAIN, 1.0),) * B_HEADS)


def _proj_heads_kernel(h_ref, w_ref, g_ref, cos_ref, sin_ref, o_ref, z_ref, *, step_kinds):
    j = pl.program_id(1)
    n_blocks = len(step_kinds)

    def step(kinds, with_dot):
        if kinds is not None:
            for hh, (kind, out_scale) in enumerate(kinds):
                y = z_ref[:, hh * HEAD_DIM:(hh + 1) * HEAD_DIM]
                if kind != HEAD_PLAIN:
                    y = _rms(y, g_ref[hh], HEAD_DIM)
                if kind == HEAD_NORM_ROPE:
                    y = _rope(y, cos_ref[...], sin_ref[...])
                if out_scale != 1.0:
                    y = y * out_scale
                o_ref[hh] = y.astype(BF16)
        if with_dot:
            z_ref[...] = jnp.dot(h_ref[...], w_ref[...], preferred_element_type=F32)

    pl.when(j == 0)(functools.partial(step, None, True))
    lo = 0
    while lo < n_blocks:
        hi = lo
        while hi < n_blocks and step_kinds[hi] == step_kinds[lo]:
            hi += 1
        last = min(hi, n_blocks - 1)
        if last > lo:
            pl.when((j > lo) & (j <= last))(functools.partial(step, step_kinds[lo], True))
        lo = hi
    pl.when(j == n_blocks)(functools.partial(step, step_kinds[-1], False))


def _proj_heads(h, w_ab, gains, cos, sin):
    t = h.shape[0]
    heads = TN_PROJ // HEAD_DIM
    n_blocks = AB_IN // TN_PROJ
    seq_tiles = SEQ // TM_PROJ
    step_kinds = tuple(AB_HEAD_KINDS[n:n + heads] for n in range(0, AB_HEADS, heads))
    prev = lambda j: jnp.maximum(j - 1, 0)
    return pl.pallas_call(
        functools.partial(_proj_heads_kernel, step_kinds=step_kinds),
        grid=(t // TM_PROJ, n_blocks + 1),
        in_specs=[pl.BlockSpec((TM_PROJ, D_MODEL), lambda i, j: (i, 0)),
                  pl.BlockSpec((D_MODEL, TN_PROJ), lambda i, j: (0, jnp.minimum(j, n_blocks - 1))),
                  pl.BlockSpec((heads, 1, HEAD_DIM), lambda i, j: (prev(j), 0, 0)),
                  pl.BlockSpec((TM_PROJ, HEAD_DIM), lambda i, j: (i % seq_tiles, 0)),
                  pl.BlockSpec((TM_PROJ, HEAD_DIM), lambda i, j: (i % seq_tiles, 0))],
        out_specs=pl.BlockSpec((heads, TM_PROJ, HEAD_DIM), lambda i, j: (prev(j), i, 0)),
        out_shape=jax.ShapeDtypeStruct((AB_HEADS, t, HEAD_DIM), BF16),
        scratch_shapes=[pltpu.VMEM((TM_PROJ, TN_PROJ), F32)],
        compiler_params=_params("parallel", "arbitrary"),
        name="proj_heads",
    )(h, w_ab, gains, cos, sin)


def _proj_c_kernel(h_ref, wc_ref, gqa_ref, gkva_ref, qup_ref, kvup_ref, gq_ref, gk_ref,
                   cos_ref, sin_ref, q_ref, k_ref, v_ref):
    nope_w = C_HEADS * C_NOPE
    half = h_ref.shape[0] // 2
    for part in range(2):
        rows = slice(part * half, (part + 1) * half)
        h = h_ref[rows, :]
        cos = cos_ref[rows, :]
        sin = sin_ref[rows, :]

        zq = jnp.dot(h, wc_ref[:, :C_Q_RANK], preferred_element_type=F32)
        zkv = jnp.dot(h, wc_ref[:, C_Q_RANK:], preferred_element_type=F32)
        q_lat = _rms(zq, gqa_ref[...], C_Q_RANK).astype(BF16)
        cq = jnp.dot(q_lat, qup_ref[...], preferred_element_type=F32)
        kv_lat = _rms(zkv[:, :C_KV_RANK], gkva_ref[...], C_KV_RANK).astype(BF16)
        k_rope = zkv[:, C_KV_RANK:]
        ckv = jnp.dot(kv_lat, kvup_ref[...], preferred_element_type=F32)

        for hh in range(C_HEADS):
            qn = cq[:, hh * C_NOPE:(hh + 1) * C_NOPE]
            qr = cq[:, nope_w + hh * V7X_LANES:nope_w + (hh + 1) * V7X_LANES]
            ms = (jnp.sum(qn * qn, axis=-1, keepdims=True)
                  + jnp.sum(qr * qr, axis=-1, keepdims=True)) * (1.0 / C_QK)
            r = lax.rsqrt(ms + EPS)
            q_ref[hh, rows, 0:C_NOPE] = (((qn * r) * gq_ref[0:1, :]) * C_Q_SCALE).astype(BF16)
            q_ref[hh, rows, C_NOPE:] = (_rope((qr * r) * gq_ref[1:2, :], cos, sin) * C_Q_SCALE).astype(BF16)

        k_rope_sq = jnp.sum(k_rope * k_rope, axis=-1, keepdims=True)
        for hh in range(C_HEADS):
            kn = ckv[:, hh * 2 * C_NOPE:hh * 2 * C_NOPE + C_NOPE]
            ms = (jnp.sum(kn * kn, axis=-1, keepdims=True) + k_rope_sq) * (1.0 / C_QK)
            r = lax.rsqrt(ms + EPS)
            k_ref[hh, rows, 0:C_NOPE] = ((kn * r) * gk_ref[0:1, :]).astype(BF16)
            k_ref[hh, rows, C_NOPE:] = _rope((k_rope * r) * gk_ref[1:2, :], cos, sin).astype(BF16)
            v_ref[hh, rows, :] = ckv[:, hh * 2 * C_NOPE + C_NOPE:(hh + 1) * 2 * C_NOPE].astype(BF16)


def _proj_c(h, w_c, gqa, gkva, q_up, kv_up, gq, gk, cos, sin):
    t = h.shape[0]
    seq_tiles = SEQ // TM_PROJ
    full = lambda a: pl.BlockSpec(a.shape, lambda i: (0,) * a.ndim)
    qk_w = 2 * V7X_LANES
    return pl.pallas_call(
        _proj_c_kernel,
        grid=(t // TM_PROJ,),
        in_specs=[pl.BlockSpec((TM_PROJ, D_MODEL), lambda i: (i, 0)),
                  full(w_c), full(gqa), full(gkva), full(q_up), full(kv_up), full(gq), full(gk),
                  pl.BlockSpec((TM_PROJ, V7X_LANES), lambda i: (i % seq_tiles, 0)),
                  pl.BlockSpec((TM_PROJ, V7X_LANES), lambda i: (i % seq_tiles, 0))],
        out_specs=[pl.BlockSpec((C_HEADS, TM_PROJ, qk_w), lambda i: (0, i, 0)),
                   pl.BlockSpec((C_HEADS, TM_PROJ, qk_w), lambda i: (0, i, 0)),
                   pl.BlockSpec((C_HEADS, TM_PROJ, C_NOPE), lambda i: (0, i, 0))],
        out_shape=[jax.ShapeDtypeStruct((C_HEADS, t, qk_w), BF16),
                   jax.ShapeDtypeStruct((C_HEADS, t, qk_w), BF16),
                   jax.ShapeDtypeStruct((C_HEADS, t, C_NOPE), BF16)],
        compiler_params=_params("parallel"),
        name="proj_c",
    )(h, w_c, gqa, gkva, q_up, kv_up, gq, gk, cos, sin)


def _attn_kernel(q_ref, k_ref, v_ref, o_ref, s0_ref, s1_ref, m0_ref, m1_ref, *, tq):
    g, _, dq = q_ref.shape
    rows = g * tq
    dv = v_ref.shape[-1]
    tiles = SEQ // tq
    contract_last = (((1,), (1,)), ((), ()))
    contract_first = (((0,), (0,)), ((), ()))

    def q_rows(tile):
        start = tile * tq
        return pl.ds(start if isinstance(tile, int) else pl.multiple_of(start, tq), tq)

    def passes(score_tile, score_bufs, out_tile, out_bufs):
        if score_tile is not None:
            sw_ref, mw_ref = score_bufs
            q = q_ref[:, q_rows(score_tile), :].reshape(rows, dq)
            m_new = None
        if out_tile is not None:
            sr_ref, mr_ref = out_bufs
            m = mr_ref[...]
            lsum = jnp.zeros((1, rows), F32)
            acc = jnp.zeros((dv, rows), F32)
        for c in range(SEQ // TK_ATT):
            keys = slice(c * TK_ATT, (c + 1) * TK_ATT)
            if score_tile is not None:
                s = lax.dot_general(k_ref[0, keys, :], q, contract_last, preferred_element_type=F32)
                sw_ref[keys, :] = s
                part = jnp.max(s, axis=0, keepdims=True)
                m_new = part if m_new is None else jnp.maximum(m_new, part)
            if out_tile is not None:
                p = jnp.exp2(sr_ref[keys, :] - m)
                lsum = lsum + jnp.sum(p, axis=0, keepdims=True)
                acc = acc + lax.dot_general(v_ref[0, keys, :], p.astype(BF16), contract_first,
                                            preferred_element_type=F32)
        if score_tile is not None:
            mw_ref[...] = m_new
        if out_tile is not None:
            o = (acc / lsum).T
            o_ref[:, q_rows(out_tile), :] = o.reshape(g, tq, dv).astype(o_ref.dtype)

    buf0, buf1 = (s0_ref, m0_ref), (s1_ref, m1_ref)
    passes(0, buf0, None, None)

    def pair(n, carry):
        tile = 2 * n
        passes(tile + 1, buf1, tile, buf0)

        @pl.when(n < tiles // 2 - 1)
        def _():
            passes(tile + 2, buf0, tile + 1, buf1)

        @pl.when(n == tiles // 2 - 1)
        def _():
            passes(None, None, tile + 1, buf1)

        return carry

    lax.fori_loop(0, tiles // 2, pair, 0)


def _attention(q, k, v, *, kv_heads, group, k_head0, v_head0, tq):
    _, t, dq = q.shape
    dv = v.shape[-1]
    nb = t // SEQ
    rows = group * tq
    return pl.pallas_call(
        functools.partial(_attn_kernel, tq=tq),
        grid=(nb, kv_heads),
        in_specs=[pl.BlockSpec((group, SEQ, dq), lambda b, h: (h, b, 0)),
                  pl.BlockSpec((1, SEQ, dq), lambda b, h: (k_head0 + h, b, 0)),
                  pl.BlockSpec((1, SEQ, dv), lambda b, h: (v_head0 + h, b, 0))],
        out_specs=pl.BlockSpec((group, SEQ, dv), lambda b, h: (h, b, 0)),
        out_shape=jax.ShapeDtypeStruct((kv_heads * group, t, dv), BF16),
        scratch_shapes=[pltpu.VMEM((SEQ, rows), F32)] * 2 + [pltpu.VMEM((1, rows), F32)] * 2,
        compiler_params=_params("parallel", "parallel"),
        name="attention",
    )(q, k, v)


def _attn_b_kernel(q0, k0, v0, q1, k1, v1, q2, k2, v2, coef_ref, tab_ref, o_ref,
                   stage_ref, qd_ref, kd_ref, vd_ref, og_ref, eg_ref):
    scale = HEAD_DIM ** -0.5
    inputs = ((q0, k0, v0), (q1, k1, v1), (q2, k2, v2))
    chunk = 64

    def gather_classes(src_ref, dst_ref, dil):
        length = SEQ // dil
        stage_ref[...] = src_ref[0].astype(F32)

        def body(c, carry):
            for r in range(dil):
                src = pl.ds(pl.multiple_of(c * chunk * dil, chunk * dil) + r, chunk, stride=dil)
                dst = pl.ds(pl.multiple_of(r * length + c * chunk, chunk), chunk)
                dst_ref[dst, :] = stage_ref[src, :].astype(BF16)
            return carry

        lax.fori_loop(0, length // chunk, body, 0)

    ones = jnp.ones((B_WIDTH, HEAD_DIM), BF16)

    def tile(g, q, k, v, steps, coef, out_rows):
        s = lax.dot_general(q, k, (((1,), (1,)), ((), ())), preferred_element_type=F32)
        s = s * scale - coef * steps
        m = jnp.max(s, axis=-1, keepdims=True)
        p = jnp.exp(s - m).astype(BF16)
        ol = jnp.dot(p, jnp.concatenate([v, ones], axis=-1), preferred_element_type=F32)
        l = ol[:, HEAD_DIM:]
        og_ref[g, out_rows, :] = ol[:, :HEAD_DIM] / l
        eg_ref[g, out_rows, :] = m + jnp.log(l)

    for g, (_, dil) in enumerate(B_CONFIGS):
        length = SEQ // dil
        nt = length // TQ_B
        coef = coef_ref[0, g:g + 1, 0:1]
        if dil == 1:
            qs, ks, vs = (lambda rows, ref=ref: ref[0, rows, :] for ref in inputs[g])
        else:
            for src, dst in zip(inputs[g], (qd_ref, kd_ref, vd_ref)):
                gather_classes(src, dst, dil)
            qs, ks, vs = (lambda rows, ref=ref: ref[rows, :] for ref in (qd_ref, kd_ref, vd_ref))

        def residue(r, g=g, dil=dil, length=length, nt=nt, coef=coef, qs=qs, ks=ks, vs=vs):
            base = r * length
            for ti in range(nt):
                l0 = ti * TQ_B
                k0_ = min(max(l0 - B_SPAN, 0), length - B_WIDTH)
                steps = tab_ref[(l0 - k0_) // B_SPAN]
                if dil == 1:
                    q_rows, k_rows, out_rows = pl.ds(l0, TQ_B), pl.ds(k0_, B_WIDTH), pl.ds(l0, TQ_B)
                else:
                    q_rows = pl.ds(pl.multiple_of(base + l0, TQ_B), TQ_B)
                    k_rows = pl.ds(pl.multiple_of(base + k0_, B_SPAN), B_WIDTH)
                    out_rows = pl.ds(l0 * dil + r, TQ_B, stride=dil)
                tile(g, qs(q_rows), ks(k_rows), vs(k_rows), steps, coef, out_rows)

        if dil == 1:
            residue(0)
        else:
            per_step = max(1, B_TILES_PER_BLOCK // nt)

            def step(it, carry, residue=residue, per_step=per_step):
                for rr in range(per_step):
                    residue(it * per_step + rr)
                return carry

            lax.fori_loop(0, dil // per_step, step, 0)

    def merge(c, carry):
        rows = pl.ds(pl.multiple_of(c * B_MERGE_ROWS, B_MERGE_ROWS), B_MERGE_ROWS)
        lse = [eg_ref[g, rows, :] for g in range(len(B_CONFIGS))]
        mx = jnp.maximum(jnp.maximum(lse[0], lse[1]), lse[2])
        w = [jnp.exp(e - mx) for e in lse]
        den = w[0] + w[1] + w[2]
        o_ref[0, rows, :] = sum((w[g] / den) * og_ref[g, rows, :] for g in range(len(B_CONFIGS))).astype(o_ref.dtype)
        return carry

    lax.fori_loop(0, SEQ // B_MERGE_ROWS, merge, 0)


def _attn_b(qkv_heads, coef, tab):
    t = qkv_heads.shape[1]
    nb = t // SEQ
    base = A_IN // HEAD_DIM
    in_specs = []
    for g in range(len(B_CONFIGS)):
        for kind in range(3):
            head0 = base + kind * B_HEADS + g * B_SLOTS
            in_specs.append(pl.BlockSpec((1, SEQ, HEAD_DIM), lambda b, j, head0=head0: (head0 + j, b, 0)))
    in_specs.append(pl.BlockSpec((1, 8, V7X_LANES), lambda b, j: (j, 0, 0)))
    in_specs.append(pl.BlockSpec(tab.shape, lambda b, j: (0, 0, 0)))
    return pl.pallas_call(
        _attn_b_kernel,
        grid=(nb, B_SLOTS),
        in_specs=in_specs,
        out_specs=pl.BlockSpec((1, SEQ, HEAD_DIM), lambda b, j: (j, b, 0)),
        out_shape=jax.ShapeDtypeStruct((B_SLOTS, t, HEAD_DIM), BF16),
        scratch_shapes=[pltpu.VMEM((SEQ, HEAD_DIM), F32)]
                       + [pltpu.VMEM((SEQ, HEAD_DIM), BF16)] * 3
                       + [pltpu.VMEM((len(B_CONFIGS), SEQ, HEAD_DIM), F32)] * 2,
        compiler_params=_params("parallel", "parallel"),
        name="attn_b",
    )(*([qkv_heads] * 9), coef, tab)


def _mix_out_kernel(x_ref, oa_ref, ob_ref, oc_ref, gout_ref, w_ref, gn_ref, xo_ref, ho_ref, y_ref):
    def group_norm(tiles, col0):
        width = len(tiles) * HEAD_DIM
        ssq = sum(jnp.sum(tl * tl, axis=-1, keepdims=True) for tl in tiles)
        r = lax.rsqrt(ssq * (1.0 / width) + EPS)
        for n, tl in enumerate(tiles):
            c = col0 + n * HEAD_DIM
            y_ref[:, c:c + HEAD_DIM] = ((tl * r) * gout_ref[:, c:c + HEAD_DIM]).astype(BF16)

    group_norm([oa_ref[n].astype(F32) for n in range(A_HEADS)], 0)
    group_norm([ob_ref[n].astype(F32) for n in range(B_SLOTS)], A_OUT)
    group_norm([oc_ref[n].astype(F32) for n in range(C_HEADS)], A_OUT + B_OUT)

    xn = x_ref[...] + jnp.dot(y_ref[...], w_ref[...], preferred_element_type=F32)
    xo_ref[...] = xn
    ho_ref[...] = _rms(xn, gn_ref[...], D_MODEL).astype(BF16)


def _mix_out(x, oa, ob, oc, gain_out, w_out, gain_next):
    t = x.shape[0]
    heads = lambda n: pl.BlockSpec((n, TM_OUT, HEAD_DIM), lambda i: (0, i, 0))
    row = pl.BlockSpec((1, D_MODEL), lambda i: (0, 0))
    tok = pl.BlockSpec((TM_OUT, D_MODEL), lambda i: (i, 0))
    return pl.pallas_call(
        _mix_out_kernel,
        grid=(t // TM_OUT,),
        in_specs=[tok, heads(A_HEADS), heads(B_SLOTS), heads(C_HEADS), row,
                  pl.BlockSpec((D_MODEL, D_MODEL), lambda i: (0, 0)), row],
        out_specs=[tok, tok],
        out_shape=[jax.ShapeDtypeStruct((t, D_MODEL), F32),
                   jax.ShapeDtypeStruct((t, D_MODEL), BF16)],
        scratch_shapes=[pltpu.VMEM((TM_OUT, D_MODEL), BF16)],
        compiler_params=_params("parallel"),
        name="mix_out",
    )(x, oa, ob, oc, gain_out, w_out, gain_next)


def _rope_tables(pos_a, pos_b):
    inv = ROPE_THETA ** (-jnp.arange(ROPE_HALF, dtype=F32) / ROPE_HALF)

    def one(pos):
        ang = pos.astype(F32)[:, None] * inv[None, :]
        return jnp.cos(ang), jnp.sin(ang)

    ca, sa = one(pos_a)
    cb, sb = one(pos_b)
    return jnp.concatenate([ca, cb, ca, cb], axis=-1), jnp.concatenate([-sa, -sb, sa, sb], axis=-1)


def _band_steps_table():
    i = jnp.arange(TQ_B)[:, None]
    jj = jnp.arange(B_WIDTH)[None, :]
    tabs = []
    for off in (0, B_SPAN, 2 * B_SPAN):
        rel = jnp.abs(off + i - jj)
        tabs.append(jnp.where(rel <= B_SPAN, rel.astype(F32), -NEG_BIG))
    return jnp.stack(tabs)


def _head_gains(gq_a, gk_a, gq_b, gk_b):
    ones = jnp.ones((HEAD_DIM,), F32)
    rows = ([gq_a] * A_HEADS + [gk_a] * A_KV_HEADS + [ones] * A_KV_HEADS
            + [gq_b] * B_HEADS + [gk_b] * B_HEADS + [ones] * B_HEADS)
    return jnp.stack(rows).astype(F32).reshape(AB_HEADS, 1, HEAD_DIM)


def kernel(x, ffn1_norm, ffn1_w_gu, ffn1_w_down, mix_norm, w_in, a_q_norm, a_k_norm, b_q_norm, b_k_norm, c_q_a_norm, c_q_up, c_kv_a_norm, c_kv_up, c_q_norm, c_k_norm, out_norm, w_out, ffn2_norm, ffn2_w_gu, ffn2_w_down):
    nb, s, d = x.shape
    assert (s, d) == (SEQ, D_MODEL)
    depth = w_in.shape[0]
    t = nb * s
    x = x.reshape(t, d)
    row = lambda v: v.reshape(1, -1).astype(F32)

    pos = jnp.arange(s, dtype=jnp.int32)
    cos_a, sin_a = _rope_tables(pos // GRID_W, pos % GRID_W)
    cos_c, sin_c = _rope_tables(pos, pos)
    lane_order = jnp.array(ROPE_LANE_ORDER)

    def rope_tile(v):
        zeros = jnp.zeros(v.shape[:-1] + (V7X_LANES // 2 - ROPE_HALF,), v.dtype)
        return jnp.concatenate([v[..., :ROPE_HALF], zeros, v[..., ROPE_HALF:], zeros], axis=-1)
    tab = _band_steps_table()
    slopes = 2.0 ** (-8.0 * jnp.arange(1, B_HEADS + 1, dtype=F32) / B_HEADS)
    dil = jnp.array([c[1] for c in B_CONFIGS], F32)
    coef = (slopes.reshape(len(B_CONFIGS), B_SLOTS) * dil[:, None]).T
    coef = jnp.zeros((B_SLOTS, 8, V7X_LANES), F32).at[:, :len(B_CONFIGS), :].set(coef[:, :, None])

    h = _norm(x, row(ffn1_norm[0]))
    for l in range(depth):
        x, h = _ffn(x, h, ffn1_w_gu[l].astype(BF16), ffn1_w_down[l].astype(BF16), row(mix_norm[l]))

        w = w_in[l]
        rot = (A_HEADS + A_KV_HEADS) * HEAD_DIM
        w_rot = w[:, :rot].reshape(D_MODEL, -1, HEAD_DIM)[:, :, lane_order].reshape(D_MODEL, rot)
        w_ab = jnp.concatenate([w_rot, w[:, rot:AB_IN]], axis=-1).astype(BF16)
        qkv = _proj_heads(h, w_ab, _head_gains(a_q_norm[l][lane_order], a_k_norm[l][lane_order],
                                               b_q_norm[l], b_k_norm[l]), cos_a, sin_a)
        w_c = jnp.concatenate([w[:, AB_IN:W_IN - C_ROPE], rope_tile(w[:, W_IN - C_ROPE:])], axis=-1).astype(BF16)
        q_up = c_q_up[l].reshape(C_Q_RANK, C_HEADS, C_QK)
        q_up = jnp.concatenate([q_up[:, :, :C_NOPE].reshape(C_Q_RANK, C_HEADS * C_NOPE),
                                rope_tile(q_up[:, :, C_NOPE:]).reshape(C_Q_RANK, C_HEADS * V7X_LANES)],
                               axis=-1).astype(BF16)
        split = lambda gvec: jnp.stack([gvec[:C_NOPE], rope_tile(gvec[C_NOPE:])]).astype(F32)
        qc, kc, vc = _proj_c(h, w_c, row(c_q_a_norm[l]), row(c_kv_a_norm[l]), q_up, c_kv_up[l].astype(BF16),
                             split(c_q_norm[l]), split(c_k_norm[l]), cos_c, sin_c)

        oa = _attention(qkv, qkv, qkv, kv_heads=A_KV_HEADS, group=A_HEADS // A_KV_HEADS,
                        k_head0=A_HEADS, v_head0=A_HEADS + A_KV_HEADS, tq=128)
        oc = _attention(qc, kc, vc, kv_heads=C_HEADS, group=1, k_head0=0, v_head0=0, tq=512)
        ob = _attn_b(qkv, coef, tab)

        x, h = _mix_out(x, oa, ob, oc, row(out_norm[l]), w_out[l].astype(BF16), row(ffn2_norm[l]))
        w_gu, w_down = ffn2_w_gu[l].astype(BF16), ffn2_w_down[l].astype(BF16)
        if l + 1 < depth:
            x, h = _ffn(x, h, w_gu, w_down, row(ffn1_norm[l + 1]))
        else:
            x, = _ffn(x, h, w_gu, w_down)
    return x.reshape(nb, s, d)
```

```python
import functools

import jax
import jax.numpy as jnp
from jax import lax
from jax.experimental import pallas as pl
from jax.experimental.pallas import tpu as pltpu

F32 = jnp.float32
BF16 = jnp.bfloat16

D_MODEL = 2048
SEQ = 4096
GRID_W = 64
HEAD_DIM = 128
ROPE_THETA = 10000.0
EPS = 1e-6
NEG_BIG = -1e30
LOG2_E = 1.4426950408889634

A_HEADS = 8
A_KV_HEADS = 2
B_CONFIGS = ((128, 1), (512, 4), (2048, 16))
B_SLOTS = 4
B_HEADS = B_SLOTS * len(B_CONFIGS)
C_HEADS = 4
C_Q_RANK = 512
C_KV_RANK = 256
C_NOPE = 128
C_ROPE = 64
C_QK = C_NOPE + C_ROPE
D_FF = 5632

A_Q = A_HEADS * HEAD_DIM
A_KV = A_KV_HEADS * HEAD_DIM
A_IN = A_Q + 2 * A_KV
B_QKV = B_HEADS * HEAD_DIM
B_IN = 3 * B_QKV
AB_IN = A_IN + B_IN
W_IN = AB_IN + C_Q_RANK + C_KV_RANK + C_ROPE
AB_HEADS = AB_IN // HEAD_DIM
A_OUT = A_HEADS * HEAD_DIM
B_OUT = B_SLOTS * HEAD_DIM
C_OUT = C_HEADS * HEAD_DIM

V7X_LANES = 128
V7X_VMEM_LIMIT = 56 * 1024 * 1024

TM_NORM = 512
TM_UP = 1024
TF_FFN = 512
TM_DOWN = 256
TM_PROJ = 1024
TN_PROJ = 512
TM_OUT = 512
TQ_B = 128
B_SPAN = 64
B_WIDTH = TQ_B + 2 * B_SPAN
B_MERGE_ROWS = 256
B_TILES_PER_BLOCK = 8
TK_ATT = 256


def _params(*sem):
    return pltpu.CompilerParams(dimension_semantics=sem, vmem_limit_bytes=V7X_VMEM_LIMIT)


def _rms(x, gain, width):
    ms = jnp.sum(x * x, axis=-1, keepdims=True) * (1.0 / width)
    return (x * lax.rsqrt(ms + EPS)) * gain


ROPE_HALF = 32
ROPE_LANE_ORDER = tuple(list(range(0, 32)) + list(range(64, 96)) + list(range(32, 64)) + list(range(96, 128)))


def _rope(y, cos, sin_signed):
    return y * cos + pltpu.roll(y, V7X_LANES // 2, 1) * sin_signed


def _norm_kernel(x_ref, g_ref, h_ref):
    h_ref[...] = _rms(x_ref[...], g_ref[...], D_MODEL).astype(BF16)


def _norm(x, gain):
    t = x.shape[0]
    return pl.pallas_call(
        _norm_kernel,
        grid=(t // TM_NORM,),
        in_specs=[pl.BlockSpec((TM_NORM, D_MODEL), lambda i: (i, 0)),
                  pl.BlockSpec((1, D_MODEL), lambda i: (0, 0))],
        out_specs=pl.BlockSpec((TM_NORM, D_MODEL), lambda i: (i, 0)),
        out_shape=jax.ShapeDtypeStruct((t, D_MODEL), BF16),
        compiler_params=_params("parallel"),
        name="norm",
    )(x, gain)


def _ffn_up_kernel(h_ref, wg_ref, wu_ref, a_ref, g_ref, u_ref, *, nf):
    j = pl.program_id(1)

    def swiglu():
        g = g_ref[...]
        a_ref[...] = (g * (1.0 / (1.0 + jnp.exp(-g))) * u_ref[...]).astype(BF16)

    def matmuls():
        h = h_ref[...]
        g_ref[...] = jnp.dot(h, wg_ref[0], preferred_element_type=F32)
        u_ref[...] = jnp.dot(h, wu_ref[0], preferred_element_type=F32)

    pl.when(j == 0)(matmuls)

    @pl.when((j > 0) & (j < nf))
    def _():
        swiglu()
        matmuls()

    pl.when(j == nf)(swiglu)


def _ffn_down_kernel(a_ref, wd_ref, x_ref, *rest, emit_next, n_tiles):
    if emit_next:
        gn_ref, xo_ref, ho_ref, d_ref = rest
    else:
        xo_ref, d_ref = rest
    i = pl.program_id(0)

    def finish():
        xn = x_ref[...] + 0.5 * d_ref[...]
        xo_ref[...] = xn
        if emit_next:
            ho_ref[...] = _rms(xn, gn_ref[...], D_MODEL).astype(BF16)

    def matmul():
        d_ref[...] = jnp.dot(a_ref[...], wd_ref[...], preferred_element_type=F32)

    pl.when(i == 0)(matmul)

    @pl.when((i > 0) & (i < n_tiles))
    def _():
        finish()
        matmul()

    pl.when(i == n_tiles)(finish)


def _column_blocks(w, width):
    k, n = w.shape
    return w.reshape(k, n // width, width).transpose(1, 0, 2).astype(BF16)


def _ffn(x, h, w_gu, w_down, gain_next=None):
    t = x.shape[0]
    nf = D_FF // TF_FFN
    cur = lambda j: jnp.minimum(j, nf - 1)
    act = pl.pallas_call(
        functools.partial(_ffn_up_kernel, nf=nf),
        grid=(t // TM_UP, nf + 1),
        in_specs=[pl.BlockSpec((TM_UP, D_MODEL), lambda i, j: (i, 0)),
                  pl.BlockSpec((1, D_MODEL, TF_FFN), lambda i, j: (cur(j), 0, 0)),
                  pl.BlockSpec((1, D_MODEL, TF_FFN), lambda i, j: (cur(j) + nf, 0, 0))],
        out_specs=pl.BlockSpec((TM_UP, TF_FFN), lambda i, j: (i, jnp.maximum(j - 1, 0))),
        out_shape=jax.ShapeDtypeStruct((t, D_FF), BF16),
        scratch_shapes=[pltpu.VMEM((TM_UP, TF_FFN), F32)] * 2,
        compiler_params=_params("parallel", "arbitrary"),
        name="ffn_up",
    )(h, w_gu, w_gu)

    emit_next = gain_next is not None
    n_tiles = t // TM_DOWN
    tok = pl.BlockSpec((TM_DOWN, D_MODEL), lambda i: (jnp.maximum(i - 1, 0), 0))
    in_specs = [pl.BlockSpec((TM_DOWN, D_FF), lambda i: (jnp.minimum(i, n_tiles - 1), 0)),
                pl.BlockSpec((D_FF, D_MODEL), lambda i: (0, 0), pipeline_mode=pl.Buffered(1)),
                tok]
    args = [act, w_down, x]
    out_specs = [tok]
    out_shape = [jax.ShapeDtypeStruct((t, D_MODEL), F32)]
    if emit_next:
        in_specs.append(pl.BlockSpec((1, D_MODEL), lambda i: (0, 0)))
        args.append(gain_next)
        out_specs.append(tok)
        out_shape.append(jax.ShapeDtypeStruct((t, D_MODEL), BF16))
    return pl.pallas_call(
        functools.partial(_ffn_down_kernel, emit_next=emit_next, n_tiles=n_tiles),
        grid=(n_tiles + 1,),
        in_specs=in_specs,
        out_specs=out_specs,
        out_shape=out_shape,
        scratch_shapes=[pltpu.VMEM((TM_DOWN, D_MODEL), F32)],
        compiler_params=_params("arbitrary"),
        name="ffn_down",
    )(*args)


HEAD_PLAIN, HEAD_NORM, HEAD_NORM_ROPE = 0, 1, 2
A_Q_SCALE = HEAD_DIM ** -0.5 * LOG2_E
C_Q_SCALE = C_QK ** -0.5 * LOG2_E
AB_HEAD_KINDS = (((HEAD_NORM_ROPE, A_Q_SCALE),) * A_HEADS + ((HEAD_NORM_ROPE, 1.0),) * A_KV_HEADS
                 + ((HEAD_PLAIN, 1.0),) * A_KV_HEADS
                 + ((HEAD_NORM, 1.0),) * (2 * B_HEADS) + ((HEAD_PLAIN, 1.0),) * B_HEADS)


def _proj_heads_kernel(h_ref, w_ref, g_ref, cos_ref, sin_ref, o_ref, z_ref, *, step_kinds):
    j = pl.program_id(1)
    n_blocks = len(step_kinds)

    def step(kinds, with_dot):
        if kinds is not None:
            for hh, (kind, out_scale) in enumerate(kinds):
                y = z_ref[:, hh * HEAD_DIM:(hh + 1) * HEAD_DIM]
                if kind != HEAD_PLAIN:
                    y = _rms(y, g_ref[hh], HEAD_DIM)
                if kind == HEAD_NORM_ROPE:
                    y = _rope(y, cos_ref[...], sin_ref[...])
                if out_scale != 1.0:
                    y = y * out_scale
                o_ref[hh] = y.astype(BF16)
        if with_dot:
            z_ref[...] = jnp.dot(h_ref[...], w_ref[0], preferred_element_type=F32)

    pl.when(j == 0)(functools.partial(step, None, True))
    lo = 0
    while lo < n_blocks:
        hi = lo
        while hi < n_blocks and step_kinds[hi] == step_kinds[lo]:
            hi += 1
        last = min(hi, n_blocks - 1)
        if last > lo:
            pl.when((j > lo) & (j <= last))(functools.partial(step, step_kinds[lo], True))
        lo = hi
    pl.when(j == n_blocks)(functools.partial(step, step_kinds[-1], False))


def _proj_heads(h, w_ab, gains, cos, sin):
    t = h.shape[0]
    heads = TN_PROJ // HEAD_DIM
    n_blocks = AB_IN // TN_PROJ
    seq_tiles = SEQ // TM_PROJ
    step_kinds = tuple(AB_HEAD_KINDS[n:n + heads] for n in range(0, AB_HEADS, heads))
    prev = lambda j: jnp.maximum(j - 1, 0)
    return pl.pallas_call(
        functools.partial(_proj_heads_kernel, step_kinds=step_kinds),
        grid=(t // TM_PROJ, n_blocks + 1),
        in_specs=[pl.BlockSpec((TM_PROJ, D_MODEL), lambda i, j: (i, 0)),
                  pl.BlockSpec((1, D_MODEL, TN_PROJ), lambda i, j: (jnp.minimum(j, n_blocks - 1), 0, 0)),
                  pl.BlockSpec((heads, 1, HEAD_DIM), lambda i, j: (prev(j), 0, 0)),
                  pl.BlockSpec((TM_PROJ, HEAD_DIM), lambda i, j: (i % seq_tiles, 0)),
                  pl.BlockSpec((TM_PROJ, HEAD_DIM), lambda i, j: (i % seq_tiles, 0))],
        out_specs=pl.BlockSpec((heads, TM_PROJ, HEAD_DIM), lambda i, j: (prev(j), i, 0)),
        out_shape=jax.ShapeDtypeStruct((AB_HEADS, t, HEAD_DIM), BF16),
        scratch_shapes=[pltpu.VMEM((TM_PROJ, TN_PROJ), F32)],
        compiler_params=_params("parallel", "arbitrary"),
        name="proj_heads",
    )(h, w_ab, gains, cos, sin)


def _proj_c_kernel(h_ref, wc_ref, gqa_ref, gkva_ref, qup_ref, kvup_ref, gq_ref, gk_ref,
                   cos_ref, sin_ref, q_ref, k_ref, v_ref):
    nope_w = C_HEADS * C_NOPE
    half = h_ref.shape[0] // 2
    for part in range(2):
        rows = slice(part * half, (part + 1) * half)
        h = h_ref[rows, :]
        cos = cos_ref[rows, :]
        sin = sin_ref[rows, :]

        zq = jnp.dot(h, wc_ref[:, :C_Q_RANK], preferred_element_type=F32)
        zkv = jnp.dot(h, wc_ref[:, C_Q_RANK:], preferred_element_type=F32)
        q_lat = _rms(zq, gqa_ref[...], C_Q_RANK).astype(BF16)
        cq = jnp.dot(q_lat, qup_ref[...], preferred_element_type=F32)
        kv_lat = _rms(zkv[:, :C_KV_RANK], gkva_ref[...], C_KV_RANK).astype(BF16)
        k_rope = zkv[:, C_KV_RANK:]
        ckv = jnp.dot(kv_lat, kvup_ref[...], preferred_element_type=F32)

        for hh in range(C_HEADS):
            qn = cq[:, hh * C_NOPE:(hh + 1) * C_NOPE]
            qr = cq[:, nope_w + hh * V7X_LANES:nope_w + (hh + 1) * V7X_LANES]
            ms = (jnp.sum(qn * qn, axis=-1, keepdims=True)
                  + jnp.sum(qr * qr, axis=-1, keepdims=True)) * (1.0 / C_QK)
            r = lax.rsqrt(ms + EPS)
            q_ref[hh, rows, 0:C_NOPE] = (((qn * r) * gq_ref[0:1, :]) * C_Q_SCALE).astype(BF16)
            q_ref[hh, rows, C_NOPE:] = (_rope((qr * r) * gq_ref[1:2, :], cos, sin) * C_Q_SCALE).astype(BF16)

        k_rope_sq = jnp.sum(k_rope * k_rope, axis=-1, keepdims=True)
        for hh in range(C_HEADS):
            kn = ckv[:, hh * 2 * C_NOPE:hh * 2 * C_NOPE + C_NOPE]
            ms = (jnp.sum(kn * kn, axis=-1, keepdims=True) + k_rope_sq) * (1.0 / C_QK)
            r = lax.rsqrt(ms + EPS)
            k_ref[hh, rows, 0:C_NOPE] = ((kn * r) * gk_ref[0:1, :]).astype(BF16)
            k_ref[hh, rows, C_NOPE:] = _rope((k_rope * r) * gk_ref[1:2, :], cos, sin).astype(BF16)
            v_ref[hh, rows, :] = ckv[:, hh * 2 * C_NOPE + C_NOPE:(hh + 1) * 2 * C_NOPE].astype(BF16)


def _proj_c(h, w_c, gqa, gkva, q_up, kv_up, gq, gk, cos, sin):
    t = h.shape[0]
    seq_tiles = SEQ // TM_PROJ
    full = lambda a: pl.BlockSpec(a.shape, lambda i: (0,) * a.ndim)
    qk_w = 2 * V7X_LANES
    return pl.pallas_call(
        _proj_c_kernel,
        grid=(t // TM_PROJ,),
        in_specs=[pl.BlockSpec((TM_PROJ, D_MODEL), lambda i: (i, 0)),
                  full(w_c), full(gqa), full(gkva), full(q_up), full(kv_up), full(gq), full(gk),
                  pl.BlockSpec((TM_PROJ, V7X_LANES), lambda i: (i % seq_tiles, 0)),
                  pl.BlockSpec((TM_PROJ, V7X_LANES), lambda i: (i % seq_tiles, 0))],
        out_specs=[pl.BlockSpec((C_HEADS, TM_PROJ, qk_w), lambda i: (0, i, 0)),
                   pl.BlockSpec((C_HEADS, TM_PROJ, qk_w), lambda i: (0, i, 0)),
                   pl.BlockSpec((C_HEADS, TM_PROJ, C_NOPE), lambda i: (0, i, 0))],
        out_shape=[jax.ShapeDtypeStruct((C_HEADS, t, qk_w), BF16),
                   jax.ShapeDtypeStruct((C_HEADS, t, qk_w), BF16),
                   jax.ShapeDtypeStruct((C_HEADS, t, C_NOPE), BF16)],
        compiler_params=_params("parallel"),
        name="proj_c",
    )(h, w_c, gqa, gkva, q_up, kv_up, gq, gk, cos, sin)


def _attn_kernel(q_ref, k_ref, v_ref, o_ref, s0_ref, s1_ref, m0_ref, m1_ref, *, tq):
    g, _, dq = q_ref.shape
    rows = g * tq
    dv = v_ref.shape[-1]
    tiles = SEQ // tq
    contract_last = (((1,), (1,)), ((), ()))
    contract_first = (((0,), (0,)), ((), ()))

    def q_rows(tile):
        start = tile * tq
        return pl.ds(start if isinstance(tile, int) else pl.multiple_of(start, tq), tq)

    def passes(score_tile, score_bufs, out_tile, out_bufs):
        if score_tile is not None:
            sw_ref, mw_ref = score_bufs
            q = q_ref[:, q_rows(score_tile), :].reshape(rows, dq)
            m_new = None
        if out_tile is not None:
            sr_ref, mr_ref = out_bufs
            m = mr_ref[...]
            lsum = jnp.zeros((1, rows), F32)
            acc = jnp.zeros((dv, rows), F32)
        for c in range(SEQ // TK_ATT):
            keys = slice(c * TK_ATT, (c + 1) * TK_ATT)
            if score_tile is not None:
                s = lax.dot_general(k_ref[0, keys, :], q, contract_last, preferred_element_type=F32)
                sw_ref[keys, :] = s
                part = jnp.max(s, axis=0, keepdims=True)
                m_new = part if m_new is None else jnp.maximum(m_new, part)
            if out_tile is not None:
                p = jnp.exp2(sr_ref[keys, :] - m)
                lsum = lsum + jnp.sum(p, axis=0, keepdims=True)
                acc = acc + lax.dot_general(v_ref[0, keys, :], p.astype(BF16), contract_first,
                                            preferred_element_type=F32)
        if score_tile is not None:
            mw_ref[...] = m_new
        if out_tile is not None:
            o = (acc / lsum).T
            o_ref[:, q_rows(out_tile), :] = o.reshape(g, tq, dv).astype(o_ref.dtype)

    buf0, buf1 = (s0_ref, m0_ref), (s1_ref, m1_ref)
    passes(0, buf0, None, None)

    def pair(n, carry):
        tile = 2 * n
        passes(tile + 1, buf1, tile, buf0)

        @pl.when(n < tiles // 2 - 1)
        def _():
            passes(tile + 2, buf0, tile + 1, buf1)

        @pl.when(n == tiles // 2 - 1)
        def _():
            passes(None, None, tile + 1, buf1)

        return carry

    lax.fori_loop(0, tiles // 2, pair, 0)


def _attention(q, k, v, *, kv_heads, group, k_head0, v_head0, tq):
    _, t, dq = q.shape
    dv = v.shape[-1]
    nb = t // SEQ
    rows = group * tq
    return pl.pallas_call(
        functools.partial(_attn_kernel, tq=tq),
        grid=(nb, kv_heads),
        in_specs=[pl.BlockSpec((group, SEQ, dq), lambda b, h: (h, b, 0)),
                  pl.BlockSpec((1, SEQ, dq), lambda b, h: (k_head0 + h, b, 0)),
                  pl.BlockSpec((1, SEQ, dv), lambda b, h: (v_head0 + h, b, 0))],
        out_specs=pl.BlockSpec((group, SEQ, dv), lambda b, h: (h, b, 0)),
        out_shape=jax.ShapeDtypeStruct((kv_heads * group, t, dv), BF16),
        scratch_shapes=[pltpu.VMEM((SEQ, rows), F32)] * 2 + [pltpu.VMEM((1, rows), F32)] * 2,
        compiler_params=_params("parallel", "parallel"),
        name="attention",
    )(q, k, v)


def _attn_b_kernel(q0, k0, v0, q1, k1, v1, q2, k2, v2, coef_ref, tab_ref, o_ref,
                   stage_ref, qd_ref, kd_ref, vd_ref, og_ref, eg_ref):
    scale = HEAD_DIM ** -0.5
    inputs = ((q0, k0, v0), (q1, k1, v1), (q2, k2, v2))
    chunk = 64

    def gather_classes(src_ref, dst_ref, dil):
        length = SEQ // dil
        stage_ref[...] = src_ref[0].astype(F32)

        def body(c, carry):
            for r in range(dil):
                src = pl.ds(pl.multiple_of(c * chunk * dil, chunk * dil) + r, chunk, stride=dil)
                dst = pl.ds(pl.multiple_of(r * length + c * chunk, chunk), chunk)
                dst_ref[dst, :] = stage_ref[src, :].astype(BF16)
            return carry

        lax.fori_loop(0, length // chunk, body, 0)

    ones = jnp.ones((B_WIDTH, HEAD_DIM), BF16)

    def tile(g, q, k, v, steps, coef, out_rows):
        s = lax.dot_general(q, k, (((1,), (1,)), ((), ())), preferred_element_type=F32)
        s = s * scale - coef * steps
        m = jnp.max(s, axis=-1, keepdims=True)
        p = jnp.exp(s - m).astype(BF16)
        ol = jnp.dot(p, jnp.concatenate([v, ones], axis=-1), preferred_element_type=F32)
        l = ol[:, HEAD_DIM:]
        og_ref[g, out_rows, :] = ol[:, :HEAD_DIM] / l
        eg_ref[g, out_rows, :] = m + jnp.log(l)

    for g, (_, dil) in enumerate(B_CONFIGS):
        length = SEQ // dil
        nt = length // TQ_B
        coef = coef_ref[0, g:g + 1, 0:1]
        if dil == 1:
            qs, ks, vs = (lambda rows, ref=ref: ref[0, rows, :] for ref in inputs[g])
        else:
            for src, dst in zip(inputs[g], (qd_ref, kd_ref, vd_ref)):
                gather_classes(src, dst, dil)
            qs, ks, vs = (lambda rows, ref=ref: ref[rows, :] for ref in (qd_ref, kd_ref, vd_ref))

        def residue(r, g=g, dil=dil, length=length, nt=nt, coef=coef, qs=qs, ks=ks, vs=vs):
            base = r * length
            for ti in range(nt):
                l0 = ti * TQ_B
                k0_ = min(max(l0 - B_SPAN, 0), length - B_WIDTH)
                steps = tab_ref[(l0 - k0_) // B_SPAN]
                if dil == 1:
                    q_rows, k_rows, out_rows = pl.ds(l0, TQ_B), pl.ds(k0_, B_WIDTH), pl.ds(l0, TQ_B)
                else:
                    q_rows = pl.ds(pl.multiple_of(base + l0, TQ_B), TQ_B)
                    k_rows = pl.ds(pl.multiple_of(base + k0_, B_SPAN), B_WIDTH)
                    out_rows = pl.ds(l0 * dil + r, TQ_B, stride=dil)
                tile(g, qs(q_rows), ks(k_rows), vs(k_rows), steps, coef, out_rows)

        if dil == 1:
            residue(0)
        else:
            per_step = max(1, B_TILES_PER_BLOCK // nt)

            def step(it, carry, residue=residue, per_step=per_step):
                for rr in range(per_step):
                    residue(it * per_step + rr)
                return carry

            lax.fori_loop(0, dil // per_step, step, 0)

    def merge(c, carry):
        rows = pl.ds(pl.multiple_of(c * B_MERGE_ROWS, B_MERGE_ROWS), B_MERGE_ROWS)
        lse = [eg_ref[g, rows, :] for g in range(len(B_CONFIGS))]
        mx = jnp.maximum(jnp.maximum(lse[0], lse[1]), lse[2])
        w = [jnp.exp(e - mx) for e in lse]
        den = w[0] + w[1] + w[2]
        o_ref[0, rows, :] = sum((w[g] / den) * og_ref[g, rows, :] for g in range(len(B_CONFIGS))).astype(o_ref.dtype)
        return carry

    lax.fori_loop(0, SEQ // B_MERGE_ROWS, merge, 0)


def _attn_b(qkv_heads, coef, tab):
    t = qkv_heads.shape[1]
    nb = t // SEQ
    base = A_IN // HEAD_DIM
    in_specs = []
    for g in range(len(B_CONFIGS)):
        for kind in range(3):
            head0 = base + kind * B_HEADS + g * B_SLOTS
            in_specs.append(pl.BlockSpec((1, SEQ, HEAD_DIM), lambda b, j, head0=head0: (head0 + j, b, 0)))
    in_specs.append(pl.BlockSpec((1, 8, V7X_LANES), lambda b, j: (j, 0, 0)))
    in_specs.append(pl.BlockSpec(tab.shape, lambda b, j: (0, 0, 0)))
    return pl.pallas_call(
        _attn_b_kernel,
        grid=(nb, B_SLOTS),
        in_specs=in_specs,
        out_specs=pl.BlockSpec((1, SEQ, HEAD_DIM), lambda b, j: (j, b, 0)),
        out_shape=jax.ShapeDtypeStruct((B_SLOTS, t, HEAD_DIM), BF16),
        scratch_shapes=[pltpu.VMEM((SEQ, HEAD_DIM), F32)]
                       + [pltpu.VMEM((SEQ, HEAD_DIM), BF16)] * 3
                       + [pltpu.VMEM((len(B_CONFIGS), SEQ, HEAD_DIM), F32)] * 2,
        compiler_params=_params("parallel", "parallel"),
        name="attn_b",
    )(*([qkv_heads] * 9), coef, tab)


def _mix_out_kernel(x_ref, oa_ref, ob_ref, oc_ref, gout_ref, w_ref, gn_ref, xo_ref, ho_ref, y_ref, d_ref,
                    *, n_tiles):
    i = pl.program_id(0)

    def group_norm(tiles, col0):
        width = len(tiles) * HEAD_DIM
        ssq = sum(jnp.sum(tl * tl, axis=-1, keepdims=True) for tl in tiles)
        r = lax.rsqrt(ssq * (1.0 / width) + EPS)
        for n, tl in enumerate(tiles):
            c = col0 + n * HEAD_DIM
            y_ref[:, c:c + HEAD_DIM] = ((tl * r) * gout_ref[:, c:c + HEAD_DIM]).astype(BF16)

    def finish():
        xn = x_ref[...] + d_ref[...]
        xo_ref[...] = xn
        ho_ref[...] = _rms(xn, gn_ref[...], D_MODEL).astype(BF16)

    def project():
        group_norm([oa_ref[n].astype(F32) for n in range(A_HEADS)], 0)
        group_norm([ob_ref[n].astype(F32) for n in range(B_SLOTS)], A_OUT)
        group_norm([oc_ref[n].astype(F32) for n in range(C_HEADS)], A_OUT + B_OUT)
        d_ref[...] = jnp.dot(y_ref[...], w_ref[...], preferred_element_type=F32)

    pl.when(i == 0)(project)

    @pl.when((i > 0) & (i < n_tiles))
    def _():
        finish()
        project()

    pl.when(i == n_tiles)(finish)


def _mix_out(x, oa, ob, oc, gain_out, w_out, gain_next):
    t = x.shape[0]
    n_tiles = t // TM_OUT
    cur = lambda i: jnp.minimum(i, n_tiles - 1)
    heads = lambda n: pl.BlockSpec((n, TM_OUT, HEAD_DIM), lambda i: (0, cur(i), 0))
    row = pl.BlockSpec((1, D_MODEL), lambda i: (0, 0))
    tok = pl.BlockSpec((TM_OUT, D_MODEL), lambda i: (jnp.maximum(i - 1, 0), 0))
    return pl.pallas_call(
        functools.partial(_mix_out_kernel, n_tiles=n_tiles),
        grid=(n_tiles + 1,),
        in_specs=[tok, heads(A_HEADS), heads(B_SLOTS), heads(C_HEADS), row,
                  pl.BlockSpec((D_MODEL, D_MODEL), lambda i: (0, 0), pipeline_mode=pl.Buffered(1)), row],
        out_specs=[tok, tok],
        out_shape=[jax.ShapeDtypeStruct((t, D_MODEL), F32),
                   jax.ShapeDtypeStruct((t, D_MODEL), BF16)],
        scratch_shapes=[pltpu.VMEM((TM_OUT, D_MODEL), BF16), pltpu.VMEM((TM_OUT, D_MODEL), F32)],
        compiler_params=_params("arbitrary"),
        name="mix_out",
    )(x, oa, ob, oc, gain_out, w_out, gain_next)


def _rope_tables(pos_a, pos_b):
    inv = ROPE_THETA ** (-jnp.arange(ROPE_HALF, dtype=F32) / ROPE_HALF)

    def one(pos):
        ang = pos.astype(F32)[:, None] * inv[None, :]
        return jnp.cos(ang), jnp.sin(ang)

    ca, sa = one(pos_a)
    cb, sb = one(pos_b)
    return jnp.concatenate([ca, cb, ca, cb], axis=-1), jnp.concatenate([-sa, -sb, sa, sb], axis=-1)


def _band_steps_table():
    i = jnp.arange(TQ_B)[:, None]
    jj = jnp.arange(B_WIDTH)[None, :]
    tabs = []
    for off in (0, B_SPAN, 2 * B_SPAN):
        rel = jnp.abs(off + i - jj)
        tabs.append(jnp.where(rel <= B_SPAN, rel.astype(F32), -NEG_BIG))
    return jnp.stack(tabs)


def _head_gains(gq_a, gk_a, gq_b, gk_b):
    ones = jnp.ones((HEAD_DIM,), F32)
    rows = ([gq_a] * A_HEADS + [gk_a] * A_KV_HEADS + [ones] * A_KV_HEADS
            + [gq_b] * B_HEADS + [gk_b] * B_HEADS + [ones] * B_HEADS)
    return jnp.stack(rows).astype(F32).reshape(AB_HEADS, 1, HEAD_DIM)


def kernel(x, ffn1_norm, ffn1_w_gu, ffn1_w_down, mix_norm, w_in, a_q_norm, a_k_norm, b_q_norm, b_k_norm, c_q_a_norm, c_q_up, c_kv_a_norm, c_kv_up, c_q_norm, c_k_norm, out_norm, w_out, ffn2_norm, ffn2_w_gu, ffn2_w_down):
    nb, s, d = x.shape
    assert (s, d) == (SEQ, D_MODEL)
    depth = w_in.shape[0]
    t = nb * s
    x = x.reshape(t, d)
    row = lambda v: v.reshape(1, -1).astype(F32)

    pos = jnp.arange(s, dtype=jnp.int32)
    cos_a, sin_a = _rope_tables(pos // GRID_W, pos % GRID_W)
    cos_c, sin_c = _rope_tables(pos, pos)
    lane_order = jnp.array(ROPE_LANE_ORDER)

    def rope_tile(v):
        zeros = jnp.zeros(v.shape[:-1] + (V7X_LANES // 2 - ROPE_HALF,), v.dtype)
        return jnp.concatenate([v[..., :ROPE_HALF], zeros, v[..., ROPE_HALF:], zeros], axis=-1)
    tab = _band_steps_table()
    slopes = 2.0 ** (-8.0 * jnp.arange(1, B_HEADS + 1, dtype=F32) / B_HEADS)
    dil = jnp.array([c[1] for c in B_CONFIGS], F32)
    coef = (slopes.reshape(len(B_CONFIGS), B_SLOTS) * dil[:, None]).T
    coef = jnp.zeros((B_SLOTS, 8, V7X_LANES), F32).at[:, :len(B_CONFIGS), :].set(coef[:, :, None])

    h = _norm(x, row(ffn1_norm[0]))
    for l in range(depth):
        x, h = _ffn(x, h, _column_blocks(ffn1_w_gu[l], TF_FFN), ffn1_w_down[l].astype(BF16), row(mix_norm[l]))

        w = w_in[l]
        rot = (A_HEADS + A_KV_HEADS) * HEAD_DIM
        w_rot = w[:, :rot].reshape(D_MODEL, -1, HEAD_DIM)[:, :, lane_order].reshape(D_MODEL, rot)
        w_ab = _column_blocks(jnp.concatenate([w_rot, w[:, rot:AB_IN]], axis=-1), TN_PROJ)
        qkv = _proj_heads(h, w_ab, _head_gains(a_q_norm[l][lane_order], a_k_norm[l][lane_order],
                                               b_q_norm[l], b_k_norm[l]), cos_a, sin_a)
        w_c = jnp.concatenate([w[:, AB_IN:W_IN - C_ROPE], rope_tile(w[:, W_IN - C_ROPE:])], axis=-1).astype(BF16)
        q_up = c_q_up[l].reshape(C_Q_RANK, C_HEADS, C_QK)
        q_up = jnp.concatenate([q_up[:, :, :C_NOPE].reshape(C_Q_RANK, C_HEADS * C_NOPE),
                                rope_tile(q_up[:, :, C_NOPE:]).reshape(C_Q_RANK, C_HEADS * V7X_LANES)],
                               axis=-1).astype(BF16)
        split = lambda gvec: jnp.stack([gvec[:C_NOPE], rope_tile(gvec[C_NOPE:])]).astype(F32)
        qc, kc, vc = _proj_c(h, w_c, row(c_q_a_norm[l]), row(c_kv_a_norm[l]), q_up, c_kv_up[l].astype(BF16),
                             split(c_q_norm[l]), split(c_k_norm[l]), cos_c, sin_c)

        oa = _attention(qkv, qkv, qkv, kv_heads=A_KV_HEADS, group=A_HEADS // A_KV_HEADS,
                        k_head0=A_HEADS, v_head0=A_HEADS + A_KV_HEADS, tq=128)
        oc = _attention(qc, kc, vc, kv_heads=C_HEADS, group=1, k_head0=0, v_head0=0, tq=512)
        ob = _attn_b(qkv, coef, tab)

        x, h = _mix_out(x, oa, ob, oc, row(out_norm[l]), w_out[l].astype(BF16), row(ffn2_norm[l]))
        w_gu, w_down = _column_blocks(ffn2_w_gu[l], TF_FFN), ffn2_w_down[l].astype(BF16)
        if l + 1 < depth:
            x, h = _ffn(x, h, w_gu, w_down, row(ffn1_norm[l + 1]))
        else:
            x, = _ffn(x, h, w_gu, w_down)
    return x.reshape(nb, s, d)
```

```python
import functools

import jax
import jax.numpy as jnp
from jax import lax
from jax.experimental import pallas as pl
from jax.experimental.pallas import tpu as pltpu

F32 = jnp.float32
BF16 = jnp.bfloat16

D_MODEL = 2048
SEQ = 4096
GRID_W = 64
HEAD_DIM = 128
ROPE_THETA = 10000.0
EPS = 1e-6
NEG_BIG = -1e30
LOG2_E = 1.4426950408889634

A_HEADS = 8
A_KV_HEADS = 2
B_CONFIGS = ((128, 1), (512, 4), (2048, 16))
B_SLOTS = 4
B_HEADS = B_SLOTS * len(B_CONFIGS)
C_HEADS = 4
C_Q_RANK = 512
C_KV_RANK = 256
C_NOPE = 128
C_ROPE = 64
C_QK = C_NOPE + C_ROPE
D_FF = 5632

A_Q = A_HEADS * HEAD_DIM
A_KV = A_KV_HEADS * HEAD_DIM
A_IN = A_Q + 2 * A_KV
B_QKV = B_HEADS * HEAD_DIM
B_IN = 3 * B_QKV
AB_IN = A_IN + B_IN
W_IN = AB_IN + C_Q_RANK + C_KV_RANK + C_ROPE
AB_HEADS = AB_IN // HEAD_DIM
A_OUT = A_HEADS * HEAD_DIM
B_OUT = B_SLOTS * HEAD_DIM
C_OUT = C_HEADS * HEAD_DIM

V7X_LANES = 128
V7X_VMEM_LIMIT = 56 * 1024 * 1024

TM_NORM = 512
TM_UP = 2048
TF_FFN = 512
TM_DOWN = 256
TM_PROJ = 2048
TN_PROJ = 512
TM_OUT = 512
TQ_B = 128
B_SPAN = 64
B_WIDTH = TQ_B + 2 * B_SPAN
B_MERGE_ROWS = 256
B_TILES_PER_BLOCK = 8
TQ_A, TK_A = 128, 256
TQ_C, TK_C = 512, 512


def _params(*sem):
    return pltpu.CompilerParams(dimension_semantics=sem, vmem_limit_bytes=V7X_VMEM_LIMIT)


def _rms(x, gain, width):
    ms = jnp.sum(x * x, axis=-1, keepdims=True) * (1.0 / width)
    return (x * lax.rsqrt(ms + EPS)) * gain


ROPE_HALF = 32
ROPE_LANE_ORDER = tuple(list(range(0, 32)) + list(range(64, 96)) + list(range(32, 64)) + list(range(96, 128)))


def _rope(y, cos, sin_signed):
    return y * cos + pltpu.roll(y, V7X_LANES // 2, 1) * sin_signed


def _norm_kernel(x_ref, g_ref, h_ref):
    h_ref[...] = _rms(x_ref[...], g_ref[...], D_MODEL).astype(BF16)


def _norm(x, gain):
    t = x.shape[0]
    return pl.pallas_call(
        _norm_kernel,
        grid=(t // TM_NORM,),
        in_specs=[pl.BlockSpec((TM_NORM, D_MODEL), lambda i: (i, 0)),
                  pl.BlockSpec((1, D_MODEL), lambda i: (0, 0))],
        out_specs=pl.BlockSpec((TM_NORM, D_MODEL), lambda i: (i, 0)),
        out_shape=jax.ShapeDtypeStruct((t, D_MODEL), BF16),
        compiler_params=_params("parallel"),
        name="norm",
    )(x, gain)


def _ffn_up_kernel(h_ref, wg_ref, wu_ref, a_ref, g_ref, u_ref, *, nf):
    j = pl.program_id(1)

    def swiglu():
        g = g_ref[...]
        a_ref[...] = (g * (1.0 / (1.0 + jnp.exp(-g))) * u_ref[...]).astype(BF16)

    def matmuls():
        h = h_ref[...]
        g_ref[...] = jnp.dot(h, wg_ref[...], preferred_element_type=F32)
        u_ref[...] = jnp.dot(h, wu_ref[...], preferred_element_type=F32)

    pl.when(j == 0)(matmuls)

    @pl.when((j > 0) & (j < nf))
    def _():
        swiglu()
        matmuls()

    pl.when(j == nf)(swiglu)


def _ffn_down_kernel(a_ref, wd_ref, x_ref, *rest, emit_next, n_tiles):
    if emit_next:
        gn_ref, xo_ref, ho_ref, d_ref = rest
    else:
        xo_ref, d_ref = rest
    i = pl.program_id(0)

    def finish():
        xn = x_ref[...] + 0.5 * d_ref[...]
        xo_ref[...] = xn
        if emit_next:
            ho_ref[...] = _rms(xn, gn_ref[...], D_MODEL).astype(BF16)

    def matmul():
        d_ref[...] = jnp.dot(a_ref[...], wd_ref[...], preferred_element_type=F32)

    pl.when(i == 0)(matmul)

    @pl.when((i > 0) & (i < n_tiles))
    def _():
        finish()
        matmul()

    pl.when(i == n_tiles)(finish)


def _ffn(x, h, w_gu, w_down, gain_next=None):
    t = x.shape[0]
    nf = D_FF // TF_FFN
    cur = lambda j: jnp.minimum(j, nf - 1)
    act = pl.pallas_call(
        functools.partial(_ffn_up_kernel, nf=nf),
        grid=(t // TM_UP, nf + 1),
        in_specs=[pl.BlockSpec((TM_UP, D_MODEL), lambda i, j: (i, 0)),
                  pl.BlockSpec((D_MODEL, TF_FFN), lambda i, j: (0, cur(j))),
                  pl.BlockSpec((D_MODEL, TF_FFN), lambda i, j: (0, cur(j) + nf))],
        out_specs=pl.BlockSpec((TM_UP, TF_FFN), lambda i, j: (i, jnp.maximum(j - 1, 0))),
        out_shape=jax.ShapeDtypeStruct((t, D_FF), BF16),
        scratch_shapes=[pltpu.VMEM((TM_UP, TF_FFN), F32)] * 2,
        compiler_params=_params("parallel", "arbitrary"),
        name="ffn_up",
    )(h, w_gu, w_gu)

    emit_next = gain_next is not None
    n_tiles = t // TM_DOWN
    tok = pl.BlockSpec((TM_DOWN, D_MODEL), lambda i: (jnp.maximum(i - 1, 0), 0))
    in_specs = [pl.BlockSpec((TM_DOWN, D_FF), lambda i: (jnp.minimum(i, n_tiles - 1), 0)),
                pl.BlockSpec((D_FF, D_MODEL), lambda i: (0, 0), pipeline_mode=pl.Buffered(1)),
                tok]
    args = [act, w_down, x]
    out_specs = [tok]
    out_shape = [jax.ShapeDtypeStruct((t, D_MODEL), F32)]
    if emit_next:
        in_specs.append(pl.BlockSpec((1, D_MODEL), lambda i: (0, 0)))
        args.append(gain_next)
        out_specs.append(tok)
        out_shape.append(jax.ShapeDtypeStruct((t, D_MODEL), BF16))
    return pl.pallas_call(
        functools.partial(_ffn_down_kernel, emit_next=emit_next, n_tiles=n_tiles),
        grid=(n_tiles + 1,),
        in_specs=in_specs,
        out_specs=out_specs,
        out_shape=out_shape,
        scratch_shapes=[pltpu.VMEM((TM_DOWN, D_MODEL), F32)],
        compiler_params=_params("arbitrary"),
        name="ffn_down",
    )(*args)


HEAD_PLAIN, HEAD_NORM, HEAD_NORM_ROPE = 0, 1, 2
A_Q_SCALE = HEAD_DIM ** -0.5 * LOG2_E
C_Q_SCALE = C_QK ** -0.5 * LOG2_E
AB_HEAD_KINDS = (((HEAD_NORM_ROPE, A_Q_SCALE),) * A_HEADS + ((HEAD_NORM_ROPE, 1.0),) * A_KV_HEADS
                 + ((HEAD_PLAIN, 1.0),) * A_KV_HEADS
                 + ((HEAD_NORM, 1.0),) * (2 * B_HEADS) + ((HEAD_PLAIN, 1.0),) * B_HEADS)


def _proj_heads_kernel(h_ref, w_ref, g_ref, cos_ref, sin_ref, o_ref, z_ref, *, step_kinds):
    j = pl.program_id(1)
    n_blocks = len(step_kinds)

    def step(kinds, with_dot):
        if kinds is not None:
            for hh, (kind, out_scale) in enumerate(kinds):
                y = z_ref[:, hh * HEAD_DIM:(hh + 1) * HEAD_DIM]
                if kind != HEAD_PLAIN:
                    y = _rms(y, g_ref[hh], HEAD_DIM)
                if kind == HEAD_NORM_ROPE:
                    y = _rope(y, cos_ref[...], sin_ref[...])
                if out_scale != 1.0:
                    y = y * out_scale
                o_ref[hh] = y.astype(BF16)
        if with_dot:
            z_ref[...] = jnp.dot(h_ref[...], w_ref[...], preferred_element_type=F32)

    pl.when(j == 0)(functools.partial(step, None, True))
    lo = 0
    while lo < n_blocks:
        hi = lo
        while hi < n_blocks and step_kinds[hi] == step_kinds[lo]:
            hi += 1
        last = min(hi, n_blocks - 1)
        if last > lo:
            pl.when((j > lo) & (j <= last))(functools.partial(step, step_kinds[lo], True))
        lo = hi
    pl.when(j == n_blocks)(functools.partial(step, step_kinds[-1], False))


def _proj_heads(h, w_ab, gains, cos, sin):
    t = h.shape[0]
    heads = TN_PROJ // HEAD_DIM
    n_blocks = AB_IN // TN_PROJ
    seq_tiles = SEQ // TM_PROJ
    step_kinds = tuple(AB_HEAD_KINDS[n:n + heads] for n in range(0, AB_HEADS, heads))
    prev = lambda j: jnp.maximum(j - 1, 0)
    return pl.pallas_call(
        functools.partial(_proj_heads_kernel, step_kinds=step_kinds),
        grid=(t // TM_PROJ, n_blocks + 1),
        in_specs=[pl.BlockSpec((TM_PROJ, D_MODEL), lambda i, j: (i, 0)),
                  pl.BlockSpec((D_MODEL, TN_PROJ), lambda i, j: (0, jnp.minimum(j, n_blocks - 1))),
                  pl.BlockSpec((heads, 1, HEAD_DIM), lambda i, j: (prev(j), 0, 0)),
                  pl.BlockSpec((TM_PROJ, HEAD_DIM), lambda i, j: (i % seq_tiles, 0)),
                  pl.BlockSpec((TM_PROJ, HEAD_DIM), lambda i, j: (i % seq_tiles, 0))],
        out_specs=pl.BlockSpec((heads, TM_PROJ, HEAD_DIM), lambda i, j: (prev(j), i, 0)),
        out_shape=jax.ShapeDtypeStruct((AB_HEADS, t, HEAD_DIM), BF16),
        scratch_shapes=[pltpu.VMEM((TM_PROJ, TN_PROJ), F32)],
        compiler_params=_params("parallel", "arbitrary"),
        name="proj_heads",
    )(h, w_ab, gains, cos, sin)


def _proj_c_kernel(h_ref, wc_ref, gqa_ref, gkva_ref, qup_ref, kvup_ref, gq_ref, gk_ref,
                   cos_ref, sin_ref, q_ref, k_ref, v_ref):
    nope_w = C_HEADS * C_NOPE
    half = h_ref.shape[0] // 2
    for part in range(2):
        rows = slice(part * half, (part + 1) * half)
        h = h_ref[rows, :]
        cos = cos_ref[rows, :]
        sin = sin_ref[rows, :]

        zq = jnp.dot(h, wc_ref[:, :C_Q_RANK], preferred_element_type=F32)
        zkv = jnp.dot(h, wc_ref[:, C_Q_RANK:], preferred_element_type=F32)
        q_lat = _rms(zq, gqa_ref[...], C_Q_RANK).astype(BF16)
        cq = jnp.dot(q_lat, qup_ref[...], preferred_element_type=F32)
        kv_lat = _rms(zkv[:, :C_KV_RANK], gkva_ref[...], C_KV_RANK).astype(BF16)
        k_rope = zkv[:, C_KV_RANK:]
        ckv = jnp.dot(kv_lat, kvup_ref[...], preferred_element_type=F32)

        for hh in range(C_HEADS):
            qn = cq[:, hh * C_NOPE:(hh + 1) * C_NOPE]
            qr = cq[:, nope_w + hh * V7X_LANES:nope_w + (hh + 1) * V7X_LANES]
            ms = (jnp.sum(qn * qn, axis=-1, keepdims=True)
                  + jnp.sum(qr * qr, axis=-1, keepdims=True)) * (1.0 / C_QK)
            r = lax.rsqrt(ms + EPS)
            q_ref[hh, rows, 0:C_NOPE] = (((qn * r) * gq_ref[0:1, :]) * C_Q_SCALE).astype(BF16)
            q_ref[hh, rows, C_NOPE:] = (_rope((qr * r) * gq_ref[1:2, :], cos, sin) * C_Q_SCALE).astype(BF16)

        k_rope_sq = jnp.sum(k_rope * k_rope, axis=-1, keepdims=True)
        for hh in range(C_HEADS):
            kn = ckv[:, hh * 2 * C_NOPE:hh * 2 * C_NOPE + C_NOPE]
            ms = (jnp.sum(kn * kn, axis=-1, keepdims=True) + k_rope_sq) * (1.0 / C_QK)
            r = lax.rsqrt(ms + EPS)
            k_ref[hh, rows, 0:C_NOPE] = ((kn * r) * gk_ref[0:1, :]).astype(BF16)
            k_ref[hh, rows, C_NOPE:] = _rope((k_rope * r) * gk_ref[1:2, :], cos, sin).astype(BF16)
            v_ref[hh, rows, :] = ckv[:, hh * 2 * C_NOPE + C_NOPE:(hh + 1) * 2 * C_NOPE].astype(BF16)


def _proj_c(h, w_c, gqa, gkva, q_up, kv_up, gq, gk, cos, sin):
    t = h.shape[0]
    seq_tiles = SEQ // TM_PROJ
    full = lambda a: pl.BlockSpec(a.shape, lambda i: (0,) * a.ndim)
    qk_w = 2 * V7X_LANES
    return pl.pallas_call(
        _proj_c_kernel,
        grid=(t // TM_PROJ,),
        in_specs=[pl.BlockSpec((TM_PROJ, D_MODEL), lambda i: (i, 0)),
                  full(w_c), full(gqa), full(gkva), full(q_up), full(kv_up), full(gq), full(gk),
                  pl.BlockSpec((TM_PROJ, V7X_LANES), lambda i: (i % seq_tiles, 0)),
                  pl.BlockSpec((TM_PROJ, V7X_LANES), lambda i: (i % seq_tiles, 0))],
        out_specs=[pl.BlockSpec((C_HEADS, TM_PROJ, qk_w), lambda i: (0, i, 0)),
                   pl.BlockSpec((C_HEADS, TM_PROJ, qk_w), lambda i: (0, i, 0)),
                   pl.BlockSpec((C_HEADS, TM_PROJ, C_NOPE), lambda i: (0, i, 0))],
        out_shape=[jax.ShapeDtypeStruct((C_HEADS, t, qk_w), BF16),
                   jax.ShapeDtypeStruct((C_HEADS, t, qk_w), BF16),
                   jax.ShapeDtypeStruct((C_HEADS, t, C_NOPE), BF16)],
        compiler_params=_params("parallel"),
        name="proj_c",
    )(h, w_c, gqa, gkva, q_up, kv_up, gq, gk, cos, sin)


def _attn_kernel(q_ref, k_ref, v_ref, o_ref, s0_ref, s1_ref, m0_ref, m1_ref, *, tq, tk):
    g, _, dq = q_ref.shape
    rows = g * tq
    dv = v_ref.shape[-1]
    tiles = SEQ // tq
    contract_last = (((1,), (1,)), ((), ()))
    contract_first = (((0,), (0,)), ((), ()))

    def q_rows(tile):
        start = tile * tq
        return pl.ds(start if isinstance(tile, int) else pl.multiple_of(start, tq), tq)

    def passes(score_tile, score_bufs, out_tile, out_bufs):
        if score_tile is not None:
            sw_ref, mw_ref = score_bufs
            q = q_ref[:, q_rows(score_tile), :].reshape(rows, dq)
            m_new = None
        if out_tile is not None:
            sr_ref, mr_ref = out_bufs
            m = mr_ref[...]
            lsum = jnp.zeros((1, rows), F32)
            acc = jnp.zeros((dv, rows), F32)
        for c in range(SEQ // tk):
            keys = slice(c * tk, (c + 1) * tk)
            if score_tile is not None:
                s = lax.dot_general(k_ref[0, keys, :], q, contract_last, preferred_element_type=F32)
                sw_ref[keys, :] = s
                part = jnp.max(s, axis=0, keepdims=True)
                m_new = part if m_new is None else jnp.maximum(m_new, part)
            if out_tile is not None:
                p = jnp.exp2(sr_ref[keys, :] - m)
                lsum = lsum + jnp.sum(p, axis=0, keepdims=True)
                acc = acc + lax.dot_general(v_ref[0, keys, :], p.astype(BF16), contract_first,
                                            preferred_element_type=F32)
        if score_tile is not None:
            mw_ref[...] = m_new
        if out_tile is not None:
            o = (acc / lsum).T
            o_ref[:, q_rows(out_tile), :] = o.reshape(g, tq, dv).astype(o_ref.dtype)

    buf0, buf1 = (s0_ref, m0_ref), (s1_ref, m1_ref)
    passes(0, buf0, None, None)

    def pair(n, carry):
        tile = 2 * n
        passes(tile + 1, buf1, tile, buf0)

        @pl.when(n < tiles // 2 - 1)
        def _():
            passes(tile + 2, buf0, tile + 1, buf1)

        @pl.when(n == tiles // 2 - 1)
        def _():
            passes(None, None, tile + 1, buf1)

        return carry

    lax.fori_loop(0, tiles // 2, pair, 0)


def _attention(q, k, v, *, kv_heads, group, k_head0, v_head0, tq, tk):
    _, t, dq = q.shape
    dv = v.shape[-1]
    nb = t // SEQ
    rows = group * tq
    return pl.pallas_call(
        functools.partial(_attn_kernel, tq=tq, tk=tk),
        grid=(nb, kv_heads),
        in_specs=[pl.BlockSpec((group, SEQ, dq), lambda b, h: (h, b, 0)),
                  pl.BlockSpec((1, SEQ, dq), lambda b, h: (k_head0 + h, b, 0)),
                  pl.BlockSpec((1, SEQ, dv), lambda b, h: (v_head0 + h, b, 0))],
        out_specs=pl.BlockSpec((group, SEQ, dv), lambda b, h: (h, b, 0)),
        out_shape=jax.ShapeDtypeStruct((kv_heads * group, t, dv), BF16),
        scratch_shapes=[pltpu.VMEM((SEQ, rows), F32)] * 2 + [pltpu.VMEM((1, rows), F32)] * 2,
        compiler_params=_params("parallel", "parallel"),
        name="attention",
    )(q, k, v)


def _attn_b_kernel(q0, k0, v0, q1, k1, v1, q2, k2, v2, coef_ref, tab_ref, o_ref,
                   stage_ref, qd_ref, kd_ref, vd_ref, og_ref, eg_ref):
    scale = HEAD_DIM ** -0.5
    inputs = ((q0, k0, v0), (q1, k1, v1), (q2, k2, v2))
    chunk = 64

    def gather_classes(src_ref, dst_ref, dil):
        length = SEQ // dil
        stage_ref[...] = src_ref[0].astype(F32)

        def body(c, carry):
            for r in range(dil):
                src = pl.ds(pl.multiple_of(c * chunk * dil, chunk * dil) + r, chunk, stride=dil)
                dst = pl.ds(pl.multiple_of(r * length + c * chunk, chunk), chunk)
                dst_ref[dst, :] = stage_ref[src, :].astype(BF16)
            return carry

        lax.fori_loop(0, length // chunk, body, 0)

    ones = jnp.ones((B_WIDTH, HEAD_DIM), BF16)

    def tile(g, q, k, v, steps, coef, out_rows):
        s = lax.dot_general(q, k, (((1,), (1,)), ((), ())), preferred_element_type=F32)
        s = s * scale - coef * steps
        m = jnp.max(s, axis=-1, keepdims=True)
        p = jnp.exp(s - m).astype(BF16)
        ol = jnp.dot(p, jnp.concatenate([v, ones], axis=-1), preferred_element_type=F32)
        l = ol[:, HEAD_DIM:]
        og_ref[g, out_rows, :] = ol[:, :HEAD_DIM] / l
        eg_ref[g, out_rows, :] = m + jnp.log(l)

    for g, (_, dil) in enumerate(B_CONFIGS):
        length = SEQ // dil
        nt = length // TQ_B
        coef = coef_ref[0, g:g + 1, 0:1]
        if dil == 1:
            qs, ks, vs = (lambda rows, ref=ref: ref[0, rows, :] for ref in inputs[g])
        else:
            for src, dst in zip(inputs[g], (qd_ref, kd_ref, vd_ref)):
                gather_classes(src, dst, dil)
            qs, ks, vs = (lambda rows, ref=ref: ref[rows, :] for ref in (qd_ref, kd_ref, vd_ref))

        def residue(r, g=g, dil=dil, length=length, nt=nt, coef=coef, qs=qs, ks=ks, vs=vs):
            base = r * length
            for ti in range(nt):
                l0 = ti * TQ_B
                k0_ = min(max(l0 - B_SPAN, 0), length - B_WIDTH)
                steps = tab_ref[(l0 - k0_) // B_SPAN]
                if dil == 1:
                    q_rows, k_rows, out_rows = pl.ds(l0, TQ_B), pl.ds(k0_, B_WIDTH), pl.ds(l0, TQ_B)
                else:
                    q_rows = pl.ds(pl.multiple_of(base + l0, TQ_B), TQ_B)
                    k_rows = pl.ds(pl.multiple_of(base + k0_, B_SPAN), B_WIDTH)
                    out_rows = pl.ds(l0 * dil + r, TQ_B, stride=dil)
                tile(g, qs(q_rows), ks(k_rows), vs(k_rows), steps, coef, out_rows)

        if dil == 1:
            residue(0)
        else:
            per_step = max(1, B_TILES_PER_BLOCK // nt)

            def step(it, carry, residue=residue, per_step=per_step):
                for rr in range(per_step):
                    residue(it * per_step + rr)
                return carry

            lax.fori_loop(0, dil // per_step, step, 0)

    def merge(c, carry):
        rows = pl.ds(pl.multiple_of(c * B_MERGE_ROWS, B_MERGE_ROWS), B_MERGE_ROWS)
        lse = [eg_ref[g, rows, :] for g in range(len(B_CONFIGS))]
        mx = jnp.maximum(jnp.maximum(lse[0], lse[1]), lse[2])
        w = [jnp.exp(e - mx) for e in lse]
        den = w[0] + w[1] + w[2]
        o_ref[0, rows, :] = sum((w[g] / den) * og_ref[g, rows, :] for g in range(len(B_CONFIGS))).astype(o_ref.dtype)
        return carry

    lax.fori_loop(0, SEQ // B_MERGE_ROWS, merge, 0)


def _attn_b(qkv_heads, coef, tab):
    t = qkv_heads.shape[1]
    nb = t // SEQ
    base = A_IN // HEAD_DIM
    in_specs = []
    for g in range(len(B_CONFIGS)):
        for kind in range(3):
            head0 = base + kind * B_HEADS + g * B_SLOTS
            in_specs.append(pl.BlockSpec((1, SEQ, HEAD_DIM), lambda b, j, head0=head0: (head0 + j, b, 0)))
    in_specs.append(pl.BlockSpec((1, 8, V7X_LANES), lambda b, j: (j, 0, 0)))
    in_specs.append(pl.BlockSpec(tab.shape, lambda b, j: (0, 0, 0)))
    return pl.pallas_call(
        _attn_b_kernel,
        grid=(nb, B_SLOTS),
        in_specs=in_specs,
        out_specs=pl.BlockSpec((1, SEQ, HEAD_DIM), lambda b, j: (j, b, 0)),
        out_shape=jax.ShapeDtypeStruct((B_SLOTS, t, HEAD_DIM), BF16),
        scratch_shapes=[pltpu.VMEM((SEQ, HEAD_DIM), F32)]
                       + [pltpu.VMEM((SEQ, HEAD_DIM), BF16)] * 3
                       + [pltpu.VMEM((len(B_CONFIGS), SEQ, HEAD_DIM), F32)] * 2,
        compiler_params=_params("parallel", "parallel"),
        name="attn_b",
    )(*([qkv_heads] * 9), coef, tab)


def _mix_out_kernel(x_ref, oa_ref, ob_ref, oc_ref, gout_ref, w_ref, gn_ref, xo_ref, ho_ref, y_ref, d_ref,
                    *, n_tiles):
    i = pl.program_id(0)

    def group_norm(tiles, col0):
        width = len(tiles) * HEAD_DIM
        ssq = sum(jnp.sum(tl * tl, axis=-1, keepdims=True) for tl in tiles)
        r = lax.rsqrt(ssq * (1.0 / width) + EPS)
        for n, tl in enumerate(tiles):
            c = col0 + n * HEAD_DIM
            y_ref[:, c:c + HEAD_DIM] = ((tl * r) * gout_ref[:, c:c + HEAD_DIM]).astype(BF16)

    def finish():
        xn = x_ref[...] + d_ref[...]
        xo_ref[...] = xn
        ho_ref[...] = _rms(xn, gn_ref[...], D_MODEL).astype(BF16)

    def project():
        group_norm([oa_ref[n].astype(F32) for n in range(A_HEADS)], 0)
        group_norm([ob_ref[n].astype(F32) for n in range(B_SLOTS)], A_OUT)
        group_norm([oc_ref[n].astype(F32) for n in range(C_HEADS)], A_OUT + B_OUT)
        d_ref[...] = jnp.dot(y_ref[...], w_ref[...], preferred_element_type=F32)

    pl.when(i == 0)(project)

    @pl.when((i > 0) & (i < n_tiles))
    def _():
        finish()
        project()

    pl.when(i == n_tiles)(finish)


def _mix_out(x, oa, ob, oc, gain_out, w_out, gain_next):
    t = x.shape[0]
    n_tiles = t // TM_OUT
    cur = lambda i: jnp.minimum(i, n_tiles - 1)
    heads = lambda n: pl.BlockSpec((n, TM_OUT, HEAD_DIM), lambda i: (0, cur(i), 0))
    row = pl.BlockSpec((1, D_MODEL), lambda i: (0, 0))
    tok = pl.BlockSpec((TM_OUT, D_MODEL), lambda i: (jnp.maximum(i - 1, 0), 0))
    return pl.pallas_call(
        functools.partial(_mix_out_kernel, n_tiles=n_tiles),
        grid=(n_tiles + 1,),
        in_specs=[tok, heads(A_HEADS), heads(B_SLOTS), heads(C_HEADS), row,
                  pl.BlockSpec((D_MODEL, D_MODEL), lambda i: (0, 0), pipeline_mode=pl.Buffered(1)), row],
        out_specs=[tok, tok],
        out_shape=[jax.ShapeDtypeStruct((t, D_MODEL), F32),
                   jax.ShapeDtypeStruct((t, D_MODEL), BF16)],
        scratch_shapes=[pltpu.VMEM((TM_OUT, D_MODEL), BF16), pltpu.VMEM((TM_OUT, D_MODEL), F32)],
        compiler_params=_params("arbitrary"),
        name="mix_out",
    )(x, oa, ob, oc, gain_out, w_out, gain_next)


def _rope_tables(pos_a, pos_b):
    inv = ROPE_THETA ** (-jnp.arange(ROPE_HALF, dtype=F32) / ROPE_HALF)

    def one(pos):
        ang = pos.astype(F32)[:, None] * inv[None, :]
        return jnp.cos(ang), jnp.sin(ang)

    ca, sa = one(pos_a)
    cb, sb = one(pos_b)
    return jnp.concatenate([ca, cb, ca, cb], axis=-1), jnp.concatenate([-sa, -sb, sa, sb], axis=-1)


def _band_steps_table():
    i = jnp.arange(TQ_B)[:, None]
    jj = jnp.arange(B_WIDTH)[None, :]
    tabs = []
    for off in (0, B_SPAN, 2 * B_SPAN):
        rel = jnp.abs(off + i - jj)
        tabs.append(jnp.where(rel <= B_SPAN, rel.astype(F32), -NEG_BIG))
    return jnp.stack(tabs)


def _head_gains(gq_a, gk_a, gq_b, gk_b):
    ones = jnp.ones((HEAD_DIM,), F32)
    rows = ([gq_a] * A_HEADS + [gk_a] * A_KV_HEADS + [ones] * A_KV_HEADS
            + [gq_b] * B_HEADS + [gk_b] * B_HEADS + [ones] * B_HEADS)
    return jnp.stack(rows).astype(F32).reshape(AB_HEADS, 1, HEAD_DIM)


def kernel(x, ffn1_norm, ffn1_w_gu, ffn1_w_down, mix_norm, w_in, a_q_norm, a_k_norm, b_q_norm, b_k_norm, c_q_a_norm, c_q_up, c_kv_a_norm, c_kv_up, c_q_norm, c_k_norm, out_norm, w_out, ffn2_norm, ffn2_w_gu, ffn2_w_down):
    nb, s, d = x.shape
    assert (s, d) == (SEQ, D_MODEL)
    depth = w_in.shape[0]
    t = nb * s
    x = x.reshape(t, d)
    row = lambda v: v.reshape(1, -1).astype(F32)

    pos = jnp.arange(s, dtype=jnp.int32)
    cos_a, sin_a = _rope_tables(pos // GRID_W, pos % GRID_W)
    cos_c, sin_c = _rope_tables(pos, pos)
    lane_order = jnp.array(ROPE_LANE_ORDER)

    def rope_tile(v):
        zeros = jnp.zeros(v.shape[:-1] + (V7X_LANES // 2 - ROPE_HALF,), v.dtype)
        return jnp.concatenate([v[..., :ROPE_HALF], zeros, v[..., ROPE_HALF:], zeros], axis=-1)
    tab = _band_steps_table()
    slopes = 2.0 ** (-8.0 * jnp.arange(1, B_HEADS + 1, dtype=F32) / B_HEADS)
    dil = jnp.array([c[1] for c in B_CONFIGS], F32)
    coef = (slopes.reshape(len(B_CONFIGS), B_SLOTS) * dil[:, None]).T
    coef = jnp.zeros((B_SLOTS, 8, V7X_LANES), F32).at[:, :len(B_CONFIGS), :].set(coef[:, :, None])

    h = _norm(x, row(ffn1_norm[0]))
    for l in range(depth):
        x, h = _ffn(x, h, ffn1_w_gu[l].astype(BF16), ffn1_w_down[l].astype(BF16), row(mix_norm[l]))

        w = w_in[l]
        rot = (A_HEADS + A_KV_HEADS) * HEAD_DIM
        w_rot = w[:, :rot].reshape(D_MODEL, -1, HEAD_DIM)[:, :, lane_order].reshape(D_MODEL, rot)
        w_ab = jnp.concatenate([w_rot, w[:, rot:AB_IN]], axis=-1).astype(BF16)
        qkv = _proj_heads(h, w_ab, _head_gains(a_q_norm[l][lane_order], a_k_norm[l][lane_order],
                                               b_q_norm[l], b_k_norm[l]), cos_a, sin_a)
        w_c = jnp.concatenate([w[:, AB_IN:W_IN - C_ROPE], rope_tile(w[:, W_IN - C_ROPE:])], axis=-1).astype(BF16)
        q_up = c_q_up[l].reshape(C_Q_RANK, C_HEADS, C_QK)
        q_up = jnp.concatenate([q_up[:, :, :C_NOPE].reshape(C_Q_RANK, C_HEADS * C_NOPE),
                                rope_tile(q_up[:, :, C_NOPE:]).reshape(C_Q_RANK, C_HEADS * V7X_LANES)],
                               axis=-1).astype(BF16)
        split = lambda gvec: jnp.stack([gvec[:C_NOPE], rope_tile(gvec[C_NOPE:])]).astype(F32)
        qc, kc, vc = _proj_c(h, w_c, row(c_q_a_norm[l]), row(c_kv_a_norm[l]), q_up, c_kv_up[l].astype(BF16),
                             split(c_q_norm[l]), split(c_k_norm[l]), cos_c, sin_c)

        oa = _attention(qkv, qkv, qkv, kv_heads=A_KV_HEADS, group=A_HEADS // A_KV_HEADS,
                        k_head0=A_HEADS, v_head0=A_HEADS + A_KV_HEADS, tq=TQ_A, tk=TK_A)
        oc = _attention(qc, kc, vc, kv_heads=C_HEADS, group=1, k_head0=0, v_head0=0, tq=TQ_C, tk=TK_C)
        ob = _attn_b(qkv, coef, tab)

        x, h = _mix_out(x, oa, ob, oc, row(out_norm[l]), w_out[l].astype(BF16), row(ffn2_norm[l]))
        w_gu, w_down = ffn2_w_gu[l].astype(BF16), ffn2_w_down[l].astype(BF16)
        if l + 1 < depth:
            x, h = _ffn(x, h, w_gu, w_down, row(ffn1_norm[l + 1]))
        else:
            x, = _ffn(x, h, w_gu, w_down)
    return x.reshape(nb, s, d)
```

```python
import functools

import jax
import jax.numpy as jnp
from jax import lax
from jax.experimental import pallas as pl
from jax.experimental.pallas import tpu as pltpu

F32 = jnp.float32
BF16 = jnp.bfloat16

D_MODEL = 2048
SEQ = 4096
GRID_W = 64
HEAD_DIM = 128
ROPE_THETA = 10000.0
EPS = 1e-6
NEG_BIG = -1e30
LOG2_E = 1.4426950408889634

A_HEADS = 8
A_KV_HEADS = 2
B_CONFIGS = ((128, 1), (512, 4), (2048, 16))
B_SLOTS = 4
B_HEADS = B_SLOTS * len(B_CONFIGS)
C_HEADS = 4
C_Q_RANK = 512
C_KV_RANK = 256
C_NOPE = 128
C_ROPE = 64
C_QK = C_NOPE + C_ROPE
D_FF = 5632

A_Q = A_HEADS * HEAD_DIM
A_KV = A_KV_HEADS * HEAD_DIM
A_IN = A_Q + 2 * A_KV
B_QKV = B_HEADS * HEAD_DIM
B_IN = 3 * B_QKV
AB_IN = A_IN + B_IN
W_IN = AB_IN + C_Q_RANK + C_KV_RANK + C_ROPE
AB_HEADS = AB_IN // HEAD_DIM
A_OUT = A_HEADS * HEAD_DIM
B_OUT = B_SLOTS * HEAD_DIM
C_OUT = C_HEADS * HEAD_DIM

V7X_LANES = 128
V7X_VMEM_LIMIT = 56 * 1024 * 1024

TM_NORM = 512
TM_UP = 2048
TF_FFN = 512
TM_DOWN = 256
TM_PROJ = 2048
TN_PROJ = 512
TM_OUT = 512
TQ_B = 128
B_SPAN = 64
B_WIDTH = TQ_B + 2 * B_SPAN
B_MERGE_ROWS = 256
B_TILES_PER_BLOCK = 8
TQ_A, TK_A = 128, 256
TQ_C, TK_C = 512, 512


def _params(*sem):
    return pltpu.CompilerParams(dimension_semantics=sem, vmem_limit_bytes=V7X_VMEM_LIMIT)


def _rms(x, gain, width):
    ms = jnp.sum(x * x, axis=-1, keepdims=True) * (1.0 / width)
    return (x * lax.rsqrt(ms + EPS)) * gain


ROPE_HALF = 32
ROPE_LANE_ORDER = tuple(list(range(0, 32)) + list(range(64, 96)) + list(range(32, 64)) + list(range(96, 128)))


def _rope(y, cos, sin_signed):
    return y * cos + pltpu.roll(y, V7X_LANES // 2, 1) * sin_signed


def _norm_kernel(x_ref, g_ref, h_ref):
    h_ref[...] = _rms(x_ref[...], g_ref[...], D_MODEL).astype(BF16)


def _norm(x, gain):
    t = x.shape[0]
    return pl.pallas_call(
        _norm_kernel,
        grid=(t // TM_NORM,),
        in_specs=[pl.BlockSpec((TM_NORM, D_MODEL), lambda i: (i, 0)),
                  pl.BlockSpec((1, D_MODEL), lambda i: (0, 0))],
        out_specs=pl.BlockSpec((TM_NORM, D_MODEL), lambda i: (i, 0)),
        out_shape=jax.ShapeDtypeStruct((t, D_MODEL), BF16),
        compiler_params=_params("parallel"),
        name="norm",
    )(x, gain)


def _ffn_up_kernel(h_ref, wg_ref, wu_ref, a_ref, g_ref, u_ref, *, nf):
    j = pl.program_id(1)

    def swiglu():
        g = g_ref[...]
        a_ref[...] = (g * (1.0 / (1.0 + jnp.exp(-g))) * u_ref[...]).astype(BF16)

    def matmuls():
        h = h_ref[...]
        g_ref[...] = jnp.dot(h, wg_ref[0].astype(BF16), preferred_element_type=F32)
        u_ref[...] = jnp.dot(h, wu_ref[0].astype(BF16), preferred_element_type=F32)

    pl.when(j == 0)(matmuls)

    @pl.when((j > 0) & (j < nf))
    def _():
        swiglu()
        matmuls()

    pl.when(j == nf)(swiglu)


def _ffn_down_kernel(a_ref, wd_ref, x_ref, *rest, emit_next, n_tiles):
    if emit_next:
        gn_ref, xo_ref, ho_ref, d_ref = rest
    else:
        xo_ref, d_ref = rest
    i = pl.program_id(0)

    def finish():
        xn = x_ref[...] + 0.5 * d_ref[...]
        xo_ref[...] = xn
        if emit_next:
            ho_ref[...] = _rms(xn, gn_ref[...], D_MODEL).astype(BF16)

    def matmul():
        d_ref[...] = jnp.dot(a_ref[...], wd_ref[...], preferred_element_type=F32)

    pl.when(i == 0)(matmul)

    @pl.when((i > 0) & (i < n_tiles))
    def _():
        finish()
        matmul()

    pl.when(i == n_tiles)(finish)


def _ffn(x, h, w_gu, layer, w_down, gain_next=None):
    t = x.shape[0]
    nf = D_FF // TF_FFN
    cur = lambda j: jnp.minimum(j, nf - 1)
    act = pl.pallas_call(
        functools.partial(_ffn_up_kernel, nf=nf),
        grid=(t // TM_UP, nf + 1),
        in_specs=[pl.BlockSpec((TM_UP, D_MODEL), lambda i, j: (i, 0), pipeline_mode=pl.Buffered(1)),
                  pl.BlockSpec((1, D_MODEL, TF_FFN), lambda i, j: (layer, 0, cur(j))),
                  pl.BlockSpec((1, D_MODEL, TF_FFN), lambda i, j: (layer, 0, cur(j) + nf))],
        out_specs=pl.BlockSpec((TM_UP, TF_FFN), lambda i, j: (i, jnp.maximum(j - 1, 0))),
        out_shape=jax.ShapeDtypeStruct((t, D_FF), BF16),
        scratch_shapes=[pltpu.VMEM((TM_UP, TF_FFN), F32)] * 2,
        compiler_params=_params("parallel", "arbitrary"),
        name="ffn_up",
    )(h, w_gu, w_gu)

    emit_next = gain_next is not None
    n_tiles = t // TM_DOWN
    tok = pl.BlockSpec((TM_DOWN, D_MODEL), lambda i: (jnp.maximum(i - 1, 0), 0))
    in_specs = [pl.BlockSpec((TM_DOWN, D_FF), lambda i: (jnp.minimum(i, n_tiles - 1), 0)),
                pl.BlockSpec((D_FF, D_MODEL), lambda i: (0, 0), pipeline_mode=pl.Buffered(1)),
                tok]
    args = [act, w_down, x]
    out_specs = [tok]
    out_shape = [jax.ShapeDtypeStruct((t, D_MODEL), F32)]
    if emit_next:
        in_specs.append(pl.BlockSpec((1, D_MODEL), lambda i: (0, 0)))
        args.append(gain_next)
        out_specs.append(tok)
        out_shape.append(jax.ShapeDtypeStruct((t, D_MODEL), BF16))
    return pl.pallas_call(
        functools.partial(_ffn_down_kernel, emit_next=emit_next, n_tiles=n_tiles),
        grid=(n_tiles + 1,),
        in_specs=in_specs,
        out_specs=out_specs,
        out_shape=out_shape,
        scratch_shapes=[pltpu.VMEM((TM_DOWN, D_MODEL), F32)],
        compiler_params=_params("arbitrary"),
        name="ffn_down",
    )(*args)


HEAD_PLAIN, HEAD_NORM, HEAD_NORM_ROPE = 0, 1, 2
A_Q_SCALE = HEAD_DIM ** -0.5 * LOG2_E
C_Q_SCALE = C_QK ** -0.5 * LOG2_E
AB_HEAD_KINDS = (((HEAD_NORM_ROPE, A_Q_SCALE),) * A_HEADS + ((HEAD_NORM_ROPE, 1.0),) * A_KV_HEADS
                 + ((HEAD_PLAIN, 1.0),) * A_KV_HEADS
                 + ((HEAD_NORM, 1.0),) * (2 * B_HEADS) + ((HEAD_PLAIN, 1.0),) * B_HEADS)


def _proj_heads_kernel(h_ref, wrot_ref, w_ref, g_ref, cos_ref, sin_ref, o_ref, z_ref, *, step_kinds,
                       rot_blocks):
    j = pl.program_id(1)
    n_blocks = len(step_kinds)

    def step(kinds, dot_ref):
        if kinds is not None:
            for hh, (kind, out_scale) in enumerate(kinds):
                y = z_ref[:, hh * HEAD_DIM:(hh + 1) * HEAD_DIM]
                if kind != HEAD_PLAIN:
                    y = _rms(y, g_ref[hh], HEAD_DIM)
                if kind == HEAD_NORM_ROPE:
                    y = _rope(y, cos_ref[...], sin_ref[...])
                if out_scale != 1.0:
                    y = y * out_scale
                o_ref[hh] = y.astype(BF16)
        if dot_ref is not None:
            z_ref[...] = jnp.dot(h_ref[...], dot_ref[0].astype(BF16), preferred_element_type=F32)

    prev_kinds = (None,) + step_kinds
    cur_ref = tuple(wrot_ref if b < rot_blocks else w_ref for b in range(n_blocks)) + (None,)
    lo = 0
    while lo <= n_blocks:
        hi = lo
        while hi <= n_blocks and prev_kinds[hi] == prev_kinds[lo] and cur_ref[hi] is cur_ref[lo]:
            hi += 1
        pl.when((j >= lo) & (j < hi))(functools.partial(step, prev_kinds[lo], cur_ref[lo]))
        lo = hi


def _proj_heads(h, w_rot, w_in, layer, gains, cos, sin):
    t = h.shape[0]
    heads = TN_PROJ // HEAD_DIM
    n_blocks = AB_IN // TN_PROJ
    rot_blocks = w_rot.shape[1] // TN_PROJ
    seq_tiles = SEQ // TM_PROJ
    step_kinds = tuple(AB_HEAD_KINDS[n:n + heads] for n in range(0, AB_HEADS, heads))
    prev = lambda j: jnp.maximum(j - 1, 0)
    return pl.pallas_call(
        functools.partial(_proj_heads_kernel, step_kinds=step_kinds, rot_blocks=rot_blocks),
        grid=(t // TM_PROJ, n_blocks + 1),
        in_specs=[pl.BlockSpec((TM_PROJ, D_MODEL), lambda i, j: (i, 0)),
                  pl.BlockSpec((1, D_MODEL, TN_PROJ), lambda i, j: (0, 0, jnp.minimum(j, rot_blocks - 1))),
                  pl.BlockSpec((1, D_MODEL, TN_PROJ),
                               lambda i, j: (layer, 0, jnp.clip(j, rot_blocks, n_blocks - 1))),
                  pl.BlockSpec((heads, 1, HEAD_DIM), lambda i, j: (prev(j), 0, 0)),
                  pl.BlockSpec((TM_PROJ, HEAD_DIM), lambda i, j: (i % seq_tiles, 0)),
                  pl.BlockSpec((TM_PROJ, HEAD_DIM), lambda i, j: (i % seq_tiles, 0))],
        out_specs=pl.BlockSpec((heads, TM_PROJ, HEAD_DIM), lambda i, j: (prev(j), i, 0)),
        out_shape=jax.ShapeDtypeStruct((AB_HEADS, t, HEAD_DIM), BF16),
        scratch_shapes=[pltpu.VMEM((TM_PROJ, TN_PROJ), F32)],
        compiler_params=_params("parallel", "arbitrary"),
        name="proj_heads",
    )(h, w_rot[None], w_in, gains, cos, sin)


def _proj_c_kernel(h_ref, wc_ref, gqa_ref, gkva_ref, qup_ref, kvup_ref, gq_ref, gk_ref,
                   cos_ref, sin_ref, q_ref, k_ref, v_ref):
    nope_w = C_HEADS * C_NOPE
    half = h_ref.shape[0] // 2
    for part in range(2):
        rows = slice(part * half, (part + 1) * half)
        h = h_ref[rows, :]
        cos = cos_ref[rows, :]
        sin = sin_ref[rows, :]

        zq = jnp.dot(h, wc_ref[:, :C_Q_RANK], preferred_element_type=F32)
        zkv = jnp.dot(h, wc_ref[:, C_Q_RANK:], preferred_element_type=F32)
        q_lat = _rms(zq, gqa_ref[...], C_Q_RANK).astype(BF16)
        cq = jnp.dot(q_lat, qup_ref[...], preferred_element_type=F32)
        kv_lat = _rms(zkv[:, :C_KV_RANK], gkva_ref[...], C_KV_RANK).astype(BF16)
        k_rope = zkv[:, C_KV_RANK:]
        ckv = jnp.dot(kv_lat, kvup_ref[...], preferred_element_type=F32)

        for hh in range(C_HEADS):
            qn = cq[:, hh * C_NOPE:(hh + 1) * C_NOPE]
            qr = cq[:, nope_w + hh * V7X_LANES:nope_w + (hh + 1) * V7X_LANES]
            ms = (jnp.sum(qn * qn, axis=-1, keepdims=True)
                  + jnp.sum(qr * qr, axis=-1, keepdims=True)) * (1.0 / C_QK)
            r = lax.rsqrt(ms + EPS)
            q_ref[hh, rows, 0:C_NOPE] = (((qn * r) * gq_ref[0:1, :]) * C_Q_SCALE).astype(BF16)
            q_ref[hh, rows, C_NOPE:] = (_rope((qr * r) * gq_ref[1:2, :], cos, sin) * C_Q_SCALE).astype(BF16)

        k_rope_sq = jnp.sum(k_rope * k_rope, axis=-1, keepdims=True)
        for hh in range(C_HEADS):
            kn = ckv[:, hh * 2 * C_NOPE:hh * 2 * C_NOPE + C_NOPE]
            ms = (jnp.sum(kn * kn, axis=-1, keepdims=True) + k_rope_sq) * (1.0 / C_QK)
            r = lax.rsqrt(ms + EPS)
            k_ref[hh, rows, 0:C_NOPE] = ((kn * r) * gk_ref[0:1, :]).astype(BF16)
            k_ref[hh, rows, C_NOPE:] = _rope((k_rope * r) * gk_ref[1:2, :], cos, sin).astype(BF16)
            v_ref[hh, rows, :] = ckv[:, hh * 2 * C_NOPE + C_NOPE:(hh + 1) * 2 * C_NOPE].astype(BF16)


def _proj_c(h, w_c, gqa, gkva, q_up, kv_up, gq, gk, cos, sin):
    t = h.shape[0]
    seq_tiles = SEQ // TM_PROJ
    full = lambda a: pl.BlockSpec(a.shape, lambda i: (0,) * a.ndim)
    qk_w = 2 * V7X_LANES
    return pl.pallas_call(
        _proj_c_kernel,
        grid=(t // TM_PROJ,),
        in_specs=[pl.BlockSpec((TM_PROJ, D_MODEL), lambda i: (i, 0)),
                  full(w_c), full(gqa), full(gkva), full(q_up), full(kv_up), full(gq), full(gk),
                  pl.BlockSpec((TM_PROJ, V7X_LANES), lambda i: (i % seq_tiles, 0)),
                  pl.BlockSpec((TM_PROJ, V7X_LANES), lambda i: (i % seq_tiles, 0))],
        out_specs=[pl.BlockSpec((C_HEADS, TM_PROJ, qk_w), lambda i: (0, i, 0)),
                   pl.BlockSpec((C_HEADS, TM_PROJ, qk_w), lambda i: (0, i, 0)),
                   pl.BlockSpec((C_HEADS, TM_PROJ, C_NOPE), lambda i: (0, i, 0))],
        out_shape=[jax.ShapeDtypeStruct((C_HEADS, t, qk_w), BF16),
                   jax.ShapeDtypeStruct((C_HEADS, t, qk_w), BF16),
                   jax.ShapeDtypeStruct((C_HEADS, t, C_NOPE), BF16)],
        compiler_params=_params("parallel"),
        name="proj_c",
    )(h, w_c, gqa, gkva, q_up, kv_up, gq, gk, cos, sin)


def _attn_kernel(q_ref, k_ref, v_ref, o_ref, s0_ref, s1_ref, m0_ref, m1_ref, *, tq, tk):
    g, _, dq = q_ref.shape
    rows = g * tq
    dv = v_ref.shape[-1]
    tiles = SEQ // tq
    contract_last = (((1,), (1,)), ((), ()))
    contract_first = (((0,), (0,)), ((), ()))

    def q_rows(tile):
        start = tile * tq
        return pl.ds(start if isinstance(tile, int) else pl.multiple_of(start, tq), tq)

    def passes(score_tile, score_bufs, out_tile, out_bufs):
        if score_tile is not None:
            sw_ref, mw_ref = score_bufs
            q = q_ref[:, q_rows(score_tile), :].reshape(rows, dq)
            m_new = None
        if out_tile is not None:
            sr_ref, mr_ref = out_bufs
            m = mr_ref[...]
            lsum = jnp.zeros((1, rows), F32)
            acc = jnp.zeros((dv, rows), F32)
        for c in range(SEQ // tk):
            keys = slice(c * tk, (c + 1) * tk)
            if score_tile is not None:
                s = lax.dot_general(k_ref[0, keys, :], q, contract_last, preferred_element_type=F32)
                sw_ref[keys, :] = s
                part = jnp.max(s, axis=0, keepdims=True)
                m_new = part if m_new is None else jnp.maximum(m_new, part)
            if out_tile is not None:
                p = jnp.exp2(sr_ref[keys, :] - m)
                lsum = lsum + jnp.sum(p, axis=0, keepdims=True)
                acc = acc + lax.dot_general(v_ref[0, keys, :], p.astype(BF16), contract_first,
                                            preferred_element_type=F32)
        if score_tile is not None:
            mw_ref[...] = m_new
        if out_tile is not None:
            o = (acc / lsum).T
            o_ref[:, q_rows(out_tile), :] = o.reshape(g, tq, dv).astype(o_ref.dtype)

    buf0, buf1 = (s0_ref, m0_ref), (s1_ref, m1_ref)
    passes(0, buf0, None, None)

    def pair(n, carry):
        tile = 2 * n
        passes(tile + 1, buf1, tile, buf0)

        @pl.when(n < tiles // 2 - 1)
        def _():
            passes(tile + 2, buf0, tile + 1, buf1)

        @pl.when(n == tiles // 2 - 1)
        def _():
            passes(None, None, tile + 1, buf1)

        return carry

    lax.fori_loop(0, tiles // 2, pair, 0)


def _attention(q, k, v, *, kv_heads, group, k_head0, v_head0, tq, tk):
    _, t, dq = q.shape
    dv = v.shape[-1]
    nb = t // SEQ
    rows = group * tq
    return pl.pallas_call(
        functools.partial(_attn_kernel, tq=tq, tk=tk),
        grid=(nb, kv_heads),
        in_specs=[pl.BlockSpec((group, SEQ, dq), lambda b, h: (h, b, 0)),
                  pl.BlockSpec((1, SEQ, dq), lambda b, h: (k_head0 + h, b, 0)),
                  pl.BlockSpec((1, SEQ, dv), lambda b, h: (v_head0 + h, b, 0))],
        out_specs=pl.BlockSpec((group, SEQ, dv), lambda b, h: (h, b, 0)),
        out_shape=jax.ShapeDtypeStruct((kv_heads * group, t, dv), BF16),
        scratch_shapes=[pltpu.VMEM((SEQ, rows), F32)] * 2 + [pltpu.VMEM((1, rows), F32)] * 2,
        compiler_params=_params("parallel", "parallel"),
        name="attention",
    )(q, k, v)


def _attn_b_kernel(q0, k0, v0, q1, k1, v1, q2, k2, v2, coef_ref, tab_ref, o_ref,
                   stage_ref, qd_ref, kd_ref, vd_ref, og_ref, eg_ref):
    scale = HEAD_DIM ** -0.5
    inputs = ((q0, k0, v0), (q1, k1, v1), (q2, k2, v2))
    chunk = 64

    def gather_classes(src_ref, dst_ref, dil):
        length = SEQ // dil
        stage_ref[...] = src_ref[0].astype(F32)

        def body(c, carry):
            for r in range(dil):
                src = pl.ds(pl.multiple_of(c * chunk * dil, chunk * dil) + r, chunk, stride=dil)
                dst = pl.ds(pl.multiple_of(r * length + c * chunk, chunk), chunk)
                dst_ref[dst, :] = stage_ref[src, :].astype(BF16)
            return carry

        lax.fori_loop(0, length // chunk, body, 0)

    ones = jnp.ones((B_WIDTH, HEAD_DIM), BF16)

    def tile(g, q, k, v, steps, coef, out_rows):
        s = lax.dot_general(q, k, (((1,), (1,)), ((), ())), preferred_element_type=F32)
        s = s * scale - coef * steps
        m = jnp.max(s, axis=-1, keepdims=True)
        p = jnp.exp(s - m).astype(BF16)
        ol = jnp.dot(p, jnp.concatenate([v, ones], axis=-1), preferred_element_type=F32)
        l = ol[:, HEAD_DIM:]
        og_ref[g, out_rows, :] = ol[:, :HEAD_DIM] / l
        eg_ref[g, out_rows, :] = m + jnp.log(l)

    for g, (_, dil) in enumerate(B_CONFIGS):
        length = SEQ // dil
        nt = length // TQ_B
        coef = coef_ref[0, g:g + 1, 0:1]
        if dil == 1:
            qs, ks, vs = (lambda rows, ref=ref: ref[0, rows, :] for ref in inputs[g])
        else:
            for src, dst in zip(inputs[g], (qd_ref, kd_ref, vd_ref)):
                gather_classes(src, dst, dil)
            qs, ks, vs = (lambda rows, ref=ref: ref[rows, :] for ref in (qd_ref, kd_ref, vd_ref))

        def residue(r, g=g, dil=dil, length=length, nt=nt, coef=coef, qs=qs, ks=ks, vs=vs):
            base = r * length
            for ti in range(nt):
                l0 = ti * TQ_B
                k0_ = min(max(l0 - B_SPAN, 0), length - B_WIDTH)
                steps = tab_ref[(l0 - k0_) // B_SPAN]
                if dil == 1:
                    q_rows, k_rows, out_rows = pl.ds(l0, TQ_B), pl.ds(k0_, B_WIDTH), pl.ds(l0, TQ_B)
                else:
                    q_rows = pl.ds(pl.multiple_of(base + l0, TQ_B), TQ_B)
                    k_rows = pl.ds(pl.multiple_of(base + k0_, B_SPAN), B_WIDTH)
                    out_rows = pl.ds(l0 * dil + r, TQ_B, stride=dil)
                tile(g, qs(q_rows), ks(k_rows), vs(k_rows), steps, coef, out_rows)

        if dil == 1:
            residue(0)
        else:
            per_step = max(1, B_TILES_PER_BLOCK // nt)

            def step(it, carry, residue=residue, per_step=per_step):
                for rr in range(per_step):
                    residue(it * per_step + rr)
                return carry

            lax.fori_loop(0, dil // per_step, step, 0)

    def merge(c, carry):
        rows = pl.ds(pl.multiple_of(c * B_MERGE_ROWS, B_MERGE_ROWS), B_MERGE_ROWS)
        lse = [eg_ref[g, rows, :] for g in range(len(B_CONFIGS))]
        mx = jnp.maximum(jnp.maximum(lse[0], lse[1]), lse[2])
        w = [jnp.exp(e - mx) for e in lse]
        den = w[0] + w[1] + w[2]
        o_ref[0, rows, :] = sum((w[g] / den) * og_ref[g, rows, :] for g in range(len(B_CONFIGS))).astype(o_ref.dtype)
        return carry

    lax.fori_loop(0, SEQ // B_MERGE_ROWS, merge, 0)


def _attn_b(qkv_heads, coef, tab):
    t = qkv_heads.shape[1]
    nb = t // SEQ
    base = A_IN // HEAD_DIM
    in_specs = []
    for g in range(len(B_CONFIGS)):
        for kind in range(3):
            head0 = base + kind * B_HEADS + g * B_SLOTS
            in_specs.append(pl.BlockSpec((1, SEQ, HEAD_DIM), lambda b, j, head0=head0: (head0 + j, b, 0)))
    in_specs.append(pl.BlockSpec((1, 8, V7X_LANES), lambda b, j: (j, 0, 0)))
    in_specs.append(pl.BlockSpec(tab.shape, lambda b, j: (0, 0, 0)))
    return pl.pallas_call(
        _attn_b_kernel,
        grid=(nb, B_SLOTS),
        in_specs=in_specs,
        out_specs=pl.BlockSpec((1, SEQ, HEAD_DIM), lambda b, j: (j, b, 0)),
        out_shape=jax.ShapeDtypeStruct((B_SLOTS, t, HEAD_DIM), BF16),
        scratch_shapes=[pltpu.VMEM((SEQ, HEAD_DIM), F32)]
                       + [pltpu.VMEM((SEQ, HEAD_DIM), BF16)] * 3
                       + [pltpu.VMEM((len(B_CONFIGS), SEQ, HEAD_DIM), F32)] * 2,
        compiler_params=_params("parallel", "parallel"),
        name="attn_b",
    )(*([qkv_heads] * 9), coef, tab)


def _mix_out_kernel(x_ref, oa_ref, ob_ref, oc_ref, gout_ref, w_ref, gn_ref, xo_ref, ho_ref, y_ref, d_ref,
                    *, n_tiles):
    i = pl.program_id(0)

    def group_norm(tiles, col0):
        width = len(tiles) * HEAD_DIM
        ssq = sum(jnp.sum(tl * tl, axis=-1, keepdims=True) for tl in tiles)
        r = lax.rsqrt(ssq * (1.0 / width) + EPS)
        for n, tl in enumerate(tiles):
            c = col0 + n * HEAD_DIM
            y_ref[:, c:c + HEAD_DIM] = ((tl * r) * gout_ref[:, c:c + HEAD_DIM]).astype(BF16)

    def finish():
        xn = x_ref[...] + d_ref[...]
        xo_ref[...] = xn
        ho_ref[...] = _rms(xn, gn_ref[...], D_MODEL).astype(BF16)

    def project():
        group_norm([oa_ref[n].astype(F32) for n in range(A_HEADS)], 0)
        group_norm([ob_ref[n].astype(F32) for n in range(B_SLOTS)], A_OUT)
        group_norm([oc_ref[n].astype(F32) for n in range(C_HEADS)], A_OUT + B_OUT)
        d_ref[...] = jnp.dot(y_ref[...], w_ref[...], preferred_element_type=F32)

    pl.when(i == 0)(project)

    @pl.when((i > 0) & (i < n_tiles))
    def _():
        finish()
        project()

    pl.when(i == n_tiles)(finish)


def _mix_out(x, oa, ob, oc, gain_out, w_out, gain_next):
    t = x.shape[0]
    n_tiles = t // TM_OUT
    cur = lambda i: jnp.minimum(i, n_tiles - 1)
    heads = lambda n: pl.BlockSpec((n, TM_OUT, HEAD_DIM), lambda i: (0, cur(i), 0))
    row = pl.BlockSpec((1, D_MODEL), lambda i: (0, 0))
    tok = pl.BlockSpec((TM_OUT, D_MODEL), lambda i: (jnp.maximum(i - 1, 0), 0))
    return pl.pallas_call(
        functools.partial(_mix_out_kernel, n_tiles=n_tiles),
        grid=(n_tiles + 1,),
        in_specs=[tok, heads(A_HEADS), heads(B_SLOTS), heads(C_HEADS), row,
                  pl.BlockSpec((D_MODEL, D_MODEL), lambda i: (0, 0), pipeline_mode=pl.Buffered(1)), row],
        out_specs=[tok, tok],
        out_shape=[jax.ShapeDtypeStruct((t, D_MODEL), F32),
                   jax.ShapeDtypeStruct((t, D_MODEL), BF16)],
        scratch_shapes=[pltpu.VMEM((TM_OUT, D_MODEL), BF16), pltpu.VMEM((TM_OUT, D_MODEL), F32)],
        compiler_params=_params("arbitrary"),
        name="mix_out",
    )(x, oa, ob, oc, gain_out, w_out, gain_next)


def _rope_tables(pos_a, pos_b):
    inv = ROPE_THETA ** (-jnp.arange(ROPE_HALF, dtype=F32) / ROPE_HALF)

    def one(pos):
        ang = pos.astype(F32)[:, None] * inv[None, :]
        return jnp.cos(ang), jnp.sin(ang)

    ca, sa = one(pos_a)
    cb, sb = one(pos_b)
    return jnp.concatenate([ca, cb, ca, cb], axis=-1), jnp.concatenate([-sa, -sb, sa, sb], axis=-1)


def _band_steps_table():
    i = jnp.arange(TQ_B)[:, None]
    jj = jnp.arange(B_WIDTH)[None, :]
    tabs = []
    for off in (0, B_SPAN, 2 * B_SPAN):
        rel = jnp.abs(off + i - jj)
        tabs.append(jnp.where(rel <= B_SPAN, rel.astype(F32), -NEG_BIG))
    return jnp.stack(tabs)


def _head_gains(gq_a, gk_a, gq_b, gk_b):
    ones = jnp.ones((HEAD_DIM,), F32)
    rows = ([gq_a] * A_HEADS + [gk_a] * A_KV_HEADS + [ones] * A_KV_HEADS
            + [gq_b] * B_HEADS + [gk_b] * B_HEADS + [ones] * B_HEADS)
    return jnp.stack(rows).astype(F32).reshape(AB_HEADS, 1, HEAD_DIM)


def kernel(x, ffn1_norm, ffn1_w_gu, ffn1_w_down, mix_norm, w_in, a_q_norm, a_k_norm, b_q_norm, b_k_norm, c_q_a_norm, c_q_up, c_kv_a_norm, c_kv_up, c_q_norm, c_k_norm, out_norm, w_out, ffn2_norm, ffn2_w_gu, ffn2_w_down):
    nb, s, d = x.shape
    assert (s, d) == (SEQ, D_MODEL)
    depth = w_in.shape[0]
    t = nb * s
    x = x.reshape(t, d)
    row = lambda v: v.reshape(1, -1).astype(F32)

    pos = jnp.arange(s, dtype=jnp.int32)
    cos_a, sin_a = _rope_tables(pos // GRID_W, pos % GRID_W)
    cos_c, sin_c = _rope_tables(pos, pos)
    lane_order = jnp.array(ROPE_LANE_ORDER)

    def rope_tile(v):
        zeros = jnp.zeros(v.shape[:-1] + (V7X_LANES // 2 - ROPE_HALF,), v.dtype)
        return jnp.concatenate([v[..., :ROPE_HALF], zeros, v[..., ROPE_HALF:], zeros], axis=-1)
    tab = _band_steps_table()
    slopes = 2.0 ** (-8.0 * jnp.arange(1, B_HEADS + 1, dtype=F32) / B_HEADS)
    dil = jnp.array([c[1] for c in B_CONFIGS], F32)
    coef = (slopes.reshape(len(B_CONFIGS), B_SLOTS) * dil[:, None]).T
    coef = jnp.zeros((B_SLOTS, 8, V7X_LANES), F32).at[:, :len(B_CONFIGS), :].set(coef[:, :, None])

    h = _norm(x, row(ffn1_norm[0]))
    for l in range(depth):
        x, h = _ffn(x, h, ffn1_w_gu, l, ffn1_w_down[l].astype(BF16), row(mix_norm[l]))

        w = w_in[l]
        rot = (A_HEADS + A_KV_HEADS) * HEAD_DIM
        w_rot = w[:, :rot].reshape(D_MODEL, -1, HEAD_DIM)[:, :, lane_order].reshape(D_MODEL, rot)
        rot_cols = -(-rot // TN_PROJ) * TN_PROJ
        w_rot = jnp.concatenate([w_rot, w[:, rot:rot_cols]], axis=-1)
        qkv = _proj_heads(h, w_rot, w_in, l, _head_gains(a_q_norm[l][lane_order], a_k_norm[l][lane_order],
                                                         b_q_norm[l], b_k_norm[l]), cos_a, sin_a)
        w_c = jnp.concatenate([w[:, AB_IN:W_IN - C_ROPE], rope_tile(w[:, W_IN - C_ROPE:])], axis=-1).astype(BF16)
        q_up = c_q_up[l].reshape(C_Q_RANK, C_HEADS, C_QK)
        q_up = jnp.concatenate([q_up[:, :, :C_NOPE].reshape(C_Q_RANK, C_HEADS * C_NOPE),
                                rope_tile(q_up[:, :, C_NOPE:]).reshape(C_Q_RANK, C_HEADS * V7X_LANES)],
                               axis=-1).astype(BF16)
        split = lambda gvec: jnp.stack([gvec[:C_NOPE], rope_tile(gvec[C_NOPE:])]).astype(F32)
        qc, kc, vc = _proj_c(h, w_c, row(c_q_a_norm[l]), row(c_kv_a_norm[l]), q_up, c_kv_up[l].astype(BF16),
                             split(c_q_norm[l]), split(c_k_norm[l]), cos_c, sin_c)

        oa = _attention(qkv, qkv, qkv, kv_heads=A_KV_HEADS, group=A_HEADS // A_KV_HEADS,
                        k_head0=A_HEADS, v_head0=A_HEADS + A_KV_HEADS, tq=TQ_A, tk=TK_A)
        oc = _attention(qc, kc, vc, kv_heads=C_HEADS, group=1, k_head0=0, v_head0=0, tq=TQ_C, tk=TK_C)
        ob = _attn_b(qkv, coef, tab)

        x, h = _mix_out(x, oa, ob, oc, row(out_norm[l]), w_out[l].astype(BF16), row(ffn2_norm[l]))
        w_down = ffn2_w_down[l].astype(BF16)
        if l + 1 < depth:
            x, h = _ffn(x, h, ffn2_w_gu, l, w_down, row(ffn1_norm[l + 1]))
        else:
            x, = _ffn(x, h, ffn2_w_gu, l, w_down)
    return x.reshape(nb, s, d)
```

```python
import functools

import jax
import jax.numpy as jnp
from jax import lax
from jax.experimental import pallas as pl
from jax.experimental.pallas import tpu as pltpu

F32 = jnp.float32
BF16 = jnp.bfloat16

D_MODEL = 2048
SEQ = 4096
GRID_W = 64
HEAD_DIM = 128
ROPE_THETA = 10000.0
EPS = 1e-6
NEG_BIG = -1e30
LOG2_E = 1.4426950408889634

A_HEADS = 8
A_KV_HEADS = 2
B_CONFIGS = ((128, 1), (512, 4), (2048, 16))
B_SLOTS = 4
B_HEADS = B_SLOTS * len(B_CONFIGS)
C_HEADS = 4
C_Q_RANK = 512
C_KV_RANK = 256
C_NOPE = 128
C_ROPE = 64
C_QK = C_NOPE + C_ROPE
D_FF = 5632

A_Q = A_HEADS * HEAD_DIM
A_KV = A_KV_HEADS * HEAD_DIM
A_IN = A_Q + 2 * A_KV
B_QKV = B_HEADS * HEAD_DIM
B_IN = 3 * B_QKV
AB_IN = A_IN + B_IN
W_IN = AB_IN + C_Q_RANK + C_KV_RANK + C_ROPE
AB_HEADS = AB_IN // HEAD_DIM
A_OUT = A_HEADS * HEAD_DIM
B_OUT = B_SLOTS * HEAD_DIM
C_OUT = C_HEADS * HEAD_DIM

V7X_LANES = 128
V7X_VMEM_LIMIT = 56 * 1024 * 1024

TM_NORM = 512
TM_UP = 2048
TF_FFN = 512
TM_DOWN = 256
TM_PROJ = 2048
TN_PROJ = 512
C_PARTS = 8
TM_OUT = 512
TQ_B = 128
B_SPAN = 64
B_WIDTH = TQ_B + 2 * B_SPAN
B_MERGE_ROWS = 256
B_TILES_PER_BLOCK = 8
TQ_A, TK_A = 128, 256
TQ_C, TK_C = 512, 512


def _params(*sem):
    return pltpu.CompilerParams(dimension_semantics=sem, vmem_limit_bytes=V7X_VMEM_LIMIT)


def _rms(x, gain, width):
    ms = jnp.sum(x * x, axis=-1, keepdims=True) * (1.0 / width)
    return (x * lax.rsqrt(ms + EPS)) * gain


ROPE_HALF = 32
ROPE_LANE_ORDER = tuple(list(range(0, 32)) + list(range(64, 96)) + list(range(32, 64)) + list(range(96, 128)))


def _rope(y, cos, sin_signed):
    return y * cos + pltpu.roll(y, V7X_LANES // 2, 1) * sin_signed


def _norm_kernel(x_ref, g_ref, h_ref):
    h_ref[...] = _rms(x_ref[...], g_ref[...], D_MODEL).astype(BF16)


def _norm(x, gain):
    t = x.shape[0]
    return pl.pallas_call(
        _norm_kernel,
        grid=(t // TM_NORM,),
        in_specs=[pl.BlockSpec((TM_NORM, D_MODEL), lambda i: (i, 0)),
                  pl.BlockSpec((1, D_MODEL), lambda i: (0, 0))],
        out_specs=pl.BlockSpec((TM_NORM, D_MODEL), lambda i: (i, 0)),
        out_shape=jax.ShapeDtypeStruct((t, D_MODEL), BF16),
        compiler_params=_params("parallel"),
        name="norm",
    )(x, gain)


def _ffn_up_kernel(h_ref, wg_ref, wu_ref, a_ref, g_ref, u_ref, *, nf):
    j = pl.program_id(1)

    def swiglu():
        g = g_ref[...]
        a_ref[...] = (g * (1.0 / (1.0 + jnp.exp(-g))) * u_ref[...]).astype(BF16)

    def matmuls():
        h = h_ref[...]
        g_ref[...] = jnp.dot(h, wg_ref[...], preferred_element_type=F32)
        u_ref[...] = jnp.dot(h, wu_ref[...], preferred_element_type=F32)

    pl.when(j == 0)(matmuls)

    @pl.when((j > 0) & (j < nf))
    def _():
        swiglu()
        matmuls()

    pl.when(j == nf)(swiglu)


def _ffn_down_kernel(a_ref, wd_ref, x_ref, *rest, emit_next, n_tiles):
    if emit_next:
        gn_ref, xo_ref, ho_ref, d_ref = rest
    else:
        xo_ref, d_ref = rest
    i = pl.program_id(0)

    def finish():
        xn = x_ref[...] + 0.5 * d_ref[...]
        xo_ref[...] = xn
        if emit_next:
            ho_ref[...] = _rms(xn, gn_ref[...], D_MODEL).astype(BF16)

    def matmul():
        d_ref[...] = jnp.dot(a_ref[...], wd_ref[...], preferred_element_type=F32)

    pl.when(i == 0)(matmul)

    @pl.when((i > 0) & (i < n_tiles))
    def _():
        finish()
        matmul()

    pl.when(i == n_tiles)(finish)


def _ffn(x, h, w_gu, w_down, gain_next=None):
    t = x.shape[0]
    nf = D_FF // TF_FFN
    cur = lambda j: jnp.minimum(j, nf - 1)
    act = pl.pallas_call(
        functools.partial(_ffn_up_kernel, nf=nf),
        grid=(t // TM_UP, nf + 1),
        in_specs=[pl.BlockSpec((TM_UP, D_MODEL), lambda i, j: (i, 0)),
                  pl.BlockSpec((D_MODEL, TF_FFN), lambda i, j: (0, cur(j))),
                  pl.BlockSpec((D_MODEL, TF_FFN), lambda i, j: (0, cur(j) + nf))],
        out_specs=pl.BlockSpec((TM_UP, TF_FFN), lambda i, j: (i, jnp.maximum(j - 1, 0))),
        out_shape=jax.ShapeDtypeStruct((t, D_FF), BF16),
        scratch_shapes=[pltpu.VMEM((TM_UP, TF_FFN), F32)] * 2,
        compiler_params=_params("parallel", "arbitrary"),
        name="ffn_up",
    )(h, w_gu, w_gu)

    emit_next = gain_next is not None
    n_tiles = t // TM_DOWN
    tok = pl.BlockSpec((TM_DOWN, D_MODEL), lambda i: (jnp.maximum(i - 1, 0), 0))
    in_specs = [pl.BlockSpec((TM_DOWN, D_FF), lambda i: (jnp.minimum(i, n_tiles - 1), 0)),
                pl.BlockSpec((D_FF, D_MODEL), lambda i: (0, 0), pipeline_mode=pl.Buffered(1)),
                tok]
    args = [act, w_down, x]
    out_specs = [tok]
    out_shape = [jax.ShapeDtypeStruct((t, D_MODEL), F32)]
    if emit_next:
        in_specs.append(pl.BlockSpec((1, D_MODEL), lambda i: (0, 0)))
        args.append(gain_next)
        out_specs.append(tok)
        out_shape.append(jax.ShapeDtypeStruct((t, D_MODEL), BF16))
    return pl.pallas_call(
        functools.partial(_ffn_down_kernel, emit_next=emit_next, n_tiles=n_tiles),
        grid=(n_tiles + 1,),
        in_specs=in_specs,
        out_specs=out_specs,
        out_shape=out_shape,
        scratch_shapes=[pltpu.VMEM((TM_DOWN, D_MODEL), F32)],
        compiler_params=_params("arbitrary"),
        name="ffn_down",
    )(*args)


HEAD_PLAIN, HEAD_NORM, HEAD_NORM_ROPE = 0, 1, 2
A_Q_SCALE = HEAD_DIM ** -0.5 * LOG2_E
C_Q_SCALE = C_QK ** -0.5 * LOG2_E
AB_HEAD_KINDS = (((HEAD_NORM_ROPE, A_Q_SCALE),) * A_HEADS + ((HEAD_NORM_ROPE, 1.0),) * A_KV_HEADS
                 + ((HEAD_PLAIN, 1.0),) * A_KV_HEADS
                 + ((HEAD_NORM, 1.0),) * (2 * B_HEADS) + ((HEAD_PLAIN, 1.0),) * B_HEADS)


def _proj_heads_kernel(h_ref, w_ref, g_ref, cos_ref, sin_ref, o_ref, z_ref, *, step_kinds):
    j = pl.program_id(1)
    n_blocks = len(step_kinds)

    def step(kinds, with_dot):
        if kinds is not None:
            for hh, (kind, out_scale) in enumerate(kinds):
                y = z_ref[:, hh * HEAD_DIM:(hh + 1) * HEAD_DIM]
                if kind != HEAD_PLAIN:
                    y = _rms(y, g_ref[hh], HEAD_DIM)
                if kind == HEAD_NORM_ROPE:
                    y = _rope(y, cos_ref[...], sin_ref[...])
                if out_scale != 1.0:
                    y = y * out_scale
                o_ref[hh] = y.astype(BF16)
        if with_dot:
            z_ref[...] = jnp.dot(h_ref[...], w_ref[...], preferred_element_type=F32)

    pl.when(j == 0)(functools.partial(step, None, True))
    lo = 0
    while lo < n_blocks:
        hi = lo
        while hi < n_blocks and step_kinds[hi] == step_kinds[lo]:
            hi += 1
        last = min(hi, n_blocks - 1)
        if last > lo:
            pl.when((j > lo) & (j <= last))(functools.partial(step, step_kinds[lo], True))
        lo = hi
    pl.when(j == n_blocks)(functools.partial(step, step_kinds[-1], False))


def _proj_heads(h, w_ab, gains, cos, sin):
    t = h.shape[0]
    heads = TN_PROJ // HEAD_DIM
    n_blocks = AB_IN // TN_PROJ
    seq_tiles = SEQ // TM_PROJ
    step_kinds = tuple(AB_HEAD_KINDS[n:n + heads] for n in range(0, AB_HEADS, heads))
    prev = lambda j: jnp.maximum(j - 1, 0)
    return pl.pallas_call(
        functools.partial(_proj_heads_kernel, step_kinds=step_kinds),
        grid=(t // TM_PROJ, n_blocks + 1),
        in_specs=[pl.BlockSpec((TM_PROJ, D_MODEL), lambda i, j: (i, 0)),
                  pl.BlockSpec((D_MODEL, TN_PROJ), lambda i, j: (0, jnp.minimum(j, n_blocks - 1))),
                  pl.BlockSpec((heads, 1, HEAD_DIM), lambda i, j: (prev(j), 0, 0)),
                  pl.BlockSpec((TM_PROJ, HEAD_DIM), lambda i, j: (i % seq_tiles, 0)),
                  pl.BlockSpec((TM_PROJ, HEAD_DIM), lambda i, j: (i % seq_tiles, 0))],
        out_specs=pl.BlockSpec((heads, TM_PROJ, HEAD_DIM), lambda i, j: (prev(j), i, 0)),
        out_shape=jax.ShapeDtypeStruct((AB_HEADS, t, HEAD_DIM), BF16),
        scratch_shapes=[pltpu.VMEM((TM_PROJ, TN_PROJ), F32)],
        compiler_params=_params("parallel", "arbitrary"),
        name="proj_heads",
    )(h, w_ab, gains, cos, sin)


def _proj_c_kernel(h_ref, wc_ref, gqa_ref, gkva_ref, qup_ref, kvup_ref, gq_ref, gk_ref,
                   cos_ref, sin_ref, q_ref, k_ref, v_ref):
    nope_w = C_HEADS * C_NOPE
    part_rows = h_ref.shape[0] // C_PARTS
    for part in range(C_PARTS):
        rows = slice(part * part_rows, (part + 1) * part_rows)
        h = h_ref[rows, :]
        cos = cos_ref[rows, :]
        sin = sin_ref[rows, :]

        zq = jnp.dot(h, wc_ref[:, :C_Q_RANK], preferred_element_type=F32)
        zkv = jnp.dot(h, wc_ref[:, C_Q_RANK:], preferred_element_type=F32)
        q_lat = _rms(zq, gqa_ref[...], C_Q_RANK).astype(BF16)
        cq = jnp.dot(q_lat, qup_ref[...], preferred_element_type=F32)
        kv_lat = _rms(zkv[:, :C_KV_RANK], gkva_ref[...], C_KV_RANK).astype(BF16)
        k_rope = zkv[:, C_KV_RANK:]
        ckv = jnp.dot(kv_lat, kvup_ref[...], preferred_element_type=F32)

        for hh in range(C_HEADS):
            qn = cq[:, hh * C_NOPE:(hh + 1) * C_NOPE]
            qr = cq[:, nope_w + hh * V7X_LANES:nope_w + (hh + 1) * V7X_LANES]
            ms = (jnp.sum(qn * qn, axis=-1, keepdims=True)
                  + jnp.sum(qr * qr, axis=-1, keepdims=True)) * (1.0 / C_QK)
            r = lax.rsqrt(ms + EPS)
            q_ref[hh, rows, 0:C_NOPE] = (((qn * r) * gq_ref[0:1, :]) * C_Q_SCALE).astype(BF16)
            q_ref[hh, rows, C_NOPE:] = (_rope((qr * r) * gq_ref[1:2, :], cos, sin) * C_Q_SCALE).astype(BF16)

        k_rope_sq = jnp.sum(k_rope * k_rope, axis=-1, keepdims=True)
        for hh in range(C_HEADS):
            kn = ckv[:, hh * 2 * C_NOPE:hh * 2 * C_NOPE + C_NOPE]
            ms = (jnp.sum(kn * kn, axis=-1, keepdims=True) + k_rope_sq) * (1.0 / C_QK)
            r = lax.rsqrt(ms + EPS)
            k_ref[hh, rows, 0:C_NOPE] = ((kn * r) * gk_ref[0:1, :]).astype(BF16)
            k_ref[hh, rows, C_NOPE:] = _rope((k_rope * r) * gk_ref[1:2, :], cos, sin).astype(BF16)
            v_ref[hh, rows, :] = ckv[:, hh * 2 * C_NOPE + C_NOPE:(hh + 1) * 2 * C_NOPE].astype(BF16)


def _proj_c(h, w_c, gqa, gkva, q_up, kv_up, gq, gk, cos, sin):
    t = h.shape[0]
    seq_tiles = SEQ // TM_PROJ
    full = lambda a: pl.BlockSpec(a.shape, lambda i: (0,) * a.ndim)
    qk_w = 2 * V7X_LANES
    return pl.pallas_call(
        _proj_c_kernel,
        grid=(t // TM_PROJ,),
        in_specs=[pl.BlockSpec((TM_PROJ, D_MODEL), lambda i: (i, 0)),
                  full(w_c), full(gqa), full(gkva), full(q_up), full(kv_up), full(gq), full(gk),
                  pl.BlockSpec((TM_PROJ, V7X_LANES), lambda i: (i % seq_tiles, 0)),
                  pl.BlockSpec((TM_PROJ, V7X_LANES), lambda i: (i % seq_tiles, 0))],
        out_specs=[pl.BlockSpec((C_HEADS, TM_PROJ, qk_w), lambda i: (0, i, 0)),
                   pl.BlockSpec((C_HEADS, TM_PROJ, qk_w), lambda i: (0, i, 0)),
                   pl.BlockSpec((C_HEADS, TM_PROJ, C_NOPE), lambda i: (0, i, 0))],
        out_shape=[jax.ShapeDtypeStruct((C_HEADS, t, qk_w), BF16),
                   jax.ShapeDtypeStruct((C_HEADS, t, qk_w), BF16),
                   jax.ShapeDtypeStruct((C_HEADS, t, C_NOPE), BF16)],
        compiler_params=_params("parallel"),
        name="proj_c",
    )(h, w_c, gqa, gkva, q_up, kv_up, gq, gk, cos, sin)


def _attn_kernel(q_ref, k_ref, v_ref, o_ref, s0_ref, s1_ref, m0_ref, m1_ref, *, tq, tk):
    g, _, dq = q_ref.shape
    rows = g * tq
    dv = v_ref.shape[-1]
    tiles = SEQ // tq
    contract_last = (((1,), (1,)), ((), ()))
    contract_first = (((0,), (0,)), ((), ()))

    def q_rows(tile):
        start = tile * tq
        return pl.ds(start if isinstance(tile, int) else pl.multiple_of(start, tq), tq)

    def passes(score_tile, score_bufs, out_tile, out_bufs):
        if score_tile is not None:
            sw_ref, mw_ref = score_bufs
            q = q_ref[:, q_rows(score_tile), :].reshape(rows, dq)
            m_new = None
        if out_tile is not None:
            sr_ref, mr_ref = out_bufs
            m = mr_ref[...]
            lsum = jnp.zeros((1, rows), F32)
            acc = jnp.zeros((dv, rows), F32)
        for c in range(SEQ // tk):
            keys = slice(c * tk, (c + 1) * tk)
            if score_tile is not None:
                s = lax.dot_general(k_ref[0, keys, :], q, contract_last, preferred_element_type=F32)
                sw_ref[keys, :] = s
                part = jnp.max(s, axis=0, keepdims=True)
                m_new = part if m_new is None else jnp.maximum(m_new, part)
            if out_tile is not None:
                p = jnp.exp2(sr_ref[keys, :] - m)
                lsum = lsum + jnp.sum(p, axis=0, keepdims=True)
                acc = acc + lax.dot_general(v_ref[0, keys, :], p.astype(BF16), contract_first,
                                            preferred_element_type=F32)
        if score_tile is not None:
            mw_ref[...] = m_new
        if out_tile is not None:
            o = (acc / lsum).T
            o_ref[:, q_rows(out_tile), :] = o.reshape(g, tq, dv).astype(o_ref.dtype)

    buf0, buf1 = (s0_ref, m0_ref), (s1_ref, m1_ref)
    passes(0, buf0, None, None)

    def pair(n, carry):
        tile = 2 * n
        passes(tile + 1, buf1, tile, buf0)

        @pl.when(n < tiles // 2 - 1)
        def _():
            passes(tile + 2, buf0, tile + 1, buf1)

        @pl.when(n == tiles // 2 - 1)
        def _():
            passes(None, None, tile + 1, buf1)

        return carry

    lax.fori_loop(0, tiles // 2, pair, 0)


def _attention(q, k, v, *, kv_heads, group, k_head0, v_head0, tq, tk):
    _, t, dq = q.shape
    dv = v.shape[-1]
    nb = t // SEQ
    rows = group * tq
    return pl.pallas_call(
        functools.partial(_attn_kernel, tq=tq, tk=tk),
        grid=(nb, kv_heads),
        in_specs=[pl.BlockSpec((group, SEQ, dq), lambda b, h: (h, b, 0)),
                  pl.BlockSpec((1, SEQ, dq), lambda b, h: (k_head0 + h, b, 0)),
                  pl.BlockSpec((1, SEQ, dv), lambda b, h: (v_head0 + h, b, 0))],
        out_specs=pl.BlockSpec((group, SEQ, dv), lambda b, h: (h, b, 0)),
        out_shape=jax.ShapeDtypeStruct((kv_heads * group, t, dv), BF16),
        scratch_shapes=[pltpu.VMEM((SEQ, rows), F32)] * 2 + [pltpu.VMEM((1, rows), F32)] * 2,
        compiler_params=_params("parallel", "parallel"),
        name="attention",
    )(q, k, v)


def _attn_b_kernel(q0, k0, v0, q1, k1, v1, q2, k2, v2, coef_ref, tab_ref, o_ref,
                   stage_ref, qd_ref, kd_ref, vd_ref, og_ref, eg_ref):
    scale = HEAD_DIM ** -0.5
    inputs = ((q0, k0, v0), (q1, k1, v1), (q2, k2, v2))
    chunk = 64

    def gather_classes(src_ref, dst_ref, dil):
        length = SEQ // dil
        stage_ref[...] = src_ref[0].astype(F32)

        def body(c, carry):
            for r in range(dil):
                src = pl.ds(pl.multiple_of(c * chunk * dil, chunk * dil) + r, chunk, stride=dil)
                dst = pl.ds(pl.multiple_of(r * length + c * chunk, chunk), chunk)
                dst_ref[dst, :] = stage_ref[src, :].astype(BF16)
            return carry

        lax.fori_loop(0, length // chunk, body, 0)

    ones = jnp.ones((B_WIDTH, HEAD_DIM), BF16)

    def tile(g, q, k, v, steps, coef, out_rows):
        s = lax.dot_general(q, k, (((1,), (1,)), ((), ())), preferred_element_type=F32)
        s = s * scale - coef * steps
        m = jnp.max(s, axis=-1, keepdims=True)
        p = jnp.exp(s - m).astype(BF16)
        ol = jnp.dot(p, jnp.concatenate([v, ones], axis=-1), preferred_element_type=F32)
        l = ol[:, HEAD_DIM:]
        og_ref[g, out_rows, :] = ol[:, :HEAD_DIM] / l
        eg_ref[g, out_rows, :] = m + jnp.log(l)

    for g, (_, dil) in enumerate(B_CONFIGS):
        length = SEQ // dil
        nt = length // TQ_B
        coef = coef_ref[0, g:g + 1, 0:1]
        if dil == 1:
            qs, ks, vs = (lambda rows, ref=ref: ref[0, rows, :] for ref in inputs[g])
        else:
            for src, dst in zip(inputs[g], (qd_ref, kd_ref, vd_ref)):
                gather_classes(src, dst, dil)
            qs, ks, vs = (lambda rows, ref=ref: ref[rows, :] for ref in (qd_ref, kd_ref, vd_ref))

        def residue(r, g=g, dil=dil, length=length, nt=nt, coef=coef, qs=qs, ks=ks, vs=vs):
            base = r * length
            for ti in range(nt):
                l0 = ti * TQ_B
                k0_ = min(max(l0 - B_SPAN, 0), length - B_WIDTH)
                steps = tab_ref[(l0 - k0_) // B_SPAN]
                if dil == 1:
                    q_rows, k_rows, out_rows = pl.ds(l0, TQ_B), pl.ds(k0_, B_WIDTH), pl.ds(l0, TQ_B)
                else:
                    q_rows = pl.ds(pl.multiple_of(base + l0, TQ_B), TQ_B)
                    k_rows = pl.ds(pl.multiple_of(base + k0_, B_SPAN), B_WIDTH)
                    out_rows = pl.ds(l0 * dil + r, TQ_B, stride=dil)
                tile(g, qs(q_rows), ks(k_rows), vs(k_rows), steps, coef, out_rows)

        if dil == 1:
            residue(0)
        else:
            per_step = max(1, B_TILES_PER_BLOCK // nt)

            def step(it, carry, residue=residue, per_step=per_step):
                for rr in range(per_step):
                    residue(it * per_step + rr)
                return carry

            lax.fori_loop(0, dil // per_step, step, 0)

    def merge(c, carry):
        rows = pl.ds(pl.multiple_of(c * B_MERGE_ROWS, B_MERGE_ROWS), B_MERGE_ROWS)
        lse = [eg_ref[g, rows, :] for g in range(len(B_CONFIGS))]
        mx = jnp.maximum(jnp.maximum(lse[0], lse[1]), lse[2])
        w = [jnp.exp(e - mx) for e in lse]
        den = w[0] + w[1] + w[2]
        o_ref[0, rows, :] = sum((w[g] / den) * og_ref[g, rows, :] for g in range(len(B_CONFIGS))).astype(o_ref.dtype)
        return carry

    lax.fori_loop(0, SEQ // B_MERGE_ROWS, merge, 0)


def _attn_b(qkv_heads, coef, tab):
    t = qkv_heads.shape[1]
    nb = t // SEQ
    base = A_IN // HEAD_DIM
    in_specs = []
    for g in range(len(B_CONFIGS)):
        for kind in range(3):
            head0 = base + kind * B_HEADS + g * B_SLOTS
            in_specs.append(pl.BlockSpec((1, SEQ, HEAD_DIM), lambda b, j, head0=head0: (head0 + j, b, 0)))
    in_specs.append(pl.BlockSpec((1, 8, V7X_LANES), lambda b, j: (j, 0, 0)))
    in_specs.append(pl.BlockSpec(tab.shape, lambda b, j: (0, 0, 0)))
    return pl.pallas_call(
        _attn_b_kernel,
        grid=(nb, B_SLOTS),
        in_specs=in_specs,
        out_specs=pl.BlockSpec((1, SEQ, HEAD_DIM), lambda b, j: (j, b, 0)),
        out_shape=jax.ShapeDtypeStruct((B_SLOTS, t, HEAD_DIM), BF16),
        scratch_shapes=[pltpu.VMEM((SEQ, HEAD_DIM), F32)]
                       + [pltpu.VMEM((SEQ, HEAD_DIM), BF16)] * 3
                       + [pltpu.VMEM((len(B_CONFIGS), SEQ, HEAD_DIM), F32)] * 2,
        compiler_params=_params("parallel", "parallel"),
        name="attn_b",
    )(*([qkv_heads] * 9), coef, tab)


def _mix_out_kernel(x_ref, oa_ref, ob_ref, oc_ref, gout_ref, w_ref, gn_ref, xo_ref, ho_ref, y_ref, d_ref,
                    *, n_tiles):
    i = pl.program_id(0)

    def group_norm(tiles, col0):
        width = len(tiles) * HEAD_DIM
        ssq = sum(jnp.sum(tl * tl, axis=-1, keepdims=True) for tl in tiles)
        r = lax.rsqrt(ssq * (1.0 / width) + EPS)
        for n, tl in enumerate(tiles):
            c = col0 + n * HEAD_DIM
            y_ref[:, c:c + HEAD_DIM] = ((tl * r) * gout_ref[:, c:c + HEAD_DIM]).astype(BF16)

    def finish():
        xn = x_ref[...] + d_ref[...]
        xo_ref[...] = xn
        ho_ref[...] = _rms(xn, gn_ref[...], D_MODEL).astype(BF16)

    def project():
        group_norm([oa_ref[n].astype(F32) for n in range(A_HEADS)], 0)
        group_norm([ob_ref[n].astype(F32) for n in range(B_SLOTS)], A_OUT)
        group_norm([oc_ref[n].astype(F32) for n in range(C_HEADS)], A_OUT + B_OUT)
        d_ref[...] = jnp.dot(y_ref[...], w_ref[...], preferred_element_type=F32)

    pl.when(i == 0)(project)

    @pl.when((i > 0) & (i < n_tiles))
    def _():
        finish()
        project()

    pl.when(i == n_tiles)(finish)


def _mix_out(x, oa, ob, oc, gain_out, w_out, gain_next):
    t = x.shape[0]
    n_tiles = t // TM_OUT
    cur = lambda i: jnp.minimum(i, n_tiles - 1)
    heads = lambda n: pl.BlockSpec((n, TM_OUT, HEAD_DIM), lambda i: (0, cur(i), 0))
    row = pl.BlockSpec((1, D_MODEL), lambda i: (0, 0))
    tok = pl.BlockSpec((TM_OUT, D_MODEL), lambda i: (jnp.maximum(i - 1, 0), 0))
    return pl.pallas_call(
        functools.partial(_mix_out_kernel, n_tiles=n_tiles),
        grid=(n_tiles + 1,),
        in_specs=[tok, heads(A_HEADS), heads(B_SLOTS), heads(C_HEADS), row,
                  pl.BlockSpec((D_MODEL, D_MODEL), lambda i: (0, 0), pipeline_mode=pl.Buffered(1)), row],
        out_specs=[tok, tok],
        out_shape=[jax.ShapeDtypeStruct((t, D_MODEL), F32),
                   jax.ShapeDtypeStruct((t, D_MODEL), BF16)],
        scratch_shapes=[pltpu.VMEM((TM_OUT, D_MODEL), BF16), pltpu.VMEM((TM_OUT, D_MODEL), F32)],
        compiler_params=_params("arbitrary"),
        name="mix_out",
    )(x, oa, ob, oc, gain_out, w_out, gain_next)


def _rope_tables(pos_a, pos_b):
    inv = ROPE_THETA ** (-jnp.arange(ROPE_HALF, dtype=F32) / ROPE_HALF)

    def one(pos):
        ang = pos.astype(F32)[:, None] * inv[None, :]
        return jnp.cos(ang), jnp.sin(ang)

    ca, sa = one(pos_a)
    cb, sb = one(pos_b)
    return jnp.concatenate([ca, cb, ca, cb], axis=-1), jnp.concatenate([-sa, -sb, sa, sb], axis=-1)


def _band_steps_table():
    i = jnp.arange(TQ_B)[:, None]
    jj = jnp.arange(B_WIDTH)[None, :]
    tabs = []
    for off in (0, B_SPAN, 2 * B_SPAN):
        rel = jnp.abs(off + i - jj)
        tabs.append(jnp.where(rel <= B_SPAN, rel.astype(F32), -NEG_BIG))
    return jnp.stack(tabs)


def _head_gains(gq_a, gk_a, gq_b, gk_b):
    ones = jnp.ones((HEAD_DIM,), F32)
    rows = ([gq_a] * A_HEADS + [gk_a] * A_KV_HEADS + [ones] * A_KV_HEADS
            + [gq_b] * B_HEADS + [gk_b] * B_HEADS + [ones] * B_HEADS)
    return jnp.stack(rows).astype(F32).reshape(AB_HEADS, 1, HEAD_DIM)


def kernel(x, ffn1_norm, ffn1_w_gu, ffn1_w_down, mix_norm, w_in, a_q_norm, a_k_norm, b_q_norm, b_k_norm, c_q_a_norm, c_q_up, c_kv_a_norm, c_kv_up, c_q_norm, c_k_norm, out_norm, w_out, ffn2_norm, ffn2_w_gu, ffn2_w_down):
    nb, s, d = x.shape
    assert (s, d) == (SEQ, D_MODEL)
    depth = w_in.shape[0]
    t = nb * s
    x = x.reshape(t, d)
    row = lambda v: v.reshape(1, -1).astype(F32)

    pos = jnp.arange(s, dtype=jnp.int32)
    cos_a, sin_a = _rope_tables(pos // GRID_W, pos % GRID_W)
    cos_c, sin_c = _rope_tables(pos, pos)
    lane_order = jnp.array(ROPE_LANE_ORDER)

    def rope_tile(v):
        zeros = jnp.zeros(v.shape[:-1] + (V7X_LANES // 2 - ROPE_HALF,), v.dtype)
        return jnp.concatenate([v[..., :ROPE_HALF], zeros, v[..., ROPE_HALF:], zeros], axis=-1)
    tab = _band_steps_table()
    slopes = 2.0 ** (-8.0 * jnp.arange(1, B_HEADS + 1, dtype=F32) / B_HEADS)
    dil = jnp.array([c[1] for c in B_CONFIGS], F32)
    coef = (slopes.reshape(len(B_CONFIGS), B_SLOTS) * dil[:, None]).T
    coef = jnp.zeros((B_SLOTS, 8, V7X_LANES), F32).at[:, :len(B_CONFIGS), :].set(coef[:, :, None])

    h = _norm(x, row(ffn1_norm[0]))
    for l in range(depth):
        x, h = _ffn(x, h, ffn1_w_gu[l].astype(BF16), ffn1_w_down[l].astype(BF16), row(mix_norm[l]))

        w = w_in[l]
        rot = (A_HEADS + A_KV_HEADS) * HEAD_DIM
        w_rot = w[:, :rot].reshape(D_MODEL, -1, HEAD_DIM)[:, :, lane_order].reshape(D_MODEL, rot)
        w_ab = jnp.concatenate([w_rot, w[:, rot:AB_IN]], axis=-1).astype(BF16)
        qkv = _proj_heads(h, w_ab, _head_gains(a_q_norm[l][lane_order], a_k_norm[l][lane_order],
                                               b_q_norm[l], b_k_norm[l]), cos_a, sin_a)
        w_c = jnp.concatenate([w[:, AB_IN:W_IN - C_ROPE], rope_tile(w[:, W_IN - C_ROPE:])], axis=-1).astype(BF16)
        q_up = c_q_up[l].reshape(C_Q_RANK, C_HEADS, C_QK)
        q_up = jnp.concatenate([q_up[:, :, :C_NOPE].reshape(C_Q_RANK, C_HEADS * C_NOPE),
                                rope_tile(q_up[:, :, C_NOPE:]).reshape(C_Q_RANK, C_HEADS * V7X_LANES)],
                               axis=-1).astype(BF16)
        split = lambda gvec: jnp.stack([gvec[:C_NOPE], rope_tile(gvec[C_NOPE:])]).astype(F32)
        qc, kc, vc = _proj_c(h, w_c, row(c_q_a_norm[l]), row(c_kv_a_norm[l]), q_up, c_kv_up[l].astype(BF16),
                             split(c_q_norm[l]), split(c_k_norm[l]), cos_c, sin_c)

        oa = _attention(qkv, qkv, qkv, kv_heads=A_KV_HEADS, group=A_HEADS // A_KV_HEADS,
                        k_head0=A_HEADS, v_head0=A_HEADS + A_KV_HEADS, tq=TQ_A, tk=TK_A)
        oc = _attention(qc, kc, vc, kv_heads=C_HEADS, group=1, k_head0=0, v_head0=0, tq=TQ_C, tk=TK_C)
        ob = _attn_b(qkv, coef, tab)

        x, h = _mix_out(x, oa, ob, oc, row(out_norm[l]), w_out[l].astype(BF16), row(ffn2_norm[l]))
        w_gu, w_down = ffn2_w_gu[l].astype(BF16), ffn2_w_down[l].astype(BF16)
        if l + 1 < depth:
            x, h = _ffn(x, h, w_gu, w_down, row(ffn1_norm[l + 1]))
        else:
            x, = _ffn(x, h, w_gu, w_down)
    return x.reshape(nb, s, d)
```

```python
import functools

import jax
import jax.numpy as jnp
from jax import lax
from jax.experimental import pallas as pl
from jax.experimental.pallas import tpu as pltpu

F32 = jnp.float32
BF16 = jnp.bfloat16

D_MODEL = 2048
SEQ = 4096
GRID_W = 64
HEAD_DIM = 128
ROPE_THETA = 10000.0
EPS = 1e-6
NEG_BIG = -1e30
LOG2_E = 1.4426950408889634

A_HEADS = 8
A_KV_HEADS = 2
B_CONFIGS = ((128, 1), (512, 4), (2048, 16))
B_SLOTS = 4
B_HEADS = B_SLOTS * len(B_CONFIGS)
C_HEADS = 4
C_Q_RANK = 512
C_KV_RANK = 256
C_NOPE = 128
C_ROPE = 64
C_QK = C_NOPE + C_ROPE
D_FF = 5632

A_Q = A_HEADS * HEAD_DIM
A_KV = A_KV_HEADS * HEAD_DIM
A_IN = A_Q + 2 * A_KV
B_QKV = B_HEADS * HEAD_DIM
B_IN = 3 * B_QKV
AB_IN = A_IN + B_IN
W_IN = AB_IN + C_Q_RANK + C_KV_RANK + C_ROPE
AB_HEADS = AB_IN // HEAD_DIM
A_OUT = A_HEADS * HEAD_DIM
B_OUT = B_SLOTS * HEAD_DIM
C_OUT = C_HEADS * HEAD_DIM

V7X_LANES = 128
V7X_VMEM_LIMIT = 56 * 1024 * 1024

TM_NORM = 512
CAST_ROWS_GU, CAST_ROWS_DOWN, CAST_ROWS_OUT = 256, 1408, 1024
TM_UP = 2048
TF_FFN = 512
TM_DOWN = 256
TM_PROJ = 2048
TN_PROJ = 512
C_PARTS = 8
TM_OUT = 512
TQ_B = 128
B_SPAN = 64
B_WIDTH = TQ_B + 2 * B_SPAN
B_MERGE_ROWS = 256
B_TILES_PER_BLOCK = 8
TQ_A, TK_A = 128, 256
TQ_C, TK_C = 512, 512
C_KV_BLOCK = 2


def _params(*sem):
    return pltpu.CompilerParams(dimension_semantics=sem, vmem_limit_bytes=V7X_VMEM_LIMIT)


def _rms(x, gain, width):
    ms = jnp.sum(x * x, axis=-1, keepdims=True) * (1.0 / width)
    return (x * lax.rsqrt(ms + EPS)) * gain


ROPE_HALF = 32
ROPE_LANE_ORDER = tuple(list(range(0, 32)) + list(range(64, 96)) + list(range(32, 64)) + list(range(96, 128)))


def _rope(y, cos, sin_signed):
    return y * cos + pltpu.roll(y, V7X_LANES // 2, 1) * sin_signed


def _norm_kernel(x_ref, g_ref, h_ref):
    h_ref[...] = _rms(x_ref[...], g_ref[...], D_MODEL).astype(BF16)


def _norm(x, gain):
    t = x.shape[0]
    return pl.pallas_call(
        _norm_kernel,
        grid=(t // TM_NORM,),
        in_specs=[pl.BlockSpec((TM_NORM, D_MODEL), lambda i: (i, 0)),
                  pl.BlockSpec((1, D_MODEL), lambda i: (0, 0))],
        out_specs=pl.BlockSpec((TM_NORM, D_MODEL), lambda i: (i, 0)),
        out_shape=jax.ShapeDtypeStruct((t, D_MODEL), BF16),
        compiler_params=_params("parallel"),
        name="norm",
    )(x, gain)


def _cast_kernel(w_ref, o_ref):
    o_ref[...] = w_ref[0].astype(BF16)


def _cast_layer(w, layer, rows):
    _, r, c = w.shape
    return pl.pallas_call(
        _cast_kernel,
        grid=(r // rows,),
        in_specs=[pl.BlockSpec((1, rows, c), lambda i: (layer, i, 0))],
        out_specs=pl.BlockSpec((rows, c), lambda i: (i, 0)),
        out_shape=jax.ShapeDtypeStruct((r, c), BF16),
        compiler_params=_params("parallel"),
        name="cast",
    )(w)


def _ffn_up_kernel(h_ref, wg_ref, wu_ref, a_ref, g_ref, u_ref, *, nf):
    j = pl.program_id(1)

    def swiglu():
        g = g_ref[...]
        a_ref[...] = (g * (1.0 / (1.0 + jnp.exp(-g))) * u_ref[...]).astype(BF16)

    def matmuls():
        h = h_ref[...]
        g_ref[...] = jnp.dot(h, wg_ref[...], preferred_element_type=F32)
        u_ref[...] = jnp.dot(h, wu_ref[...], preferred_element_type=F32)

    pl.when(j == 0)(matmuls)

    @pl.when((j > 0) & (j < nf))
    def _():
        swiglu()
        matmuls()

    pl.when(j == nf)(swiglu)


def _ffn_down_kernel(a_ref, wd_ref, x_ref, *rest, emit_next, n_tiles):
    if emit_next:
        gn_ref, xo_ref, ho_ref, d_ref = rest
    else:
        xo_ref, d_ref = rest
    i = pl.program_id(0)

    def finish():
        xn = x_ref[...] + 0.5 * d_ref[...]
        xo_ref[...] = xn
        if emit_next:
            ho_ref[...] = _rms(xn, gn_ref[...], D_MODEL).astype(BF16)

    def matmul():
        d_ref[...] = jnp.dot(a_ref[...], wd_ref[...], preferred_element_type=F32)

    pl.when(i == 0)(matmul)

    @pl.when((i > 0) & (i < n_tiles))
    def _():
        finish()
        matmul()

    pl.when(i == n_tiles)(finish)


def _ffn(x, h, w_gu, w_down, gain_next=None):
    t = x.shape[0]
    nf = D_FF // TF_FFN
    cur = lambda j: jnp.minimum(j, nf - 1)
    act = pl.pallas_call(
        functools.partial(_ffn_up_kernel, nf=nf),
        grid=(t // TM_UP, nf + 1),
        in_specs=[pl.BlockSpec((TM_UP, D_MODEL), lambda i, j: (i, 0)),
                  pl.BlockSpec((D_MODEL, TF_FFN), lambda i, j: (0, cur(j))),
                  pl.BlockSpec((D_MODEL, TF_FFN), lambda i, j: (0, cur(j) + nf))],
        out_specs=pl.BlockSpec((TM_UP, TF_FFN), lambda i, j: (i, jnp.maximum(j - 1, 0))),
        out_shape=jax.ShapeDtypeStruct((t, D_FF), BF16),
        scratch_shapes=[pltpu.VMEM((TM_UP, TF_FFN), F32)] * 2,
        compiler_params=_params("parallel", "arbitrary"),
        name="ffn_up",
    )(h, w_gu, w_gu)

    emit_next = gain_next is not None
    n_tiles = t // TM_DOWN
    tok = pl.BlockSpec((TM_DOWN, D_MODEL), lambda i: (jnp.maximum(i - 1, 0), 0))
    in_specs = [pl.BlockSpec((TM_DOWN, D_FF), lambda i: (jnp.minimum(i, n_tiles - 1), 0)),
                pl.BlockSpec((D_FF, D_MODEL), lambda i: (0, 0), pipeline_mode=pl.Buffered(1)),
                tok]
    args = [act, w_down, x]
    out_specs = [tok]
    out_shape = [jax.ShapeDtypeStruct((t, D_MODEL), F32)]
    if emit_next:
        in_specs.append(pl.BlockSpec((1, D_MODEL), lambda i: (0, 0)))
        args.append(gain_next)
        out_specs.append(tok)
        out_shape.append(jax.ShapeDtypeStruct((t, D_MODEL), BF16))
    return pl.pallas_call(
        functools.partial(_ffn_down_kernel, emit_next=emit_next, n_tiles=n_tiles),
        grid=(n_tiles + 1,),
        in_specs=in_specs,
        out_specs=out_specs,
        out_shape=out_shape,
        scratch_shapes=[pltpu.VMEM((TM_DOWN, D_MODEL), F32)],
        compiler_params=_params("arbitrary"),
        name="ffn_down",
    )(*args)


HEAD_PLAIN, HEAD_NORM, HEAD_NORM_ROPE = 0, 1, 2
A_Q_SCALE = HEAD_DIM ** -0.5 * LOG2_E
C_Q_SCALE = C_QK ** -0.5 * LOG2_E
AB_HEAD_KINDS = (((HEAD_NORM_ROPE, A_Q_SCALE),) * A_HEADS + ((HEAD_NORM_ROPE, 1.0),) * A_KV_HEADS
                 + ((HEAD_PLAIN, 1.0),) * A_KV_HEADS
                 + ((HEAD_NORM, 1.0),) * (2 * B_HEADS) + ((HEAD_PLAIN, 1.0),) * B_HEADS)


def _proj_heads_kernel(h_ref, w_ref, g_ref, cos_ref, sin_ref, o_ref, z_ref, *, step_kinds):
    j = pl.program_id(1)
    n_blocks = len(step_kinds)

    def step(kinds, with_dot):
        if kinds is not None:
            for hh, (kind, out_scale) in enumerate(kinds):
                y = z_ref[:, hh * HEAD_DIM:(hh + 1) * HEAD_DIM]
                if kind != HEAD_PLAIN:
                    y = _rms(y, g_ref[hh], HEAD_DIM)
                if kind == HEAD_NORM_ROPE:
                    y = _rope(y, cos_ref[...], sin_ref[...])
                if out_scale != 1.0:
                    y = y * out_scale
                o_ref[hh] = y.astype(BF16)
        if with_dot:
            z_ref[...] = jnp.dot(h_ref[...], w_ref[...], preferred_element_type=F32)

    pl.when(j == 0)(functools.partial(step, None, True))
    lo = 0
    while lo < n_blocks:
        hi = lo
        while hi < n_blocks and step_kinds[hi] == step_kinds[lo]:
            hi += 1
        last = min(hi, n_blocks - 1)
        if last > lo:
            pl.when((j > lo) & (j <= last))(functools.partial(step, step_kinds[lo], True))
        lo = hi
    pl.when(j == n_blocks)(functools.partial(step, step_kinds[-1], False))


def _proj_heads(h, w_ab, gains, cos, sin):
    t = h.shape[0]
    heads = TN_PROJ // HEAD_DIM
    n_blocks = AB_IN // TN_PROJ
    seq_tiles = SEQ // TM_PROJ
    step_kinds = tuple(AB_HEAD_KINDS[n:n + heads] for n in range(0, AB_HEADS, heads))
    prev = lambda j: jnp.maximum(j - 1, 0)
    return pl.pallas_call(
        functools.partial(_proj_heads_kernel, step_kinds=step_kinds),
        grid=(t // TM_PROJ, n_blocks + 1),
        in_specs=[pl.BlockSpec((TM_PROJ, D_MODEL), lambda i, j: (i, 0)),
                  pl.BlockSpec((D_MODEL, TN_PROJ), lambda i, j: (0, jnp.minimum(j, n_blocks - 1))),
                  pl.BlockSpec((heads, 1, HEAD_DIM), lambda i, j: (prev(j), 0, 0)),
                  pl.BlockSpec((TM_PROJ, HEAD_DIM), lambda i, j: (i % seq_tiles, 0)),
                  pl.BlockSpec((TM_PROJ, HEAD_DIM), lambda i, j: (i % seq_tiles, 0))],
        out_specs=pl.BlockSpec((heads, TM_PROJ, HEAD_DIM), lambda i, j: (prev(j), i, 0)),
        out_shape=jax.ShapeDtypeStruct((AB_HEADS, t, HEAD_DIM), BF16),
        scratch_shapes=[pltpu.VMEM((TM_PROJ, TN_PROJ), F32)],
        compiler_params=_params("parallel", "arbitrary"),
        name="proj_heads",
    )(h, w_ab, gains, cos, sin)


def _proj_c_kernel(h_ref, wc_ref, gqa_ref, gkva_ref, qup_ref, kvup_ref, gq_ref, gk_ref,
                   cos_ref, sin_ref, q_ref, k_ref, v_ref):
    nope_w = C_HEADS * C_NOPE
    part_rows = h_ref.shape[0] // C_PARTS
    for part in range(C_PARTS):
        rows = slice(part * part_rows, (part + 1) * part_rows)
        h = h_ref[rows, :]
        cos = cos_ref[rows, :]
        sin = sin_ref[rows, :]

        zq = jnp.dot(h, wc_ref[:, :C_Q_RANK], preferred_element_type=F32)
        zkv = jnp.dot(h, wc_ref[:, C_Q_RANK:], preferred_element_type=F32)
        q_lat = _rms(zq, gqa_ref[...], C_Q_RANK).astype(BF16)
        cq = jnp.dot(q_lat, qup_ref[...], preferred_element_type=F32)
        kv_lat = _rms(zkv[:, :C_KV_RANK], gkva_ref[...], C_KV_RANK).astype(BF16)
        k_rope = zkv[:, C_KV_RANK:]
        ckv = jnp.dot(kv_lat, kvup_ref[...], preferred_element_type=F32)

        for hh in range(C_HEADS):
            qn = cq[:, hh * C_NOPE:(hh + 1) * C_NOPE]
            qr = cq[:, nope_w + hh * V7X_LANES:nope_w + (hh + 1) * V7X_LANES]
            ms = (jnp.sum(qn * qn, axis=-1, keepdims=True)
                  + jnp.sum(qr * qr, axis=-1, keepdims=True)) * (1.0 / C_QK)
            r = lax.rsqrt(ms + EPS)
            q_ref[hh, rows, 0:C_NOPE] = (((qn * r) * gq_ref[0:1, :]) * C_Q_SCALE).astype(BF16)
            q_ref[hh, rows, C_NOPE:] = (_rope((qr * r) * gq_ref[1:2, :], cos, sin) * C_Q_SCALE).astype(BF16)

        k_rope_sq = jnp.sum(k_rope * k_rope, axis=-1, keepdims=True)
        for hh in range(C_HEADS):
            kn = ckv[:, hh * 2 * C_NOPE:hh * 2 * C_NOPE + C_NOPE]
            ms = (jnp.sum(kn * kn, axis=-1, keepdims=True) + k_rope_sq) * (1.0 / C_QK)
            r = lax.rsqrt(ms + EPS)
            k_ref[hh, rows, 0:C_NOPE] = ((kn * r) * gk_ref[0:1, :]).astype(BF16)
            k_ref[hh, rows, C_NOPE:] = _rope((k_rope * r) * gk_ref[1:2, :], cos, sin).astype(BF16)
            v_ref[hh, rows, :] = ckv[:, hh * 2 * C_NOPE + C_NOPE:(hh + 1) * 2 * C_NOPE].astype(BF16)


def _proj_c(h, w_c, gqa, gkva, q_up, kv_up, gq, gk, cos, sin):
    t = h.shape[0]
    seq_tiles = SEQ // TM_PROJ
    full = lambda a: pl.BlockSpec(a.shape, lambda i: (0,) * a.ndim)
    qk_w = 2 * V7X_LANES
    return pl.pallas_call(
        _proj_c_kernel,
        grid=(t // TM_PROJ,),
        in_specs=[pl.BlockSpec((TM_PROJ, D_MODEL), lambda i: (i, 0)),
                  full(w_c), full(gqa), full(gkva), full(q_up), full(kv_up), full(gq), full(gk),
                  pl.BlockSpec((TM_PROJ, V7X_LANES), lambda i: (i % seq_tiles, 0)),
                  pl.BlockSpec((TM_PROJ, V7X_LANES), lambda i: (i % seq_tiles, 0))],
        out_specs=[pl.BlockSpec((C_HEADS, TM_PROJ, qk_w), lambda i: (0, i, 0)),
                   pl.BlockSpec((C_HEADS, TM_PROJ, qk_w), lambda i: (0, i, 0)),
                   pl.BlockSpec((C_HEADS, TM_PROJ, C_NOPE), lambda i: (0, i, 0))],
        out_shape=[jax.ShapeDtypeStruct((C_HEADS, t, qk_w), BF16),
                   jax.ShapeDtypeStruct((C_HEADS, t, qk_w), BF16),
                   jax.ShapeDtypeStruct((C_HEADS, t, C_NOPE), BF16)],
        compiler_params=_params("parallel"),
        name="proj_c",
    )(h, w_c, gqa, gkva, q_up, kv_up, gq, gk, cos, sin)


def _attn_kernel(q_ref, k_ref, v_ref, o_ref, s0_ref, s1_ref, m0_ref, m1_ref, *, tq, tk):
    kv_block = k_ref.shape[0]
    g = q_ref.shape[0] // kv_block
    dq = q_ref.shape[-1]
    rows = g * tq
    dv = v_ref.shape[-1]
    seq_tiles = SEQ // tq
    tiles = kv_block * seq_tiles
    contract_last = (((1,), (1,)), ((), ()))
    contract_first = (((0,), (0,)), ((), ()))

    def where(tile):
        if isinstance(tile, int):
            head, start = tile // seq_tiles, (tile % seq_tiles) * tq
            return head, pl.ds(head * g, g), pl.ds(start, tq)
        head = tile // seq_tiles
        start = pl.multiple_of((tile % seq_tiles) * tq, tq)
        return head, pl.ds(head * g, g), pl.ds(start, tq)

    def passes(score_tile, score_bufs, out_tile, out_bufs):
        if score_tile is not None:
            sw_ref, mw_ref = score_bufs
            k_head, q_heads, q_rows = where(score_tile)
            q = q_ref[q_heads, q_rows, :].reshape(rows, dq)
            m_new = None
        if out_tile is not None:
            sr_ref, mr_ref = out_bufs
            v_head, o_heads, o_rows = where(out_tile)
            m = mr_ref[...]
            lsum = jnp.zeros((1, rows), F32)
            acc = jnp.zeros((dv, rows), F32)
        for c in range(SEQ // tk):
            keys = slice(c * tk, (c + 1) * tk)
            if score_tile is not None:
                s = lax.dot_general(k_ref[k_head, keys, :], q, contract_last, preferred_element_type=F32)
                sw_ref[keys, :] = s
                part = jnp.max(s, axis=0, keepdims=True)
                m_new = part if m_new is None else jnp.maximum(m_new, part)
            if out_tile is not None:
                p = jnp.exp2(sr_ref[keys, :] - m)
                lsum = lsum + jnp.sum(p, axis=0, keepdims=True)
                acc = acc + lax.dot_general(v_ref[v_head, keys, :], p.astype(BF16), contract_first,
                                            preferred_element_type=F32)
        if score_tile is not None:
            mw_ref[...] = m_new
        if out_tile is not None:
            o = (acc / lsum).T
            o_ref[o_heads, o_rows, :] = o.reshape(g, tq, dv).astype(o_ref.dtype)

    buf0, buf1 = (s0_ref, m0_ref), (s1_ref, m1_ref)
    passes(0, buf0, None, None)

    def pair(n, carry):
        tile = 2 * n
        passes(tile + 1, buf1, tile, buf0)

        @pl.when(n < tiles // 2 - 1)
        def _():
            passes(tile + 2, buf0, tile + 1, buf1)

        @pl.when(n == tiles // 2 - 1)
        def _():
            passes(None, None, tile + 1, buf1)

        return carry

    lax.fori_loop(0, tiles // 2, pair, 0)


def _attention(q, k, v, *, kv_heads, group, k_head0, v_head0, kv_block, tq, tk):
    _, t, dq = q.shape
    dv = v.shape[-1]
    nb = t // SEQ
    rows = group * tq
    assert kv_heads % kv_block == 0 and k_head0 % kv_block == 0 and v_head0 % kv_block == 0
    k_blk0, v_blk0 = k_head0 // kv_block, v_head0 // kv_block
    return pl.pallas_call(
        functools.partial(_attn_kernel, tq=tq, tk=tk),
        grid=(nb, kv_heads // kv_block),
        in_specs=[pl.BlockSpec((kv_block * group, SEQ, dq), lambda b, h: (h, b, 0)),
                  pl.BlockSpec((kv_block, SEQ, dq), lambda b, h: (k_blk0 + h, b, 0)),
                  pl.BlockSpec((kv_block, SEQ, dv), lambda b, h: (v_blk0 + h, b, 0))],
        out_specs=pl.BlockSpec((kv_block * group, SEQ, dv), lambda b, h: (h, b, 0)),
        out_shape=jax.ShapeDtypeStruct((kv_heads * group, t, dv), BF16),
        scratch_shapes=[pltpu.VMEM((SEQ, rows), F32)] * 2 + [pltpu.VMEM((1, rows), F32)] * 2,
        compiler_params=_params("parallel", "parallel"),
        name="attention",
    )(q, k, v)


def _attn_b_kernel(q0, k0, v0, q1, k1, v1, q2, k2, v2, coef_ref, tab_ref, o_ref,
                   stage_ref, qd_ref, kd_ref, vd_ref, og_ref, eg_ref):
    scale = HEAD_DIM ** -0.5
    inputs = ((q0, k0, v0), (q1, k1, v1), (q2, k2, v2))
    chunk = 64

    def gather_classes(src_ref, dst_ref, dil):
        length = SEQ // dil
        stage_ref[...] = src_ref[0].astype(F32)

        def body(c, carry):
            for r in range(dil):
                src = pl.ds(pl.multiple_of(c * chunk * dil, chunk * dil) + r, chunk, stride=dil)
                dst = pl.ds(pl.multiple_of(r * length + c * chunk, chunk), chunk)
                dst_ref[dst, :] = stage_ref[src, :].astype(BF16)
            return carry

        lax.fori_loop(0, length // chunk, body, 0)

    ones = jnp.ones((B_WIDTH, HEAD_DIM), BF16)

    def tile(g, q, k, v, steps, coef, out_rows):
        s = lax.dot_general(q, k, (((1,), (1,)), ((), ())), preferred_element_type=F32)
        s = s * scale - coef * steps
        m = jnp.max(s, axis=-1, keepdims=True)
        p = jnp.exp(s - m).astype(BF16)
        ol = jnp.dot(p, jnp.concatenate([v, ones], axis=-1), preferred_element_type=F32)
        l = ol[:, HEAD_DIM:]
        og_ref[g, out_rows, :] = ol[:, :HEAD_DIM] / l
        eg_ref[g, out_rows, :] = m + jnp.log(l)

    for g, (_, dil) in enumerate(B_CONFIGS):
        length = SEQ // dil
        nt = length // TQ_B
        coef = coef_ref[0, g:g + 1, 0:1]
        if dil == 1:
            qs, ks, vs = (lambda rows, ref=ref: ref[0, rows, :] for ref in inputs[g])
        else:
            for src, dst in zip(inputs[g], (qd_ref, kd_ref, vd_ref)):
                gather_classes(src, dst, dil)
            qs, ks, vs = (lambda rows, ref=ref: ref[rows, :] for ref in (qd_ref, kd_ref, vd_ref))

        def residue(r, g=g, dil=dil, length=length, nt=nt, coef=coef, qs=qs, ks=ks, vs=vs):
            base = r * length
            for ti in range(nt):
                l0 = ti * TQ_B
                k0_ = min(max(l0 - B_SPAN, 0), length - B_WIDTH)
                steps = tab_ref[(l0 - k0_) // B_SPAN]
                if dil == 1:
                    q_rows, k_rows, out_rows = pl.ds(l0, TQ_B), pl.ds(k0_, B_WIDTH), pl.ds(l0, TQ_B)
                else:
                    q_rows = pl.ds(pl.multiple_of(base + l0, TQ_B), TQ_B)
                    k_rows = pl.ds(pl.multiple_of(base + k0_, B_SPAN), B_WIDTH)
                    out_rows = pl.ds(l0 * dil + r, TQ_B, stride=dil)
                tile(g, qs(q_rows), ks(k_rows), vs(k_rows), steps, coef, out_rows)

        if dil == 1:
            residue(0)
        else:
            per_step = max(1, B_TILES_PER_BLOCK // nt)

            def step(it, carry, residue=residue, per_step=per_step):
                for rr in range(per_step):
                    residue(it * per_step + rr)
                return carry

            lax.fori_loop(0, dil // per_step, step, 0)

    def merge(c, carry):
        rows = pl.ds(pl.multiple_of(c * B_MERGE_ROWS, B_MERGE_ROWS), B_MERGE_ROWS)
        lse = [eg_ref[g, rows, :] for g in range(len(B_CONFIGS))]
        mx = jnp.maximum(jnp.maximum(lse[0], lse[1]), lse[2])
        w = [jnp.exp(e - mx) for e in lse]
        den = w[0] + w[1] + w[2]
        o_ref[0, rows, :] = sum((w[g] / den) * og_ref[g, rows, :] for g in range(len(B_CONFIGS))).astype(o_ref.dtype)
        return carry

    lax.fori_loop(0, SEQ // B_MERGE_ROWS, merge, 0)


def _attn_b(qkv_heads, coef, tab):
    t = qkv_heads.shape[1]
    nb = t // SEQ
    base = A_IN // HEAD_DIM
    in_specs = []
    for g in range(len(B_CONFIGS)):
        for kind in range(3):
            head0 = base + kind * B_HEADS + g * B_SLOTS
            in_specs.append(pl.BlockSpec((1, SEQ, HEAD_DIM), lambda b, j, head0=head0: (head0 + j, b, 0)))
    in_specs.append(pl.BlockSpec((1, 8, V7X_LANES), lambda b, j: (j, 0, 0)))
    in_specs.append(pl.BlockSpec(tab.shape, lambda b, j: (0, 0, 0)))
    return pl.pallas_call(
        _attn_b_kernel,
        grid=(nb, B_SLOTS),
        in_specs=in_specs,
        out_specs=pl.BlockSpec((1, SEQ, HEAD_DIM), lambda b, j: (j, b, 0)),
        out_shape=jax.ShapeDtypeStruct((B_SLOTS, t, HEAD_DIM), BF16),
        scratch_shapes=[pltpu.VMEM((SEQ, HEAD_DIM), F32)]
                       + [pltpu.VMEM((SEQ, HEAD_DIM), BF16)] * 3
                       + [pltpu.VMEM((len(B_CONFIGS), SEQ, HEAD_DIM), F32)] * 2,
        compiler_params=_params("parallel", "parallel"),
        name="attn_b",
    )(*([qkv_heads] * 9), coef, tab)


def _mix_out_kernel(x_ref, oa_ref, ob_ref, oc_ref, gout_ref, w_ref, gn_ref, xo_ref, ho_ref, y_ref, d_ref,
                    *, n_tiles):
    i = pl.program_id(0)

    def group_norm(tiles, col0):
        width = len(tiles) * HEAD_DIM
        ssq = sum(jnp.sum(tl * tl, axis=-1, keepdims=True) for tl in tiles)
        r = lax.rsqrt(ssq * (1.0 / width) + EPS)
        for n, tl in enumerate(tiles):
            c = col0 + n * HEAD_DIM
            y_ref[:, c:c + HEAD_DIM] = ((tl * r) * gout_ref[:, c:c + HEAD_DIM]).astype(BF16)

    def finish():
        xn = x_ref[...] + d_ref[...]
        xo_ref[...] = xn
        ho_ref[...] = _rms(xn, gn_ref[...], D_MODEL).astype(BF16)

    def project():
        group_norm([oa_ref[n].astype(F32) for n in range(A_HEADS)], 0)
        group_norm([ob_ref[n].astype(F32) for n in range(B_SLOTS)], A_OUT)
        group_norm([oc_ref[n].astype(F32) for n in range(C_HEADS)], A_OUT + B_OUT)
        d_ref[...] = jnp.dot(y_ref[...], w_ref[...], preferred_element_type=F32)

    pl.when(i == 0)(project)

    @pl.when((i > 0) & (i < n_tiles))
    def _():
        finish()
        project()

    pl.when(i == n_tiles)(finish)


def _mix_out(x, oa, ob, oc, gain_out, w_out, gain_next):
    t = x.shape[0]
    n_tiles = t // TM_OUT
    cur = lambda i: jnp.minimum(i, n_tiles - 1)
    heads = lambda n: pl.BlockSpec((n, TM_OUT, HEAD_DIM), lambda i: (0, cur(i), 0))
    row = pl.BlockSpec((1, D_MODEL), lambda i: (0, 0))
    tok = pl.BlockSpec((TM_OUT, D_MODEL), lambda i: (jnp.maximum(i - 1, 0), 0))
    return pl.pallas_call(
        functools.partial(_mix_out_kernel, n_tiles=n_tiles),
        grid=(n_tiles + 1,),
        in_specs=[tok, heads(A_HEADS), heads(B_SLOTS), heads(C_HEADS), row,
                  pl.BlockSpec((D_MODEL, D_MODEL), lambda i: (0, 0), pipeline_mode=pl.Buffered(1)), row],
        out_specs=[tok, tok],
        out_shape=[jax.ShapeDtypeStruct((t, D_MODEL), F32),
                   jax.ShapeDtypeStruct((t, D_MODEL), BF16)],
        scratch_shapes=[pltpu.VMEM((TM_OUT, D_MODEL), BF16), pltpu.VMEM((TM_OUT, D_MODEL), F32)],
        compiler_params=_params("arbitrary"),
        name="mix_out",
    )(x, oa, ob, oc, gain_out, w_out, gain_next)


def _rope_tables(pos_a, pos_b):
    inv = ROPE_THETA ** (-jnp.arange(ROPE_HALF, dtype=F32) / ROPE_HALF)

    def one(pos):
        ang = pos.astype(F32)[:, None] * inv[None, :]
        return jnp.cos(ang), jnp.sin(ang)

    ca, sa = one(pos_a)
    cb, sb = one(pos_b)
    return jnp.concatenate([ca, cb, ca, cb], axis=-1), jnp.concatenate([-sa, -sb, sa, sb], axis=-1)


def _band_steps_table():
    i = jnp.arange(TQ_B)[:, None]
    jj = jnp.arange(B_WIDTH)[None, :]
    tabs = []
    for off in (0, B_SPAN, 2 * B_SPAN):
        rel = jnp.abs(off + i - jj)
        tabs.append(jnp.where(rel <= B_SPAN, rel.astype(F32), -NEG_BIG))
    return jnp.stack(tabs)


def _head_gains(gq_a, gk_a, gq_b, gk_b):
    ones = jnp.ones((HEAD_DIM,), F32)
    rows = ([gq_a] * A_HEADS + [gk_a] * A_KV_HEADS + [ones] * A_KV_HEADS
            + [gq_b] * B_HEADS + [gk_b] * B_HEADS + [ones] * B_HEADS)
    return jnp.stack(rows).astype(F32).reshape(AB_HEADS, 1, HEAD_DIM)


def kernel(x, ffn1_norm, ffn1_w_gu, ffn1_w_down, mix_norm, w_in, a_q_norm, a_k_norm, b_q_norm, b_k_norm, c_q_a_norm, c_q_up, c_kv_a_norm, c_kv_up, c_q_norm, c_k_norm, out_norm, w_out, ffn2_norm, ffn2_w_gu, ffn2_w_down):
    nb, s, d = x.shape
    assert (s, d) == (SEQ, D_MODEL)
    depth = w_in.shape[0]
    t = nb * s
    x = x.reshape(t, d)
    row = lambda v: v.reshape(1, -1).astype(F32)

    pos = jnp.arange(s, dtype=jnp.int32)
    cos_a, sin_a = _rope_tables(pos // GRID_W, pos % GRID_W)
    cos_c, sin_c = _rope_tables(pos, pos)
    lane_order = jnp.array(ROPE_LANE_ORDER)

    def rope_tile(v):
        zeros = jnp.zeros(v.shape[:-1] + (V7X_LANES // 2 - ROPE_HALF,), v.dtype)
        return jnp.concatenate([v[..., :ROPE_HALF], zeros, v[..., ROPE_HALF:], zeros], axis=-1)
    tab = _band_steps_table()
    slopes = 2.0 ** (-8.0 * jnp.arange(1, B_HEADS + 1, dtype=F32) / B_HEADS)
    dil = jnp.array([c[1] for c in B_CONFIGS], F32)
    coef = (slopes.reshape(len(B_CONFIGS), B_SLOTS) * dil[:, None]).T
    coef = jnp.zeros((B_SLOTS, 8, V7X_LANES), F32).at[:, :len(B_CONFIGS), :].set(coef[:, :, None])

    h = _norm(x, row(ffn1_norm[0]))
    for l in range(depth):
        x, h = _ffn(x, h, _cast_layer(ffn1_w_gu, l, CAST_ROWS_GU), _cast_layer(ffn1_w_down, l, CAST_ROWS_DOWN),
                    row(mix_norm[l]))

        w = w_in[l]
        rot = (A_HEADS + A_KV_HEADS) * HEAD_DIM
        w_rot = w[:, :rot].reshape(D_MODEL, -1, HEAD_DIM)[:, :, lane_order].reshape(D_MODEL, rot)
        w_ab = jnp.concatenate([w_rot, w[:, rot:AB_IN]], axis=-1).astype(BF16)
        qkv = _proj_heads(h, w_ab, _head_gains(a_q_norm[l][lane_order], a_k_norm[l][lane_order],
                                               b_q_norm[l], b_k_norm[l]), cos_a, sin_a)
        w_c = jnp.concatenate([w[:, AB_IN:W_IN - C_ROPE], rope_tile(w[:, W_IN - C_ROPE:])], axis=-1).astype(BF16)
        q_up = c_q_up[l].reshape(C_Q_RANK, C_HEADS, C_QK)
        q_up = jnp.concatenate([q_up[:, :, :C_NOPE].reshape(C_Q_RANK, C_HEADS * C_NOPE),
                                rope_tile(q_up[:, :, C_NOPE:]).reshape(C_Q_RANK, C_HEADS * V7X_LANES)],
                               axis=-1).astype(BF16)
        split = lambda gvec: jnp.stack([gvec[:C_NOPE], rope_tile(gvec[C_NOPE:])]).astype(F32)
        qc, kc, vc = _proj_c(h, w_c, row(c_q_a_norm[l]), row(c_kv_a_norm[l]), q_up, c_kv_up[l].astype(BF16),
                             split(c_q_norm[l]), split(c_k_norm[l]), cos_c, sin_c)

        oa = _attention(qkv, qkv, qkv, kv_heads=A_KV_HEADS, group=A_HEADS // A_KV_HEADS,
                        k_head0=A_HEADS, v_head0=A_HEADS + A_KV_HEADS, kv_block=1, tq=TQ_A, tk=TK_A)
        oc = _attention(qc, kc, vc, kv_heads=C_HEADS, group=1, k_head0=0, v_head0=0, kv_block=C_KV_BLOCK,
                        tq=TQ_C, tk=TK_C)
        ob = _attn_b(qkv, coef, tab)

        x, h = _mix_out(x, oa, ob, oc, row(out_norm[l]), _cast_layer(w_out, l, CAST_ROWS_OUT), row(ffn2_norm[l]))
        w_gu = _cast_layer(ffn2_w_gu, l, CAST_ROWS_GU)
        w_down = _cast_layer(ffn2_w_down, l, CAST_ROWS_DOWN)
        if l + 1 < depth:
            x, h = _ffn(x, h, w_gu, w_down, row(ffn1_norm[l + 1]))
        else:
            x, = _ffn(x, h, w_gu, w_down)
    return x.reshape(nb, s, d)
```

```python
import functools

import jax
import jax.numpy as jnp
from jax import lax
from jax.experimental import pallas as pl
from jax.experimental.pallas import tpu as pltpu

F32 = jnp.float32
BF16 = jnp.bfloat16

D_MODEL = 2048
SEQ = 4096
GRID_W = 64
HEAD_DIM = 128
ROPE_THETA = 10000.0
EPS = 1e-6
NEG_BIG = -1e30
LOG2_E = 1.4426950408889634

A_HEADS = 8
A_KV_HEADS = 2
B_CONFIGS = ((128, 1), (512, 4), (2048, 16))
B_SLOTS = 4
B_HEADS = B_SLOTS * len(B_CONFIGS)
C_HEADS = 4
C_Q_RANK = 512
C_KV_RANK = 256
C_NOPE = 128
C_ROPE = 64
C_QK = C_NOPE + C_ROPE
D_FF = 5632

A_Q = A_HEADS * HEAD_DIM
A_KV = A_KV_HEADS * HEAD_DIM
A_IN = A_Q + 2 * A_KV
B_QKV = B_HEADS * HEAD_DIM
B_IN = 3 * B_QKV
AB_IN = A_IN + B_IN
W_IN = AB_IN + C_Q_RANK + C_KV_RANK + C_ROPE
AB_HEADS = AB_IN // HEAD_DIM
A_OUT = A_HEADS * HEAD_DIM
B_OUT = B_SLOTS * HEAD_DIM
C_OUT = C_HEADS * HEAD_DIM

V7X_LANES = 128
V7X_VMEM_LIMIT = 56 * 1024 * 1024

TM_NORM = 512
CAST_ROWS_GU, CAST_ROWS_DOWN, CAST_ROWS_OUT = 256, 1408, 1024
TM_UP = 2048
TF_FFN = 512
TM_DOWN = 256
TM_PROJ = 2048
TN_PROJ = 512
C_PARTS = 8
TM_OUT = 512
TQ_B = 128
B_SPAN = 64
B_WIDTH = TQ_B + 2 * B_SPAN
B_MERGE_ROWS = 256
B_TILES_PER_BLOCK = 8
TQ_A, TK_A = 128, 256
TQ_C, TK_C = 512, 512


def _params(*sem):
    return pltpu.CompilerParams(dimension_semantics=sem, vmem_limit_bytes=V7X_VMEM_LIMIT)


def _rms(x, gain, width):
    ms = jnp.sum(x * x, axis=-1, keepdims=True) * (1.0 / width)
    return (x * lax.rsqrt(ms + EPS)) * gain


ROPE_HALF = 32
ROPE_LANE_ORDER = tuple(list(range(0, 32)) + list(range(64, 96)) + list(range(32, 64)) + list(range(96, 128)))


def _rope(y, cos, sin_signed):
    return y * cos + pltpu.roll(y, V7X_LANES // 2, 1) * sin_signed


def _norm_kernel(x_ref, g_ref, h_ref):
    h_ref[...] = _rms(x_ref[...], g_ref[...], D_MODEL).astype(BF16)


def _norm(x, gain):
    t = x.shape[0]
    return pl.pallas_call(
        _norm_kernel,
        grid=(t // TM_NORM,),
        in_specs=[pl.BlockSpec((TM_NORM, D_MODEL), lambda i: (i, 0)),
                  pl.BlockSpec((1, D_MODEL), lambda i: (0, 0))],
        out_specs=pl.BlockSpec((TM_NORM, D_MODEL), lambda i: (i, 0)),
        out_shape=jax.ShapeDtypeStruct((t, D_MODEL), BF16),
        compiler_params=_params("parallel"),
        name="norm",
    )(x, gain)


def _cast_kernel(w_ref, o_ref):
    o_ref[...] = w_ref[0].astype(BF16)


def _cast_layer(w, layer, rows):
    _, r, c = w.shape
    return pl.pallas_call(
        _cast_kernel,
        grid=(r // rows,),
        in_specs=[pl.BlockSpec((1, rows, c), lambda i: (layer, i, 0))],
        out_specs=pl.BlockSpec((rows, c), lambda i: (i, 0)),
        out_shape=jax.ShapeDtypeStruct((r, c), BF16),
        compiler_params=_params("parallel"),
        name="cast",
    )(w)


def _ffn_up_kernel(h_ref, wg_ref, wu_ref, a_ref, g_ref, u_ref, *, nf):
    j = pl.program_id(1)

    def swiglu():
        g = g_ref[...]
        a_ref[...] = (g * (1.0 / (1.0 + jnp.exp(-g))) * u_ref[...]).astype(BF16)

    def matmuls():
        h = h_ref[...]
        g_ref[...] = jnp.dot(h, wg_ref[...], preferred_element_type=F32)
        u_ref[...] = jnp.dot(h, wu_ref[...], preferred_element_type=F32)

    pl.when(j == 0)(matmuls)

    @pl.when((j > 0) & (j < nf))
    def _():
        swiglu()
        matmuls()

    pl.when(j == nf)(swiglu)


def _ffn_down_kernel(a_ref, wd_ref, x_ref, *rest, emit_next, n_tiles):
    if emit_next:
        gn_ref, xo_ref, ho_ref, d_ref = rest
    else:
        xo_ref, d_ref = rest
    i = pl.program_id(0)

    def finish():
        xn = x_ref[...] + 0.5 * d_ref[...]
        xo_ref[...] = xn
        if emit_next:
            ho_ref[...] = _rms(xn, gn_ref[...], D_MODEL).astype(BF16)

    def matmul():
        d_ref[...] = jnp.dot(a_ref[...], wd_ref[...], preferred_element_type=F32)

    pl.when(i == 0)(matmul)

    @pl.when((i > 0) & (i < n_tiles))
    def _():
        finish()
        matmul()

    pl.when(i == n_tiles)(finish)


def _ffn(x, h, w_gu, w_down, gain_next=None):
    t = x.shape[0]
    nf = D_FF // TF_FFN
    cur = lambda j: jnp.minimum(j, nf - 1)
    act = pl.pallas_call(
        functools.partial(_ffn_up_kernel, nf=nf),
        grid=(t // TM_UP, nf + 1),
        in_specs=[pl.BlockSpec((TM_UP, D_MODEL), lambda i, j: (i, 0)),
                  pl.BlockSpec((D_MODEL, TF_FFN), lambda i, j: (0, cur(j))),
                  pl.BlockSpec((D_MODEL, TF_FFN), lambda i, j: (0, cur(j) + nf))],
        out_specs=pl.BlockSpec((TM_UP, TF_FFN), lambda i, j: (i, jnp.maximum(j - 1, 0))),
        out_shape=jax.ShapeDtypeStruct((t, D_FF), BF16),
        scratch_shapes=[pltpu.VMEM((TM_UP, TF_FFN), F32)] * 2,
        compiler_params=_params("parallel", "arbitrary"),
        name="ffn_up",
    )(h, w_gu, w_gu)

    emit_next = gain_next is not None
    n_tiles = t // TM_DOWN
    tok = pl.BlockSpec((TM_DOWN, D_MODEL), lambda i: (jnp.maximum(i - 1, 0), 0))
    in_specs = [pl.BlockSpec((TM_DOWN, D_FF), lambda i: (jnp.minimum(i, n_tiles - 1), 0)),
                pl.BlockSpec((D_FF, D_MODEL), lambda i: (0, 0), pipeline_mode=pl.Buffered(1)),
                tok]
    args = [act, w_down, x]
    out_specs = [tok]
    out_shape = [jax.ShapeDtypeStruct((t, D_MODEL), F32)]
    if emit_next:
        in_specs.append(pl.BlockSpec((1, D_MODEL), lambda i: (0, 0)))
        args.append(gain_next)
        out_specs.append(tok)
        out_shape.append(jax.ShapeDtypeStruct((t, D_MODEL), BF16))
    return pl.pallas_call(
        functools.partial(_ffn_down_kernel, emit_next=emit_next, n_tiles=n_tiles),
        grid=(n_tiles + 1,),
        in_specs=in_specs,
        out_specs=out_specs,
        out_shape=out_shape,
        scratch_shapes=[pltpu.VMEM((TM_DOWN, D_MODEL), F32)],
        compiler_params=_params("arbitrary"),
        name="ffn_down",
    )(*args)


HEAD_PLAIN, HEAD_NORM, HEAD_NORM_ROPE = 0, 1, 2
A_Q_SCALE = HEAD_DIM ** -0.5 * LOG2_E
C_Q_SCALE = C_QK ** -0.5 * LOG2_E
AB_HEAD_KINDS = (((HEAD_NORM_ROPE, A_Q_SCALE),) * A_HEADS + ((HEAD_NORM_ROPE, 1.0),) * A_KV_HEADS
                 + ((HEAD_PLAIN, 1.0),) * A_KV_HEADS
                 + ((HEAD_NORM, 1.0),) * (2 * B_HEADS) + ((HEAD_PLAIN, 1.0),) * B_HEADS)


def _proj_heads_kernel(h_ref, w_ref, g_ref, cos_ref, sin_ref, o_ref, z_ref, *, step_kinds):
    j = pl.program_id(1)
    n_blocks = len(step_kinds)

    def step(kinds, with_dot):
        if kinds is not None:
            for hh, (kind, out_scale) in enumerate(kinds):
                y = z_ref[:, hh * HEAD_DIM:(hh + 1) * HEAD_DIM]
                if kind != HEAD_PLAIN:
                    y = _rms(y, g_ref[hh], HEAD_DIM)
                if kind == HEAD_NORM_ROPE:
                    y = _rope(y, cos_ref[...], sin_ref[...])
                if out_scale != 1.0:
                    y = y * out_scale
                o_ref[hh] = y.astype(BF16)
        if with_dot:
            z_ref[...] = jnp.dot(h_ref[...], w_ref[...], preferred_element_type=F32)

    pl.when(j == 0)(functools.partial(step, None, True))
    lo = 0
    while lo < n_blocks:
        hi = lo
        while hi < n_blocks and step_kinds[hi] == step_kinds[lo]:
            hi += 1
        last = min(hi, n_blocks - 1)
        if last > lo:
            pl.when((j > lo) & (j <= last))(functools.partial(step, step_kinds[lo], True))
        lo = hi
    pl.when(j == n_blocks)(functools.partial(step, step_kinds[-1], False))


def _proj_heads(h, w_ab, gains, cos, sin):
    t = h.shape[0]
    heads = TN_PROJ // HEAD_DIM
    n_blocks = AB_IN // TN_PROJ
    seq_tiles = SEQ // TM_PROJ
    step_kinds = tuple(AB_HEAD_KINDS[n:n + heads] for n in range(0, AB_HEADS, heads))
    prev = lambda j: jnp.maximum(j - 1, 0)
    return pl.pallas_call(
        functools.partial(_proj_heads_kernel, step_kinds=step_kinds),
        grid=(t // TM_PROJ, n_blocks + 1),
        in_specs=[pl.BlockSpec((TM_PROJ, D_MODEL), lambda i, j: (i, 0)),
                  pl.BlockSpec((D_MODEL, TN_PROJ), lambda i, j: (0, jnp.minimum(j, n_blocks - 1))),
                  pl.BlockSpec((heads, 1, HEAD_DIM), lambda i, j: (prev(j), 0, 0)),
                  pl.BlockSpec((TM_PROJ, HEAD_DIM), lambda i, j: (i % seq_tiles, 0)),
                  pl.BlockSpec((TM_PROJ, HEAD_DIM), lambda i, j: (i % seq_tiles, 0))],
        out_specs=pl.BlockSpec((heads, TM_PROJ, HEAD_DIM), lambda i, j: (prev(j), i, 0)),
        out_shape=jax.ShapeDtypeStruct((AB_HEADS, t, HEAD_DIM), BF16),
        scratch_shapes=[pltpu.VMEM((TM_PROJ, TN_PROJ), F32)],
        compiler_params=_params("parallel", "arbitrary"),
        name="proj_heads",
    )(h, w_ab, gains, cos, sin)


def _proj_c_kernel(h_ref, wc_ref, gqa_ref, gkva_ref, qup_ref, kvup_ref, gq_ref, gk_ref,
                   cos_ref, sin_ref, q_ref, k_ref, v_ref):
    nope_w = C_HEADS * C_NOPE
    part_rows = h_ref.shape[0] // C_PARTS
    for part in range(C_PARTS):
        rows = slice(part * part_rows, (part + 1) * part_rows)
        h = h_ref[rows, :]
        cos = cos_ref[rows, :]
        sin = sin_ref[rows, :]

        zq = jnp.dot(h, wc_ref[:, :C_Q_RANK], preferred_element_type=F32)
        zkv = jnp.dot(h, wc_ref[:, C_Q_RANK:], preferred_element_type=F32)
        q_lat = _rms(zq, gqa_ref[...], C_Q_RANK).astype(BF16)
        cq = jnp.dot(q_lat, qup_ref[...], preferred_element_type=F32)
        kv_lat = _rms(zkv[:, :C_KV_RANK], gkva_ref[...], C_KV_RANK).astype(BF16)
        k_rope = zkv[:, C_KV_RANK:]
        ckv = jnp.dot(kv_lat, kvup_ref[...], preferred_element_type=F32)

        for hh in range(C_HEADS):
            qn = cq[:, hh * C_NOPE:(hh + 1) * C_NOPE]
            qr = cq[:, nope_w + hh * V7X_LANES:nope_w + (hh + 1) * V7X_LANES]
            ms = (jnp.sum(qn * qn, axis=-1, keepdims=True)
                  + jnp.sum(qr * qr, axis=-1, keepdims=True)) * (1.0 / C_QK)
            r = lax.rsqrt(ms + EPS)
            q_ref[hh, rows, 0:C_NOPE] = (((qn * r) * gq_ref[0:1, :]) * C_Q_SCALE).astype(BF16)
            q_ref[hh, rows, C_NOPE:] = (_rope((qr * r) * gq_ref[1:2, :], cos, sin) * C_Q_SCALE).astype(BF16)

        k_rope_sq = jnp.sum(k_rope * k_rope, axis=-1, keepdims=True)
        for hh in range(C_HEADS):
            kn = ckv[:, hh * 2 * C_NOPE:hh * 2 * C_NOPE + C_NOPE]
            ms = (jnp.sum(kn * kn, axis=-1, keepdims=True) + k_rope_sq) * (1.0 / C_QK)
            r = lax.rsqrt(ms + EPS)
            k_ref[hh, rows, 0:C_NOPE] = ((kn * r) * gk_ref[0:1, :]).astype(BF16)
            k_ref[hh, rows, C_NOPE:] = _rope((k_rope * r) * gk_ref[1:2, :], cos, sin).astype(BF16)
            v_ref[hh, rows, :] = ckv[:, hh * 2 * C_NOPE + C_NOPE:(hh + 1) * 2 * C_NOPE].astype(BF16)


def _proj_c(h, w_c, gqa, gkva, q_up, kv_up, gq, gk, cos, sin):
    t = h.shape[0]
    seq_tiles = SEQ // TM_PROJ
    full = lambda a: pl.BlockSpec(a.shape, lambda i: (0,) * a.ndim)
    qk_w = 2 * V7X_LANES
    return pl.pallas_call(
        _proj_c_kernel,
        grid=(t // TM_PROJ,),
        in_specs=[pl.BlockSpec((TM_PROJ, D_MODEL), lambda i: (i, 0)),
                  full(w_c), full(gqa), full(gkva), full(q_up), full(kv_up), full(gq), full(gk),
                  pl.BlockSpec((TM_PROJ, V7X_LANES), lambda i: (i % seq_tiles, 0)),
                  pl.BlockSpec((TM_PROJ, V7X_LANES), lambda i: (i % seq_tiles, 0))],
        out_specs=[pl.BlockSpec((C_HEADS, TM_PROJ, qk_w), lambda i: (0, i, 0)),
                   pl.BlockSpec((C_HEADS, TM_PROJ, qk_w), lambda i: (0, i, 0)),
                   pl.BlockSpec((C_HEADS, TM_PROJ, C_NOPE), lambda i: (0, i, 0))],
        out_shape=[jax.ShapeDtypeStruct((C_HEADS, t, qk_w), BF16),
                   jax.ShapeDtypeStruct((C_HEADS, t, qk_w), BF16),
                   jax.ShapeDtypeStruct((C_HEADS, t, C_NOPE), BF16)],
        compiler_params=_params("parallel"),
        name="proj_c",
    )(h, w_c, gqa, gkva, q_up, kv_up, gq, gk, cos, sin)


def _attn_kernel(q_ref, k_ref, v_ref, o_ref, s0_ref, s1_ref, m0_ref, m1_ref, *, tq, tk):
    g, _, dq = q_ref.shape
    rows = g * tq
    dv = v_ref.shape[-1]
    tiles = SEQ // tq
    contract_last = (((1,), (1,)), ((), ()))
    contract_first = (((0,), (0,)), ((), ()))

    def q_rows(tile):
        start = tile * tq
        return pl.ds(start if isinstance(tile, int) else pl.multiple_of(start, tq), tq)

    def passes(score_tile, score_bufs, out_tile, out_bufs):
        if score_tile is not None:
            sw_ref, mw_ref = score_bufs
            q = q_ref[:, q_rows(score_tile), :].reshape(rows, dq)
            m_new = None
        if out_tile is not None:
            sr_ref, mr_ref = out_bufs
            m = mr_ref[...]
            lsum = jnp.zeros((1, rows), F32)
            acc = jnp.zeros((dv, rows), F32)
        for c in range(SEQ // tk):
            keys = slice(c * tk, (c + 1) * tk)
            if score_tile is not None:
                s = lax.dot_general(k_ref[0, keys, :], q, contract_last, preferred_element_type=F32)
                sw_ref[keys, :] = s
                part = jnp.max(s, axis=0, keepdims=True)
                m_new = part if m_new is None else jnp.maximum(m_new, part)
            if out_tile is not None:
                p = jnp.exp2(sr_ref[keys, :] - m)
                lsum = lsum + jnp.sum(p, axis=0, keepdims=True)
                acc = acc + lax.dot_general(v_ref[0, keys, :], p.astype(BF16), contract_first,
                                            preferred_element_type=F32)
        if score_tile is not None:
            mw_ref[...] = m_new
        if out_tile is not None:
            o = (acc / lsum).T
            o_ref[:, q_rows(out_tile), :] = o.reshape(g, tq, dv).astype(o_ref.dtype)

    buf0, buf1 = (s0_ref, m0_ref), (s1_ref, m1_ref)
    passes(0, buf0, None, None)

    def pair(n, carry):
        tile = 2 * n
        passes(tile + 1, buf1, tile, buf0)

        @pl.when(n < tiles // 2 - 1)
        def _():
            passes(tile + 2, buf0, tile + 1, buf1)

        @pl.when(n == tiles // 2 - 1)
        def _():
            passes(None, None, tile + 1, buf1)

        return carry

    lax.fori_loop(0, tiles // 2, pair, 0)


def _attention(q, k, v, *, kv_heads, group, k_head0, v_head0, tq, tk):
    _, t, dq = q.shape
    dv = v.shape[-1]
    nb = t // SEQ
    rows = group * tq
    return pl.pallas_call(
        functools.partial(_attn_kernel, tq=tq, tk=tk),
        grid=(nb, kv_heads),
        in_specs=[pl.BlockSpec((group, SEQ, dq), lambda b, h: (h, b, 0)),
                  pl.BlockSpec((1, SEQ, dq), lambda b, h: (k_head0 + h, b, 0)),
                  pl.BlockSpec((1, SEQ, dv), lambda b, h: (v_head0 + h, b, 0))],
        out_specs=pl.BlockSpec((group, SEQ, dv), lambda b, h: (h, b, 0)),
        out_shape=jax.ShapeDtypeStruct((kv_heads * group, t, dv), BF16),
        scratch_shapes=[pltpu.VMEM((SEQ, rows), F32)] * 2 + [pltpu.VMEM((1, rows), F32)] * 2,
        compiler_params=_params("parallel", "parallel"),
        name="attention",
    )(q, k, v)


def _attn_b_kernel(q0, k0, v0, q1, k1, v1, q2, k2, v2, coef_ref, tab_ref, o_ref,
                   stage_ref, qd_ref, kd_ref, vd_ref, og_ref, eg_ref):
    scale = HEAD_DIM ** -0.5
    inputs = ((q0, k0, v0), (q1, k1, v1), (q2, k2, v2))
    chunk = 64

    def gather_classes(src_ref, dst_ref, dil):
        length = SEQ // dil
        stage_ref[...] = src_ref[0].astype(F32)

        def body(c, carry):
            for r in range(dil):
                src = pl.ds(pl.multiple_of(c * chunk * dil, chunk * dil) + r, chunk, stride=dil)
                dst = pl.ds(pl.multiple_of(r * length + c * chunk, chunk), chunk)
                dst_ref[dst, :] = stage_ref[src, :].astype(BF16)
            return carry

        lax.fori_loop(0, length // chunk, body, 0)

    ones = jnp.ones((B_WIDTH, HEAD_DIM), BF16)

    def tile(g, q, k, v, steps, coef, out_rows):
        s = lax.dot_general(q, k, (((1,), (1,)), ((), ())), preferred_element_type=F32)
        s = s * scale - coef * steps
        m = jnp.max(s, axis=-1, keepdims=True)
        p = jnp.exp(s - m).astype(BF16)
        ol = jnp.dot(p, jnp.concatenate([v, ones], axis=-1), preferred_element_type=F32)
        l = ol[:, HEAD_DIM:]
        og_ref[g, out_rows, :] = ol[:, :HEAD_DIM] / l
        eg_ref[g, out_rows, :] = m + jnp.log(l)

    for g, (_, dil) in enumerate(B_CONFIGS):
        length = SEQ // dil
        nt = length // TQ_B
        coef = coef_ref[0, g:g + 1, 0:1]
        if dil == 1:
            qs, ks, vs = (lambda rows, ref=ref: ref[0, rows, :] for ref in inputs[g])
        else:
            for src, dst in zip(inputs[g], (qd_ref, kd_ref, vd_ref)):
                gather_classes(src, dst, dil)
            qs, ks, vs = (lambda rows, ref=ref: ref[rows, :] for ref in (qd_ref, kd_ref, vd_ref))

        def residue(r, g=g, dil=dil, length=length, nt=nt, coef=coef, qs=qs, ks=ks, vs=vs):
            base = r * length
            for ti in range(nt):
                l0 = ti * TQ_B
                k0_ = min(max(l0 - B_SPAN, 0), length - B_WIDTH)
                steps = tab_ref[(l0 - k0_) // B_SPAN]
                if dil == 1:
                    q_rows, k_rows, out_rows = pl.ds(l0, TQ_B), pl.ds(k0_, B_WIDTH), pl.ds(l0, TQ_B)
                else:
                    q_rows = pl.ds(pl.multiple_of(base + l0, TQ_B), TQ_B)
                    k_rows = pl.ds(pl.multiple_of(base + k0_, B_SPAN), B_WIDTH)
                    out_rows = pl.ds(l0 * dil + r, TQ_B, stride=dil)
                tile(g, qs(q_rows), ks(k_rows), vs(k_rows), steps, coef, out_rows)

        if dil == 1:
            residue(0)
        else:
            per_step = max(1, B_TILES_PER_BLOCK // nt)

            def step(it, carry, residue=residue, per_step=per_step):
                for rr in range(per_step):
                    residue(it * per_step + rr)
                return carry

            lax.fori_loop(0, dil // per_step, step, 0)

    def merge(c, carry):
        rows = pl.ds(pl.multiple_of(c * B_MERGE_ROWS, B_MERGE_ROWS), B_MERGE_ROWS)
        lse = [eg_ref[g, rows, :] for g in range(len(B_CONFIGS))]
        mx = jnp.maximum(jnp.maximum(lse[0], lse[1]), lse[2])
        w = [jnp.exp(e - mx) for e in lse]
        den = w[0] + w[1] + w[2]
        o_ref[0, rows, :] = sum((w[g] / den) * og_ref[g, rows, :] for g in range(len(B_CONFIGS))).astype(o_ref.dtype)
        return carry

    lax.fori_loop(0, SEQ // B_MERGE_ROWS, merge, 0)


def _attn_b(qkv_heads, coef, tab):
    t = qkv_heads.shape[1]
    nb = t // SEQ
    base = A_IN // HEAD_DIM
    in_specs = []
    for g in range(len(B_CONFIGS)):
        for kind in range(3):
            head0 = base + kind * B_HEADS + g * B_SLOTS
            in_specs.append(pl.BlockSpec((1, SEQ, HEAD_DIM), lambda b, j, head0=head0: (head0 + j, b, 0)))
    in_specs.append(pl.BlockSpec((1, 8, V7X_LANES), lambda b, j: (j, 0, 0)))
    in_specs.append(pl.BlockSpec(tab.shape, lambda b, j: (0, 0, 0)))
    return pl.pallas_call(
        _attn_b_kernel,
        grid=(nb, B_SLOTS),
        in_specs=in_specs,
        out_specs=pl.BlockSpec((1, SEQ, HEAD_DIM), lambda b, j: (j, b, 0)),
        out_shape=jax.ShapeDtypeStruct((B_SLOTS, t, HEAD_DIM), BF16),
        scratch_shapes=[pltpu.VMEM((SEQ, HEAD_DIM), F32)]
                       + [pltpu.VMEM((SEQ, HEAD_DIM), BF16)] * 3
                       + [pltpu.VMEM((len(B_CONFIGS), SEQ, HEAD_DIM), F32)] * 2,
        compiler_params=_params("parallel", "parallel"),
        name="attn_b",
    )(*([qkv_heads] * 9), coef, tab)


def _mix_out_kernel(x_ref, oa_ref, ob_ref, oc_ref, gout_ref, w_ref, gn_ref, xo_ref, ho_ref, y_ref, d_ref,
                    *, n_tiles):
    i = pl.program_id(0)

    def group_norm(tiles, col0):
        width = len(tiles) * HEAD_DIM
        ssq = sum(jnp.sum(tl * tl, axis=-1, keepdims=True) for tl in tiles)
        r = lax.rsqrt(ssq * (1.0 / width) + EPS)
        for n, tl in enumerate(tiles):
            c = col0 + n * HEAD_DIM
            y_ref[:, c:c + HEAD_DIM] = ((tl * r) * gout_ref[:, c:c + HEAD_DIM]).astype(BF16)

    def finish():
        xn = x_ref[...] + d_ref[...]
        xo_ref[...] = xn
        ho_ref[...] = _rms(xn, gn_ref[...], D_MODEL).astype(BF16)

    def project():
        group_norm([oa_ref[n].astype(F32) for n in range(A_HEADS)], 0)
        group_norm([ob_ref[n].astype(F32) for n in range(B_SLOTS)], A_OUT)
        group_norm([oc_ref[n].astype(F32) for n in range(C_HEADS)], A_OUT + B_OUT)
        d_ref[...] = jnp.dot(y_ref[...], w_ref[...], preferred_element_type=F32)

    pl.when(i == 0)(project)

    @pl.when((i > 0) & (i < n_tiles))
    def _():
        finish()
        project()

    pl.when(i == n_tiles)(finish)


def _mix_out(x, oa, ob, oc, gain_out, w_out, gain_next):
    t = x.shape[0]
    n_tiles = t // TM_OUT
    cur = lambda i: jnp.minimum(i, n_tiles - 1)
    heads = lambda n: pl.BlockSpec((n, TM_OUT, HEAD_DIM), lambda i: (0, cur(i), 0))
    row = pl.BlockSpec((1, D_MODEL), lambda i: (0, 0))
    tok = pl.BlockSpec((TM_OUT, D_MODEL), lambda i: (jnp.maximum(i - 1, 0), 0))
    return pl.pallas_call(
        functools.partial(_mix_out_kernel, n_tiles=n_tiles),
        grid=(n_tiles + 1,),
        in_specs=[tok, heads(A_HEADS), heads(B_SLOTS), heads(C_HEADS), row,
                  pl.BlockSpec((D_MODEL, D_MODEL), lambda i: (0, 0), pipeline_mode=pl.Buffered(1)), row],
        out_specs=[tok, tok],
        out_shape=[jax.ShapeDtypeStruct((t, D_MODEL), F32),
                   jax.ShapeDtypeStruct((t, D_MODEL), BF16)],
        scratch_shapes=[pltpu.VMEM((TM_OUT, D_MODEL), BF16), pltpu.VMEM((TM_OUT, D_MODEL), F32)],
        compiler_params=_params("arbitrary"),
        name="mix_out",
    )(x, oa, ob, oc, gain_out, w_out, gain_next)


def _rope_tables(pos_a, pos_b):
    inv = ROPE_THETA ** (-jnp.arange(ROPE_HALF, dtype=F32) / ROPE_HALF)

    def one(pos):
        ang = pos.astype(F32)[:, None] * inv[None, :]
        return jnp.cos(ang), jnp.sin(ang)

    ca, sa = one(pos_a)
    cb, sb = one(pos_b)
    return jnp.concatenate([ca, cb, ca, cb], axis=-1), jnp.concatenate([-sa, -sb, sa, sb], axis=-1)


def _band_steps_table():
    i = jnp.arange(TQ_B)[:, None]
    jj = jnp.arange(B_WIDTH)[None, :]
    tabs = []
    for off in (0, B_SPAN, 2 * B_SPAN):
        rel = jnp.abs(off + i - jj)
        tabs.append(jnp.where(rel <= B_SPAN, rel.astype(F32), -NEG_BIG))
    return jnp.stack(tabs)


def _head_gains(gq_a, gk_a, gq_b, gk_b):
    ones = jnp.ones((HEAD_DIM,), F32)
    rows = ([gq_a] * A_HEADS + [gk_a] * A_KV_HEADS + [ones] * A_KV_HEADS
            + [gq_b] * B_HEADS + [gk_b] * B_HEADS + [ones] * B_HEADS)
    return jnp.stack(rows).astype(F32).reshape(AB_HEADS, 1, HEAD_DIM)


def kernel(x, ffn1_norm, ffn1_w_gu, ffn1_w_down, mix_norm, w_in, a_q_norm, a_k_norm, b_q_norm, b_k_norm, c_q_a_norm, c_q_up, c_kv_a_norm, c_kv_up, c_q_norm, c_k_norm, out_norm, w_out, ffn2_norm, ffn2_w_gu, ffn2_w_down):
    nb, s, d = x.shape
    assert (s, d) == (SEQ, D_MODEL)
    depth = w_in.shape[0]
    t = nb * s
    x = x.reshape(t, d)
    row = lambda v: v.reshape(1, -1).astype(F32)

    pos = jnp.arange(s, dtype=jnp.int32)
    cos_a, sin_a = _rope_tables(pos // GRID_W, pos % GRID_W)
    cos_c, sin_c = _rope_tables(pos, pos)
    lane_order = jnp.array(ROPE_LANE_ORDER)

    def rope_tile(v):
        zeros = jnp.zeros(v.shape[:-1] + (V7X_LANES // 2 - ROPE_HALF,), v.dtype)
        return jnp.concatenate([v[..., :ROPE_HALF], zeros, v[..., ROPE_HALF:], zeros], axis=-1)
    tab = _band_steps_table()
    slopes = 2.0 ** (-8.0 * jnp.arange(1, B_HEADS + 1, dtype=F32) / B_HEADS)
    dil = jnp.array([c[1] for c in B_CONFIGS], F32)
    coef = (slopes.reshape(len(B_CONFIGS), B_SLOTS) * dil[:, None]).T
    coef = jnp.zeros((B_SLOTS, 8, V7X_LANES), F32).at[:, :len(B_CONFIGS), :].set(coef[:, :, None])

    h = _norm(x, row(ffn1_norm[0]))
    for l in range(depth):
        x, h = _ffn(x, h, _cast_layer(ffn1_w_gu, l, CAST_ROWS_GU), _cast_layer(ffn1_w_down, l, CAST_ROWS_DOWN),
                    row(mix_norm[l]))

        w = w_in[l]
        rot = (A_HEADS + A_KV_HEADS) * HEAD_DIM
        w_rot = w[:, :rot].reshape(D_MODEL, -1, HEAD_DIM)[:, :, lane_order].reshape(D_MODEL, rot)
        w_ab = jnp.concatenate([w_rot, w[:, rot:AB_IN]], axis=-1).astype(BF16)
        qkv = _proj_heads(h, w_ab, _head_gains(a_q_norm[l][lane_order], a_k_norm[l][lane_order],
                                               b_q_norm[l], b_k_norm[l]), cos_a, sin_a)
        w_c = jnp.concatenate([w[:, AB_IN:W_IN - C_ROPE], rope_tile(w[:, W_IN - C_ROPE:])], axis=-1).astype(BF16)
        q_up = c_q_up[l].reshape(C_Q_RANK, C_HEADS, C_QK)
        q_up = jnp.concatenate([q_up[:, :, :C_NOPE].reshape(C_Q_RANK, C_HEADS * C_NOPE),
                                rope_tile(q_up[:, :, C_NOPE:]).reshape(C_Q_RANK, C_HEADS * V7X_LANES)],
                               axis=-1).astype(BF16)
        split = lambda gvec: jnp.stack([gvec[:C_NOPE], rope_tile(gvec[C_NOPE:])]).astype(F32)
        qc, kc, vc = _proj_c(h, w_c, row(c_q_a_norm[l]), row(c_kv_a_norm[l]), q_up, c_kv_up[l].astype(BF16),
                             split(c_q_norm[l]), split(c_k_norm[l]), cos_c, sin_c)

        oa = _attention(qkv, qkv, qkv, kv_heads=A_KV_HEADS, group=A_HEADS // A_KV_HEADS,
                        k_head0=A_HEADS, v_head0=A_HEADS + A_KV_HEADS, tq=TQ_A, tk=TK_A)
        oc = _attention(qc, kc, vc, kv_heads=C_HEADS, group=1, k_head0=0, v_head0=0, tq=TQ_C, tk=TK_C)
        ob = _attn_b(qkv, coef, tab)

        x, h = _mix_out(x, oa, ob, oc, row(out_norm[l]), _cast_layer(w_out, l, CAST_ROWS_OUT), row(ffn2_norm[l]))
        w_gu = _cast_layer(ffn2_w_gu, l, CAST_ROWS_GU)
        w_down = _cast_layer(ffn2_w_down, l, CAST_ROWS_DOWN)
        if l + 1 < depth:
            x, h = _ffn(x, h, w_gu, w_down, row(ffn1_norm[l + 1]))
        else:
            x, = _ffn(x, h, w_gu, w_down)
    return x.reshape(nb, s, d)
```

```python
import functools

import jax
import jax.numpy as jnp
import numpy as np
from jax import lax
from jax.experimental import pallas as pl
from jax.experimental.pallas import tpu as pltpu

F32 = jnp.float32
BF16 = jnp.bfloat16

D_MODEL = 2048
SEQ = 4096
GRID_W = 64
HEAD_DIM = 128
ROPE_THETA = 10000.0
EPS = 1e-6
NEG_BIG = -1e30
LOG2_E = 1.4426950408889634

A_HEADS = 8
A_KV_HEADS = 2
B_CONFIGS = ((128, 1), (512, 4), (2048, 16))
B_SLOTS = 4
B_HEADS = B_SLOTS * len(B_CONFIGS)
C_HEADS = 4
C_Q_RANK = 512
C_KV_RANK = 256
C_NOPE = 128
C_ROPE = 64
C_QK = C_NOPE + C_ROPE
D_FF = 5632

A_Q = A_HEADS * HEAD_DIM
A_KV = A_KV_HEADS * HEAD_DIM
A_IN = A_Q + 2 * A_KV
B_QKV = B_HEADS * HEAD_DIM
B_IN = 3 * B_QKV
AB_IN = A_IN + B_IN
W_IN = AB_IN + C_Q_RANK + C_KV_RANK + C_ROPE
AB_HEADS = AB_IN // HEAD_DIM
A_OUT = A_HEADS * HEAD_DIM
B_OUT = B_SLOTS * HEAD_DIM
C_OUT = C_HEADS * HEAD_DIM

V7X_LANES = 128
V7X_VMEM_LIMIT = 56 * 1024 * 1024

TM_NORM = 512
CAST_ROWS_GU, CAST_ROWS_DOWN, CAST_ROWS_OUT = 256, 1408, 1024
TM_UP = 2048
TF_FFN = 512
TM_DOWN = 256
TM_PROJ = 2048
TN_PROJ = 512
C_PARTS = 8
TM_OUT = 512
TQ_B = 128
B_SPAN = 64
B_WIDTH = TQ_B + 2 * B_SPAN
B_MERGE_ROWS = 256
B_TILES_PER_BLOCK = 8
TQ_A, TK_A = 128, 256
TQ_C, TK_C = 512, 512


def _params(*sem):
    return pltpu.CompilerParams(dimension_semantics=sem, vmem_limit_bytes=V7X_VMEM_LIMIT)


def _rms(x, gain, width):
    ms = jnp.sum(x * x, axis=-1, keepdims=True) * (1.0 / width)
    return (x * lax.rsqrt(ms + EPS)) * gain


ROPE_HALF = 32
ROPE_LANE_ORDER = tuple(list(range(0, 32)) + list(range(64, 96)) + list(range(32, 64)) + list(range(96, 128)))


def _rope(y, cos, sin_signed):
    return y * cos + pltpu.roll(y, V7X_LANES // 2, 1) * sin_signed


def _norm_kernel(x_ref, g_ref, h_ref):
    h_ref[...] = _rms(x_ref[...], g_ref[...], D_MODEL).astype(BF16)


def _norm(x, gain):
    t = x.shape[0]
    return pl.pallas_call(
        _norm_kernel,
        grid=(t // TM_NORM,),
        in_specs=[pl.BlockSpec((TM_NORM, D_MODEL), lambda i: (i, 0)),
                  pl.BlockSpec((1, D_MODEL), lambda i: (0, 0))],
        out_specs=pl.BlockSpec((TM_NORM, D_MODEL), lambda i: (i, 0)),
        out_shape=jax.ShapeDtypeStruct((t, D_MODEL), BF16),
        compiler_params=_params("parallel"),
        name="norm",
    )(x, gain)


def _cast_kernel(w_ref, o_ref):
    o_ref[...] = w_ref[0].astype(BF16)


def _cast_layer(w, layer, rows):
    _, r, c = w.shape
    return pl.pallas_call(
        _cast_kernel,
        grid=(r // rows,),
        in_specs=[pl.BlockSpec((1, rows, c), lambda i: (layer, i, 0))],
        out_specs=pl.BlockSpec((rows, c), lambda i: (i, 0)),
        out_shape=jax.ShapeDtypeStruct((r, c), BF16),
        compiler_params=_params("parallel"),
        name="cast",
    )(w)


def _ffn_up_kernel(h_ref, wg_ref, wu_ref, a_ref, g_ref, u_ref, *, nf):
    j = pl.program_id(1)

    def swiglu():
        g = g_ref[...]
        a_ref[...] = (g * (1.0 / (1.0 + jnp.exp(-g))) * u_ref[...]).astype(BF16)

    def matmuls():
        h = h_ref[...]
        g_ref[...] = jnp.dot(h, wg_ref[...], preferred_element_type=F32)
        u_ref[...] = jnp.dot(h, wu_ref[...], preferred_element_type=F32)

    pl.when(j == 0)(matmuls)

    @pl.when((j > 0) & (j < nf))
    def _():
        swiglu()
        matmuls()

    pl.when(j == nf)(swiglu)


def _ffn_down_kernel(a_ref, wd_ref, x_ref, *rest, emit_next, n_tiles):
    if emit_next:
        gn_ref, xo_ref, ho_ref, d_ref = rest
    else:
        xo_ref, d_ref = rest
    i = pl.program_id(0)

    def finish():
        xn = x_ref[...] + 0.5 * d_ref[...]
        xo_ref[...] = xn
        if emit_next:
            ho_ref[...] = _rms(xn, gn_ref[...], D_MODEL).astype(BF16)

    def matmul():
        d_ref[...] = jnp.dot(a_ref[...], wd_ref[...], preferred_element_type=F32)

    pl.when(i == 0)(matmul)

    @pl.when((i > 0) & (i < n_tiles))
    def _():
        finish()
        matmul()

    pl.when(i == n_tiles)(finish)


def _ffn(x, h, w_gu, w_down, gain_next=None):
    t = x.shape[0]
    nf = D_FF // TF_FFN
    cur = lambda j: jnp.minimum(j, nf - 1)
    act = pl.pallas_call(
        functools.partial(_ffn_up_kernel, nf=nf),
        grid=(t // TM_UP, nf + 1),
        in_specs=[pl.BlockSpec((TM_UP, D_MODEL), lambda i, j: (i, 0)),
                  pl.BlockSpec((D_MODEL, TF_FFN), lambda i, j: (0, cur(j))),
                  pl.BlockSpec((D_MODEL, TF_FFN), lambda i, j: (0, cur(j) + nf))],
        out_specs=pl.BlockSpec((TM_UP, TF_FFN), lambda i, j: (i, jnp.maximum(j - 1, 0))),
        out_shape=jax.ShapeDtypeStruct((t, D_FF), BF16),
        scratch_shapes=[pltpu.VMEM((TM_UP, TF_FFN), F32)] * 2,
        compiler_params=_params("parallel", "arbitrary"),
        name="ffn_up",
    )(h, w_gu, w_gu)

    emit_next = gain_next is not None
    n_tiles = t // TM_DOWN
    tok = pl.BlockSpec((TM_DOWN, D_MODEL), lambda i: (jnp.maximum(i - 1, 0), 0))
    in_specs = [pl.BlockSpec((TM_DOWN, D_FF), lambda i: (jnp.minimum(i, n_tiles - 1), 0)),
                pl.BlockSpec((D_FF, D_MODEL), lambda i: (0, 0), pipeline_mode=pl.Buffered(1)),
                tok]
    args = [act, w_down, x]
    out_specs = [tok]
    out_shape = [jax.ShapeDtypeStruct((t, D_MODEL), F32)]
    if emit_next:
        in_specs.append(pl.BlockSpec((1, D_MODEL), lambda i: (0, 0)))
        args.append(gain_next)
        out_specs.append(tok)
        out_shape.append(jax.ShapeDtypeStruct((t, D_MODEL), BF16))
    return pl.pallas_call(
        functools.partial(_ffn_down_kernel, emit_next=emit_next, n_tiles=n_tiles),
        grid=(n_tiles + 1,),
        in_specs=in_specs,
        out_specs=out_specs,
        out_shape=out_shape,
        scratch_shapes=[pltpu.VMEM((TM_DOWN, D_MODEL), F32)],
        compiler_params=_params("arbitrary"),
        name="ffn_down",
    )(*args)


HEAD_PLAIN, HEAD_NORM, HEAD_NORM_ROPE = 0, 1, 2
A_Q_SCALE = HEAD_DIM ** -0.5 * LOG2_E
C_Q_SCALE = C_QK ** -0.5 * LOG2_E
AB_HEAD_KINDS = (((HEAD_NORM_ROPE, A_Q_SCALE),) * A_HEADS + ((HEAD_NORM_ROPE, 1.0),) * A_KV_HEADS
                 + ((HEAD_PLAIN, 1.0),) * A_KV_HEADS
                 + ((HEAD_NORM, 1.0),) * (2 * B_HEADS) + ((HEAD_PLAIN, 1.0),) * B_HEADS)


def _proj_heads_kernel(h_ref, w_ref, g_ref, cos_ref, sin_ref, o_ref, z_ref, *, step_kinds):
    j = pl.program_id(1)
    n_blocks = len(step_kinds)

    def step(kinds, with_dot):
        if kinds is not None:
            for hh, (kind, out_scale) in enumerate(kinds):
                y = z_ref[:, hh * HEAD_DIM:(hh + 1) * HEAD_DIM]
                if kind != HEAD_PLAIN:
                    y = _rms(y, g_ref[hh], HEAD_DIM)
                if kind == HEAD_NORM_ROPE:
                    y = _rope(y, cos_ref[...], sin_ref[...])
                if out_scale != 1.0:
                    y = y * out_scale
                o_ref[hh] = y.astype(BF16)
        if with_dot:
            z_ref[...] = jnp.dot(h_ref[...], w_ref[...], preferred_element_type=F32)

    pl.when(j == 0)(functools.partial(step, None, True))
    lo = 0
    while lo < n_blocks:
        hi = lo
        while hi < n_blocks and step_kinds[hi] == step_kinds[lo]:
            hi += 1
        last = min(hi, n_blocks - 1)
        if last > lo:
            pl.when((j > lo) & (j <= last))(functools.partial(step, step_kinds[lo], True))
        lo = hi
    pl.when(j == n_blocks)(functools.partial(step, step_kinds[-1], False))


def _proj_heads(h, w_ab, gains, cos, sin):
    t = h.shape[0]
    heads = TN_PROJ // HEAD_DIM
    n_blocks = AB_IN // TN_PROJ
    seq_tiles = SEQ // TM_PROJ
    step_kinds = tuple(AB_HEAD_KINDS[n:n + heads] for n in range(0, AB_HEADS, heads))
    prev = lambda j: jnp.maximum(j - 1, 0)
    return pl.pallas_call(
        functools.partial(_proj_heads_kernel, step_kinds=step_kinds),
        grid=(t // TM_PROJ, n_blocks + 1),
        in_specs=[pl.BlockSpec((TM_PROJ, D_MODEL), lambda i, j: (i, 0)),
                  pl.BlockSpec((D_MODEL, TN_PROJ), lambda i, j: (0, jnp.minimum(j, n_blocks - 1))),
                  pl.BlockSpec((heads, 1, HEAD_DIM), lambda i, j: (prev(j), 0, 0)),
                  pl.BlockSpec((TM_PROJ, HEAD_DIM), lambda i, j: (i % seq_tiles, 0)),
                  pl.BlockSpec((TM_PROJ, HEAD_DIM), lambda i, j: (i % seq_tiles, 0))],
        out_specs=pl.BlockSpec((heads, TM_PROJ, HEAD_DIM), lambda i, j: (prev(j), i, 0)),
        out_shape=jax.ShapeDtypeStruct((AB_HEADS, t, HEAD_DIM), BF16),
        scratch_shapes=[pltpu.VMEM((TM_PROJ, TN_PROJ), F32)],
        compiler_params=_params("parallel", "arbitrary"),
        name="proj_heads",
    )(h, w_ab, gains, cos, sin)


def _proj_c_kernel(h_ref, wc_ref, gqa_ref, gkva_ref, qup_ref, kvup_ref, gq_ref, gk_ref,
                   cos_ref, sin_ref, q_ref, k_ref, v_ref):
    nope_w = C_HEADS * C_NOPE
    part_rows = h_ref.shape[0] // C_PARTS
    for part in range(C_PARTS):
        rows = slice(part * part_rows, (part + 1) * part_rows)
        h = h_ref[rows, :]
        cos = cos_ref[rows, :]
        sin = sin_ref[rows, :]

        zq = jnp.dot(h, wc_ref[:, :C_Q_RANK], preferred_element_type=F32)
        zkv = jnp.dot(h, wc_ref[:, C_Q_RANK:], preferred_element_type=F32)
        q_lat = _rms(zq, gqa_ref[...], C_Q_RANK).astype(BF16)
        cq = jnp.dot(q_lat, qup_ref[...], preferred_element_type=F32)
        kv_lat = _rms(zkv[:, :C_KV_RANK], gkva_ref[...], C_KV_RANK).astype(BF16)
        k_rope = zkv[:, C_KV_RANK:]
        ckv = jnp.dot(kv_lat, kvup_ref[...], preferred_element_type=F32)

        for hh in range(C_HEADS):
            qn = cq[:, hh * C_NOPE:(hh + 1) * C_NOPE]
            qr = cq[:, nope_w + hh * V7X_LANES:nope_w + (hh + 1) * V7X_LANES]
            ms = (jnp.sum(qn * qn, axis=-1, keepdims=True)
                  + jnp.sum(qr * qr, axis=-1, keepdims=True)) * (1.0 / C_QK)
            r = lax.rsqrt(ms + EPS)
            q_ref[hh, rows, 0:C_NOPE] = (((qn * r) * gq_ref[0:1, :]) * C_Q_SCALE).astype(BF16)
            q_ref[hh, rows, C_NOPE:] = (_rope((qr * r) * gq_ref[1:2, :], cos, sin) * C_Q_SCALE).astype(BF16)

        k_rope_sq = jnp.sum(k_rope * k_rope, axis=-1, keepdims=True)
        for hh in range(C_HEADS):
            kn = ckv[:, hh * 2 * C_NOPE:hh * 2 * C_NOPE + C_NOPE]
            ms = (jnp.sum(kn * kn, axis=-1, keepdims=True) + k_rope_sq) * (1.0 / C_QK)
            r = lax.rsqrt(ms + EPS)
            k_ref[hh, rows, 0:C_NOPE] = ((kn * r) * gk_ref[0:1, :]).astype(BF16)
            k_ref[hh, rows, C_NOPE:] = _rope((k_rope * r) * gk_ref[1:2, :], cos, sin).astype(BF16)
            v_ref[hh, rows, :] = ckv[:, hh * 2 * C_NOPE + C_NOPE:(hh + 1) * 2 * C_NOPE].astype(BF16)


def _proj_c(h, w_c, gqa, gkva, q_up, kv_up, gq, gk, cos, sin):
    t = h.shape[0]
    seq_tiles = SEQ // TM_PROJ
    full = lambda a: pl.BlockSpec(a.shape, lambda i: (0,) * a.ndim)
    qk_w = 2 * V7X_LANES
    return pl.pallas_call(
        _proj_c_kernel,
        grid=(t // TM_PROJ,),
        in_specs=[pl.BlockSpec((TM_PROJ, D_MODEL), lambda i: (i, 0)),
                  full(w_c), full(gqa), full(gkva), full(q_up), full(kv_up), full(gq), full(gk),
                  pl.BlockSpec((TM_PROJ, V7X_LANES), lambda i: (i % seq_tiles, 0)),
                  pl.BlockSpec((TM_PROJ, V7X_LANES), lambda i: (i % seq_tiles, 0))],
        out_specs=[pl.BlockSpec((C_HEADS, TM_PROJ, qk_w), lambda i: (0, i, 0)),
                   pl.BlockSpec((C_HEADS, TM_PROJ, qk_w), lambda i: (0, i, 0)),
                   pl.BlockSpec((C_HEADS, TM_PROJ, C_NOPE), lambda i: (0, i, 0))],
        out_shape=[jax.ShapeDtypeStruct((C_HEADS, t, qk_w), BF16),
                   jax.ShapeDtypeStruct((C_HEADS, t, qk_w), BF16),
                   jax.ShapeDtypeStruct((C_HEADS, t, C_NOPE), BF16)],
        compiler_params=_params("parallel"),
        name="proj_c",
    )(h, w_c, gqa, gkva, q_up, kv_up, gq, gk, cos, sin)


def _attn_kernel(q_ref, k_ref, v_ref, o_ref, s0_ref, s1_ref, m0_ref, m1_ref, *, tq, tk):
    g, _, dq = q_ref.shape
    rows = g * tq
    dv = v_ref.shape[-1]
    tiles = SEQ // tq
    contract_last = (((1,), (1,)), ((), ()))
    contract_first = (((0,), (0,)), ((), ()))

    def q_rows(tile):
        start = tile * tq
        return pl.ds(start if isinstance(tile, int) else pl.multiple_of(start, tq), tq)

    def passes(score_tile, score_bufs, out_tile, out_bufs):
        if score_tile is not None:
            sw_ref, mw_ref = score_bufs
            q = q_ref[:, q_rows(score_tile), :].reshape(rows, dq)
            m_new = None
        if out_tile is not None:
            sr_ref, mr_ref = out_bufs
            m = mr_ref[...]
            lsum = jnp.zeros((1, rows), F32)
            acc = jnp.zeros((dv, rows), F32)
        for c in range(SEQ // tk):
            keys = slice(c * tk, (c + 1) * tk)
            if score_tile is not None:
                s = lax.dot_general(k_ref[0, keys, :], q, contract_last, preferred_element_type=F32)
                sw_ref[keys, :] = s
                part = jnp.max(s, axis=0, keepdims=True)
                m_new = part if m_new is None else jnp.maximum(m_new, part)
            if out_tile is not None:
                p = jnp.exp2(sr_ref[keys, :] - m)
                lsum = lsum + jnp.sum(p, axis=0, keepdims=True)
                acc = acc + lax.dot_general(v_ref[0, keys, :], p.astype(BF16), contract_first,
                                            preferred_element_type=F32)
        if score_tile is not None:
            mw_ref[...] = m_new
        if out_tile is not None:
            o = (acc / lsum).T
            o_ref[:, q_rows(out_tile), :] = o.reshape(g, tq, dv).astype(o_ref.dtype)

    buf0, buf1 = (s0_ref, m0_ref), (s1_ref, m1_ref)
    passes(0, buf0, None, None)

    def pair(n, carry):
        tile = 2 * n
        passes(tile + 1, buf1, tile, buf0)

        @pl.when(n < tiles // 2 - 1)
        def _():
            passes(tile + 2, buf0, tile + 1, buf1)

        @pl.when(n == tiles // 2 - 1)
        def _():
            passes(None, None, tile + 1, buf1)

        return carry

    lax.fori_loop(0, tiles // 2, pair, 0)


def _attention(q, k, v, *, kv_heads, group, k_head0, v_head0, tq, tk):
    _, t, dq = q.shape
    dv = v.shape[-1]
    nb = t // SEQ
    rows = group * tq
    return pl.pallas_call(
        functools.partial(_attn_kernel, tq=tq, tk=tk),
        grid=(nb, kv_heads),
        in_specs=[pl.BlockSpec((group, SEQ, dq), lambda b, h: (h, b, 0)),
                  pl.BlockSpec((1, SEQ, dq), lambda b, h: (k_head0 + h, b, 0)),
                  pl.BlockSpec((1, SEQ, dv), lambda b, h: (v_head0 + h, b, 0))],
        out_specs=pl.BlockSpec((group, SEQ, dv), lambda b, h: (h, b, 0)),
        out_shape=jax.ShapeDtypeStruct((kv_heads * group, t, dv), BF16),
        scratch_shapes=[pltpu.VMEM((SEQ, rows), F32)] * 2 + [pltpu.VMEM((1, rows), F32)] * 2,
        compiler_params=_params("parallel", "parallel"),
        name="attention",
    )(q, k, v)


def _attn_b_kernel(q0, k0, v0, q1, k1, v1, q2, k2, v2, coef_ref, tab_ref, o_ref,
                   stage_ref, qd_ref, kd_ref, vd_ref, og_ref, eg_ref):
    scale = HEAD_DIM ** -0.5
    inputs = ((q0, k0, v0), (q1, k1, v1), (q2, k2, v2))
    chunk = 64

    def gather_classes(src_ref, dst_ref, dil):
        length = SEQ // dil
        stage_ref[...] = src_ref[0].astype(F32)

        def body(c, carry):
            for r in range(dil):
                src = pl.ds(pl.multiple_of(c * chunk * dil, chunk * dil) + r, chunk, stride=dil)
                dst = pl.ds(pl.multiple_of(r * length + c * chunk, chunk), chunk)
                dst_ref[dst, :] = stage_ref[src, :].astype(BF16)
            return carry

        lax.fori_loop(0, length // chunk, body, 0)

    ones = jnp.ones((B_WIDTH, HEAD_DIM), BF16)

    def tile(g, q, k, v, steps, coef, out_rows):
        s = lax.dot_general(q, k, (((1,), (1,)), ((), ())), preferred_element_type=F32)
        s = s * scale - coef * steps
        m = jnp.max(s, axis=-1, keepdims=True)
        p = jnp.exp(s - m).astype(BF16)
        ol = jnp.dot(p, jnp.concatenate([v, ones], axis=-1), preferred_element_type=F32)
        l = ol[:, HEAD_DIM:]
        og_ref[g, out_rows, :] = ol[:, :HEAD_DIM] / l
        eg_ref[g, out_rows, :] = m + jnp.log(l)

    for g, (_, dil) in enumerate(B_CONFIGS):
        length = SEQ // dil
        nt = length // TQ_B
        coef = coef_ref[0, g:g + 1, 0:1]
        if dil == 1:
            qs, ks, vs = (lambda rows, ref=ref: ref[0, rows, :] for ref in inputs[g])
        else:
            for src, dst in zip(inputs[g], (qd_ref, kd_ref, vd_ref)):
                gather_classes(src, dst, dil)
            qs, ks, vs = (lambda rows, ref=ref: ref[rows, :] for ref in (qd_ref, kd_ref, vd_ref))

        def residue(r, g=g, dil=dil, length=length, nt=nt, coef=coef, qs=qs, ks=ks, vs=vs):
            base = r * length
            for ti in range(nt):
                l0 = ti * TQ_B
                k0_ = min(max(l0 - B_SPAN, 0), length - B_WIDTH)
                steps = tab_ref[(l0 - k0_) // B_SPAN]
                if dil == 1:
                    q_rows, k_rows, out_rows = pl.ds(l0, TQ_B), pl.ds(k0_, B_WIDTH), pl.ds(l0, TQ_B)
                else:
                    q_rows = pl.ds(pl.multiple_of(base + l0, TQ_B), TQ_B)
                    k_rows = pl.ds(pl.multiple_of(base + k0_, B_SPAN), B_WIDTH)
                    out_rows = pl.ds(l0 * dil + r, TQ_B, stride=dil)
                tile(g, qs(q_rows), ks(k_rows), vs(k_rows), steps, coef, out_rows)

        if dil == 1:
            residue(0)
        else:
            per_step = max(1, B_TILES_PER_BLOCK // nt)

            def step(it, carry, residue=residue, per_step=per_step):
                for rr in range(per_step):
                    residue(it * per_step + rr)
                return carry

            lax.fori_loop(0, dil // per_step, step, 0)

    def merge(c, carry):
        rows = pl.ds(pl.multiple_of(c * B_MERGE_ROWS, B_MERGE_ROWS), B_MERGE_ROWS)
        lse = [eg_ref[g, rows, :] for g in range(len(B_CONFIGS))]
        mx = jnp.maximum(jnp.maximum(lse[0], lse[1]), lse[2])
        w = [jnp.exp(e - mx) for e in lse]
        den = w[0] + w[1] + w[2]
        o_ref[0, rows, :] = sum((w[g] / den) * og_ref[g, rows, :] for g in range(len(B_CONFIGS))).astype(o_ref.dtype)
        return carry

    lax.fori_loop(0, SEQ // B_MERGE_ROWS, merge, 0)


def _attn_b(qkv_heads, coef, tab):
    t = qkv_heads.shape[1]
    nb = t // SEQ
    base = A_IN // HEAD_DIM
    in_specs = []
    for g in range(len(B_CONFIGS)):
        for kind in range(3):
            head0 = base + kind * B_HEADS + g * B_SLOTS
            in_specs.append(pl.BlockSpec((1, SEQ, HEAD_DIM), lambda b, j, head0=head0: (head0 + j, b, 0)))
    in_specs.append(pl.BlockSpec((1, 8, V7X_LANES), lambda b, j: (j, 0, 0)))
    in_specs.append(pl.BlockSpec(tab.shape, lambda b, j: (0, 0, 0)))
    return pl.pallas_call(
        _attn_b_kernel,
        grid=(nb, B_SLOTS),
        in_specs=in_specs,
        out_specs=pl.BlockSpec((1, SEQ, HEAD_DIM), lambda b, j: (j, b, 0)),
        out_shape=jax.ShapeDtypeStruct((B_SLOTS, t, HEAD_DIM), BF16),
        scratch_shapes=[pltpu.VMEM((SEQ, HEAD_DIM), F32)]
                       + [pltpu.VMEM((SEQ, HEAD_DIM), BF16)] * 3
                       + [pltpu.VMEM((len(B_CONFIGS), SEQ, HEAD_DIM), F32)] * 2,
        compiler_params=_params("parallel", "parallel"),
        name="attn_b",
    )(*([qkv_heads] * 9), coef, tab)


def _mix_out_kernel(x_ref, oa_ref, ob_ref, oc_ref, gout_ref, w_ref, gn_ref, xo_ref, ho_ref, y_ref, d_ref,
                    *, n_tiles):
    i = pl.program_id(0)

    def group_norm(tiles, col0):
        width = len(tiles) * HEAD_DIM
        ssq = sum(jnp.sum(tl * tl, axis=-1, keepdims=True) for tl in tiles)
        r = lax.rsqrt(ssq * (1.0 / width) + EPS)
        for n, tl in enumerate(tiles):
            c = col0 + n * HEAD_DIM
            y_ref[:, c:c + HEAD_DIM] = ((tl * r) * gout_ref[:, c:c + HEAD_DIM]).astype(BF16)

    def finish():
        xn = x_ref[...] + d_ref[...]
        xo_ref[...] = xn
        ho_ref[...] = _rms(xn, gn_ref[...], D_MODEL).astype(BF16)

    def project():
        group_norm([oa_ref[n].astype(F32) for n in range(A_HEADS)], 0)
        group_norm([ob_ref[n].astype(F32) for n in range(B_SLOTS)], A_OUT)
        group_norm([oc_ref[n].astype(F32) for n in range(C_HEADS)], A_OUT + B_OUT)
        d_ref[...] = jnp.dot(y_ref[...], w_ref[...], preferred_element_type=F32)

    pl.when(i == 0)(project)

    @pl.when((i > 0) & (i < n_tiles))
    def _():
        finish()
        project()

    pl.when(i == n_tiles)(finish)


def _mix_out(x, oa, ob, oc, gain_out, w_out, gain_next):
    t = x.shape[0]
    n_tiles = t // TM_OUT
    cur = lambda i: jnp.minimum(i, n_tiles - 1)
    heads = lambda n: pl.BlockSpec((n, TM_OUT, HEAD_DIM), lambda i: (0, cur(i), 0))
    row = pl.BlockSpec((1, D_MODEL), lambda i: (0, 0))
    tok = pl.BlockSpec((TM_OUT, D_MODEL), lambda i: (jnp.maximum(i - 1, 0), 0))
    return pl.pallas_call(
        functools.partial(_mix_out_kernel, n_tiles=n_tiles),
        grid=(n_tiles + 1,),
        in_specs=[tok, heads(A_HEADS), heads(B_SLOTS), heads(C_HEADS), row,
                  pl.BlockSpec((D_MODEL, D_MODEL), lambda i: (0, 0), pipeline_mode=pl.Buffered(1)), row],
        out_specs=[tok, tok],
        out_shape=[jax.ShapeDtypeStruct((t, D_MODEL), F32),
                   jax.ShapeDtypeStruct((t, D_MODEL), BF16)],
        scratch_shapes=[pltpu.VMEM((TM_OUT, D_MODEL), BF16), pltpu.VMEM((TM_OUT, D_MODEL), F32)],
        compiler_params=_params("arbitrary"),
        name="mix_out",
    )(x, oa, ob, oc, gain_out, w_out, gain_next)


def _rope_tables(pos_a, pos_b):
    inv = ROPE_THETA ** (-np.arange(ROPE_HALF, dtype=np.float64) / ROPE_HALF)

    def one(pos):
        ang = pos.astype(np.float64)[:, None] * inv[None, :]
        return np.cos(ang), np.sin(ang)

    ca, sa = one(pos_a)
    cb, sb = one(pos_b)
    cos = np.concatenate([ca, cb, ca, cb], axis=-1).astype(np.float32)
    sin_signed = np.concatenate([-sa, -sb, sa, sb], axis=-1).astype(np.float32)
    return jnp.asarray(cos), jnp.asarray(sin_signed)


def _band_steps_table():
    i = np.arange(TQ_B)[:, None]
    jj = np.arange(B_WIDTH)[None, :]
    tabs = []
    for off in (0, B_SPAN, 2 * B_SPAN):
        rel = np.abs(off + i - jj)
        tabs.append(np.where(rel <= B_SPAN, rel, -NEG_BIG).astype(np.float32))
    return jnp.asarray(np.stack(tabs))


def _alibi_coefficients():
    slopes = 2.0 ** (-8.0 * np.arange(1, B_HEADS + 1, dtype=np.float64) / B_HEADS)
    dil = np.array([c[1] for c in B_CONFIGS], np.float64)
    coef = (slopes.reshape(len(B_CONFIGS), B_SLOTS) * dil[:, None]).T
    out = np.zeros((B_SLOTS, 8, V7X_LANES), np.float32)
    out[:, :len(B_CONFIGS), :] = coef[:, :, None]
    return jnp.asarray(out)


def _head_gains(gq_a, gk_a, gq_b, gk_b):
    ones = jnp.ones((HEAD_DIM,), F32)
    rows = ([gq_a] * A_HEADS + [gk_a] * A_KV_HEADS + [ones] * A_KV_HEADS
            + [gq_b] * B_HEADS + [gk_b] * B_HEADS + [ones] * B_HEADS)
    return jnp.stack(rows).astype(F32).reshape(AB_HEADS, 1, HEAD_DIM)


def kernel(x, ffn1_norm, ffn1_w_gu, ffn1_w_down, mix_norm, w_in, a_q_norm, a_k_norm, b_q_norm, b_k_norm, c_q_a_norm, c_q_up, c_kv_a_norm, c_kv_up, c_q_norm, c_k_norm, out_norm, w_out, ffn2_norm, ffn2_w_gu, ffn2_w_down):
    nb, s, d = x.shape
    assert (s, d) == (SEQ, D_MODEL)
    depth = w_in.shape[0]
    t = nb * s
    x = x.reshape(t, d)
    row = lambda v: v.reshape(1, -1).astype(F32)

    pos = np.arange(s)
    cos_a, sin_a = _rope_tables(pos // GRID_W, pos % GRID_W)
    cos_c, sin_c = _rope_tables(pos, pos)
    lane_order = np.array(ROPE_LANE_ORDER)

    def rope_tile(v):
        zeros = jnp.zeros(v.shape[:-1] + (V7X_LANES // 2 - ROPE_HALF,), v.dtype)
        return jnp.concatenate([v[..., :ROPE_HALF], zeros, v[..., ROPE_HALF:], zeros], axis=-1)
    tab = _band_steps_table()
    coef = _alibi_coefficients()

    h = _norm(x, row(ffn1_norm[0]))
    for l in range(depth):
        x, h = _ffn(x, h, _cast_layer(ffn1_w_gu, l, CAST_ROWS_GU), _cast_layer(ffn1_w_down, l, CAST_ROWS_DOWN),
                    row(mix_norm[l]))

        w = w_in[l]
        rot = (A_HEADS + A_KV_HEADS) * HEAD_DIM
        w_rot = w[:, :rot].reshape(D_MODEL, -1, HEAD_DIM)[:, :, lane_order].reshape(D_MODEL, rot)
        w_ab = jnp.concatenate([w_rot, w[:, rot:AB_IN]], axis=-1).astype(BF16)
        qkv = _proj_heads(h, w_ab, _head_gains(a_q_norm[l][lane_order], a_k_norm[l][lane_order],
                                               b_q_norm[l], b_k_norm[l]), cos_a, sin_a)
        w_c = jnp.concatenate([w[:, AB_IN:W_IN - C_ROPE], rope_tile(w[:, W_IN - C_ROPE:])], axis=-1).astype(BF16)
        q_up = c_q_up[l].reshape(C_Q_RANK, C_HEADS, C_QK)
        q_up = jnp.concatenate([q_up[:, :, :C_NOPE].reshape(C_Q_RANK, C_HEADS * C_NOPE),
                                rope_tile(q_up[:, :, C_NOPE:]).reshape(C_Q_RANK, C_HEADS * V7X_LANES)],
                               axis=-1).astype(BF16)
        split = lambda gvec: jnp.stack([gvec[:C_NOPE], rope_tile(gvec[C_NOPE:])]).astype(F32)
        qc, kc, vc = _proj_c(h, w_c, row(c_q_a_norm[l]), row(c_kv_a_norm[l]), q_up, c_kv_up[l].astype(BF16),
                             split(c_q_norm[l]), split(c_k_norm[l]), cos_c, sin_c)

        oa = _attention(qkv, qkv, qkv, kv_heads=A_KV_HEADS, group=A_HEADS // A_KV_HEADS,
                        k_head0=A_HEADS, v_head0=A_HEADS + A_KV_HEADS, tq=TQ_A, tk=TK_A)
        oc = _attention(qc, kc, vc, kv_heads=C_HEADS, group=1, k_head0=0, v_head0=0, tq=TQ_C, tk=TK_C)
        ob = _attn_b(qkv, coef, tab)

        x, h = _mix_out(x, oa, ob, oc, row(out_norm[l]), _cast_layer(w_out, l, CAST_ROWS_OUT), row(ffn2_norm[l]))
        w_gu = _cast_layer(ffn2_w_gu, l, CAST_ROWS_GU)
        w_down = _cast_layer(ffn2_w_down, l, CAST_ROWS_DOWN)
        if l + 1 < depth:
            x, h = _ffn(x, h, w_gu, w_down, row(ffn1_norm[l + 1]))
        else:
            x, = _ffn(x, h, w_gu, w_down)
    return x.reshape(nb, s, d)
```

```python
import functools

import jax
import jax.numpy as jnp
import numpy as np
from jax import lax
from jax.experimental import pallas as pl
from jax.experimental.pallas import tpu as pltpu

F32 = jnp.float32
BF16 = jnp.bfloat16

D_MODEL = 2048
SEQ = 4096
GRID_W = 64
HEAD_DIM = 128
ROPE_THETA = 10000.0
EPS = 1e-6
NEG_BIG = -1e30
LOG2_E = 1.4426950408889634

A_HEADS = 8
A_KV_HEADS = 2
B_CONFIGS = ((128, 1), (512, 4), (2048, 16))
B_SLOTS = 4
B_HEADS = B_SLOTS * len(B_CONFIGS)
C_HEADS = 4
C_Q_RANK = 512
C_KV_RANK = 256
C_NOPE = 128
C_ROPE = 64
C_QK = C_NOPE + C_ROPE
D_FF = 5632

A_Q = A_HEADS * HEAD_DIM
A_KV = A_KV_HEADS * HEAD_DIM
A_IN = A_Q + 2 * A_KV
B_QKV = B_HEADS * HEAD_DIM
B_IN = 3 * B_QKV
AB_IN = A_IN + B_IN
W_IN = AB_IN + C_Q_RANK + C_KV_RANK + C_ROPE
AB_HEADS = AB_IN // HEAD_DIM
A_OUT = A_HEADS * HEAD_DIM
B_OUT = B_SLOTS * HEAD_DIM
C_OUT = C_HEADS * HEAD_DIM

V7X_LANES = 128
V7X_VMEM_LIMIT = 56 * 1024 * 1024

TM_NORM = 512
CAST_ROWS_GU, CAST_ROWS_DOWN, CAST_ROWS_OUT = 256, 1408, 1024
TM_UP = 2048
TF_FFN = 512
TM_DOWN = 256
TM_PROJ = 2048
TN_PROJ = 512
C_PARTS = 8
TM_OUT = 512
TQ_B = 128
B_SPAN = 64
B_WIDTH = TQ_B + 2 * B_SPAN
B_MERGE_ROWS = 256
B_TILES_PER_BLOCK = 32
TQ_A, TK_A = 128, 256
TQ_C, TK_C = 512, 512


def _params(*sem):
    return pltpu.CompilerParams(dimension_semantics=sem, vmem_limit_bytes=V7X_VMEM_LIMIT)


def _rms(x, gain, width):
    ms = jnp.sum(x * x, axis=-1, keepdims=True) * (1.0 / width)
    return (x * lax.rsqrt(ms + EPS)) * gain


ROPE_HALF = 32
ROPE_LANE_ORDER = tuple(list(range(0, 32)) + list(range(64, 96)) + list(range(32, 64)) + list(range(96, 128)))


def _rope(y, cos, sin_signed):
    return y * cos + pltpu.roll(y, V7X_LANES // 2, 1) * sin_signed


def _norm_kernel(x_ref, g_ref, h_ref):
    h_ref[...] = _rms(x_ref[...], g_ref[...], D_MODEL).astype(BF16)


def _norm(x, gain):
    t = x.shape[0]
    return pl.pallas_call(
        _norm_kernel,
        grid=(t // TM_NORM,),
        in_specs=[pl.BlockSpec((TM_NORM, D_MODEL), lambda i: (i, 0)),
                  pl.BlockSpec((1, D_MODEL), lambda i: (0, 0))],
        out_specs=pl.BlockSpec((TM_NORM, D_MODEL), lambda i: (i, 0)),
        out_shape=jax.ShapeDtypeStruct((t, D_MODEL), BF16),
        compiler_params=_params("parallel"),
        name="norm",
    )(x, gain)


def _cast_kernel(w_ref, o_ref):
    o_ref[...] = w_ref[0].astype(BF16)


def _cast_layer(w, layer, rows):
    _, r, c = w.shape
    return pl.pallas_call(
        _cast_kernel,
        grid=(r // rows,),
        in_specs=[pl.BlockSpec((1, rows, c), lambda i: (layer, i, 0))],
        out_specs=pl.BlockSpec((rows, c), lambda i: (i, 0)),
        out_shape=jax.ShapeDtypeStruct((r, c), BF16),
        compiler_params=_params("parallel"),
        name="cast",
    )(w)


def _ffn_up_kernel(h_ref, wg_ref, wu_ref, a_ref, g_ref, u_ref, *, nf):
    j = pl.program_id(1)

    def swiglu():
        g = g_ref[...]
        a_ref[...] = (g * (1.0 / (1.0 + jnp.exp(-g))) * u_ref[...]).astype(BF16)

    def matmuls():
        h = h_ref[...]
        g_ref[...] = jnp.dot(h, wg_ref[...], preferred_element_type=F32)
        u_ref[...] = jnp.dot(h, wu_ref[...], preferred_element_type=F32)

    pl.when(j == 0)(matmuls)

    @pl.when((j > 0) & (j < nf))
    def _():
        swiglu()
        matmuls()

    pl.when(j == nf)(swiglu)


def _ffn_down_kernel(a_ref, wd_ref, x_ref, *rest, emit_next, n_tiles):
    if emit_next:
        gn_ref, xo_ref, ho_ref, d_ref = rest
    else:
        xo_ref, d_ref = rest
    i = pl.program_id(0)

    def finish():
        xn = x_ref[...] + 0.5 * d_ref[...]
        xo_ref[...] = xn
        if emit_next:
            ho_ref[...] = _rms(xn, gn_ref[...], D_MODEL).astype(BF16)

    def matmul():
        d_ref[...] = jnp.dot(a_ref[...], wd_ref[...], preferred_element_type=F32)

    pl.when(i == 0)(matmul)

    @pl.when((i > 0) & (i < n_tiles))
    def _():
        finish()
        matmul()

    pl.when(i == n_tiles)(finish)


def _ffn(x, h, w_gu, w_down, gain_next=None):
    t = x.shape[0]
    nf = D_FF // TF_FFN
    cur = lambda j: jnp.minimum(j, nf - 1)
    act = pl.pallas_call(
        functools.partial(_ffn_up_kernel, nf=nf),
        grid=(t // TM_UP, nf + 1),
        in_specs=[pl.BlockSpec((TM_UP, D_MODEL), lambda i, j: (i, 0)),
                  pl.BlockSpec((D_MODEL, TF_FFN), lambda i, j: (0, cur(j))),
                  pl.BlockSpec((D_MODEL, TF_FFN), lambda i, j: (0, cur(j) + nf))],
        out_specs=pl.BlockSpec((TM_UP, TF_FFN), lambda i, j: (i, jnp.maximum(j - 1, 0))),
        out_shape=jax.ShapeDtypeStruct((t, D_FF), BF16),
        scratch_shapes=[pltpu.VMEM((TM_UP, TF_FFN), F32)] * 2,
        compiler_params=_params("parallel", "arbitrary"),
        name="ffn_up",
    )(h, w_gu, w_gu)

    emit_next = gain_next is not None
    n_tiles = t // TM_DOWN
    tok = pl.BlockSpec((TM_DOWN, D_MODEL), lambda i: (jnp.maximum(i - 1, 0), 0))
    in_specs = [pl.BlockSpec((TM_DOWN, D_FF), lambda i: (jnp.minimum(i, n_tiles - 1), 0)),
                pl.BlockSpec((D_FF, D_MODEL), lambda i: (0, 0), pipeline_mode=pl.Buffered(1)),
                tok]
    args = [act, w_down, x]
    out_specs = [tok]
    out_shape = [jax.ShapeDtypeStruct((t, D_MODEL), F32)]
    if emit_next:
        in_specs.append(pl.BlockSpec((1, D_MODEL), lambda i: (0, 0)))
        args.append(gain_next)
        out_specs.append(tok)
        out_shape.append(jax.ShapeDtypeStruct((t, D_MODEL), BF16))
    return pl.pallas_call(
        functools.partial(_ffn_down_kernel, emit_next=emit_next, n_tiles=n_tiles),
        grid=(n_tiles + 1,),
        in_specs=in_specs,
        out_specs=out_specs,
        out_shape=out_shape,
        scratch_shapes=[pltpu.VMEM((TM_DOWN, D_MODEL), F32)],
        compiler_params=_params("arbitrary"),
        name="ffn_down",
    )(*args)


HEAD_PLAIN, HEAD_NORM, HEAD_NORM_ROPE = 0, 1, 2
A_Q_SCALE = HEAD_DIM ** -0.5 * LOG2_E
C_Q_SCALE = C_QK ** -0.5 * LOG2_E
AB_HEAD_KINDS = (((HEAD_NORM_ROPE, A_Q_SCALE),) * A_HEADS + ((HEAD_NORM_ROPE, 1.0),) * A_KV_HEADS
                 + ((HEAD_PLAIN, 1.0),) * A_KV_HEADS
                 + ((HEAD_NORM, 1.0),) * (2 * B_HEADS) + ((HEAD_PLAIN, 1.0),) * B_HEADS)


def _proj_heads_kernel(h_ref, w_ref, g_ref, cos_ref, sin_ref, o_ref, z_ref, *, step_kinds):
    j = pl.program_id(1)
    n_blocks = len(step_kinds)

    def step(kinds, with_dot):
        if kinds is not None:
            for hh, (kind, out_scale) in enumerate(kinds):
                y = z_ref[:, hh * HEAD_DIM:(hh + 1) * HEAD_DIM]
                if kind != HEAD_PLAIN:
                    y = _rms(y, g_ref[hh], HEAD_DIM)
                if kind == HEAD_NORM_ROPE:
                    y = _rope(y, cos_ref[...], sin_ref[...])
                if out_scale != 1.0:
                    y = y * out_scale
                o_ref[hh] = y.astype(BF16)
        if with_dot:
            z_ref[...] = jnp.dot(h_ref[...], w_ref[...], preferred_element_type=F32)

    pl.when(j == 0)(functools.partial(step, None, True))
    lo = 0
    while lo < n_blocks:
        hi = lo
        while hi < n_blocks and step_kinds[hi] == step_kinds[lo]:
            hi += 1
        last = min(hi, n_blocks - 1)
        if last > lo:
            pl.when((j > lo) & (j <= last))(functools.partial(step, step_kinds[lo], True))
        lo = hi
    pl.when(j == n_blocks)(functools.partial(step, step_kinds[-1], False))


def _proj_heads(h, w_ab, gains, cos, sin):
    t = h.shape[0]
    heads = TN_PROJ // HEAD_DIM
    n_blocks = AB_IN // TN_PROJ
    seq_tiles = SEQ // TM_PROJ
    step_kinds = tuple(AB_HEAD_KINDS[n:n + heads] for n in range(0, AB_HEADS, heads))
    prev = lambda j: jnp.maximum(j - 1, 0)
    return pl.pallas_call(
        functools.partial(_proj_heads_kernel, step_kinds=step_kinds),
        grid=(t // TM_PROJ, n_blocks + 1),
        in_specs=[pl.BlockSpec((TM_PROJ, D_MODEL), lambda i, j: (i, 0)),
                  pl.BlockSpec((D_MODEL, TN_PROJ), lambda i, j: (0, jnp.minimum(j, n_blocks - 1))),
                  pl.BlockSpec((heads, 1, HEAD_DIM), lambda i, j: (prev(j), 0, 0)),
                  pl.BlockSpec((TM_PROJ, HEAD_DIM), lambda i, j: (i % seq_tiles, 0)),
                  pl.BlockSpec((TM_PROJ, HEAD_DIM), lambda i, j: (i % seq_tiles, 0))],
        out_specs=pl.BlockSpec((heads, TM_PROJ, HEAD_DIM), lambda i, j: (prev(j), i, 0)),
        out_shape=jax.ShapeDtypeStruct((AB_HEADS, t, HEAD_DIM), BF16),
        scratch_shapes=[pltpu.VMEM((TM_PROJ, TN_PROJ), F32)],
        compiler_params=_params("parallel", "arbitrary"),
        name="proj_heads",
    )(h, w_ab, gains, cos, sin)


def _proj_c_kernel(h_ref, wc_ref, gqa_ref, gkva_ref, qup_ref, kvup_ref, gq_ref, gk_ref,
                   cos_ref, sin_ref, q_ref, k_ref, v_ref):
    nope_w = C_HEADS * C_NOPE
    part_rows = h_ref.shape[0] // C_PARTS
    for part in range(C_PARTS):
        rows = slice(part * part_rows, (part + 1) * part_rows)
        h = h_ref[rows, :]
        cos = cos_ref[rows, :]
        sin = sin_ref[rows, :]

        zq = jnp.dot(h, wc_ref[:, :C_Q_RANK], preferred_element_type=F32)
        zkv = jnp.dot(h, wc_ref[:, C_Q_RANK:], preferred_element_type=F32)
        q_lat = _rms(zq, gqa_ref[...], C_Q_RANK).astype(BF16)
        cq = jnp.dot(q_lat, qup_ref[...], preferred_element_type=F32)
        kv_lat = _rms(zkv[:, :C_KV_RANK], gkva_ref[...], C_KV_RANK).astype(BF16)
        k_rope = zkv[:, C_KV_RANK:]
        ckv = jnp.dot(kv_lat, kvup_ref[...], preferred_element_type=F32)

        for hh in range(C_HEADS):
            qn = cq[:, hh * C_NOPE:(hh + 1) * C_NOPE]
            qr = cq[:, nope_w + hh * V7X_LANES:nope_w + (hh + 1) * V7X_LANES]
            ms = (jnp.sum(qn * qn, axis=-1, keepdims=True)
                  + jnp.sum(qr * qr, axis=-1, keepdims=True)) * (1.0 / C_QK)
            r = lax.rsqrt(ms + EPS)
            q_ref[hh, rows, 0:C_NOPE] = (((qn * r) * gq_ref[0:1, :]) * C_Q_SCALE).astype(BF16)
            q_ref[hh, rows, C_NOPE:] = (_rope((qr * r) * gq_ref[1:2, :], cos, sin) * C_Q_SCALE).astype(BF16)

        k_rope_sq = jnp.sum(k_rope * k_rope, axis=-1, keepdims=True)
        for hh in range(C_HEADS):
            kn = ckv[:, hh * 2 * C_NOPE:hh * 2 * C_NOPE + C_NOPE]
            ms = (jnp.sum(kn * kn, axis=-1, keepdims=True) + k_rope_sq) * (1.0 / C_QK)
            r = lax.rsqrt(ms + EPS)
            k_ref[hh, rows, 0:C_NOPE] = ((kn * r) * gk_ref[0:1, :]).astype(BF16)
            k_ref[hh, rows, C_NOPE:] = _rope((k_rope * r) * gk_ref[1:2, :], cos, sin).astype(BF16)
            v_ref[hh, rows, :] = ckv[:, hh * 2 * C_NOPE + C_NOPE:(hh + 1) * 2 * C_NOPE].astype(BF16)


def _proj_c(h, w_c, gqa, gkva, q_up, kv_up, gq, gk, cos, sin):
    t = h.shape[0]
    seq_tiles = SEQ // TM_PROJ
    full = lambda a: pl.BlockSpec(a.shape, lambda i: (0,) * a.ndim)
    qk_w = 2 * V7X_LANES
    return pl.pallas_call(
        _proj_c_kernel,
        grid=(t // TM_PROJ,),
        in_specs=[pl.BlockSpec((TM_PROJ, D_MODEL), lambda i: (i, 0)),
                  full(w_c), full(gqa), full(gkva), full(q_up), full(kv_up), full(gq), full(gk),
                  pl.BlockSpec((TM_PROJ, V7X_LANES), lambda i: (i % seq_tiles, 0)),
                  pl.BlockSpec((TM_PROJ, V7X_LANES), lambda i: (i % seq_tiles, 0))],
        out_specs=[pl.BlockSpec((C_HEADS, TM_PROJ, qk_w), lambda i: (0, i, 0)),
                   pl.BlockSpec((C_HEADS, TM_PROJ, qk_w), lambda i: (0, i, 0)),
                   pl.BlockSpec((C_HEADS, TM_PROJ, C_NOPE), lambda i: (0, i, 0))],
        out_shape=[jax.ShapeDtypeStruct((C_HEADS, t, qk_w), BF16),
                   jax.ShapeDtypeStruct((C_HEADS, t, qk_w), BF16),
                   jax.ShapeDtypeStruct((C_HEADS, t, C_NOPE), BF16)],
        compiler_params=_params("parallel"),
        name="proj_c",
    )(h, w_c, gqa, gkva, q_up, kv_up, gq, gk, cos, sin)


def _attn_kernel(q_ref, k_ref, v_ref, o_ref, s0_ref, s1_ref, m0_ref, m1_ref, *, tq, tk):
    g, _, dq = q_ref.shape
    rows = g * tq
    dv = v_ref.shape[-1]
    tiles = SEQ // tq
    contract_last = (((1,), (1,)), ((), ()))
    contract_first = (((0,), (0,)), ((), ()))

    def q_rows(tile):
        start = tile * tq
        return pl.ds(start if isinstance(tile, int) else pl.multiple_of(start, tq), tq)

    def passes(score_tile, score_bufs, out_tile, out_bufs):
        if score_tile is not None:
            sw_ref, mw_ref = score_bufs
            q = q_ref[:, q_rows(score_tile), :].reshape(rows, dq)
            m_new = None
        if out_tile is not None:
            sr_ref, mr_ref = out_bufs
            m = mr_ref[...]
            lsum = jnp.zeros((1, rows), F32)
            acc = jnp.zeros((dv, rows), F32)
        for c in range(SEQ // tk):
            keys = slice(c * tk, (c + 1) * tk)
            if score_tile is not None:
                s = lax.dot_general(k_ref[0, keys, :], q, contract_last, preferred_element_type=F32)
                sw_ref[keys, :] = s
                part = jnp.max(s, axis=0, keepdims=True)
                m_new = part if m_new is None else jnp.maximum(m_new, part)
            if out_tile is not None:
                p = jnp.exp2(sr_ref[keys, :] - m)
                lsum = lsum + jnp.sum(p, axis=0, keepdims=True)
                acc = acc + lax.dot_general(v_ref[0, keys, :], p.astype(BF16), contract_first,
                                            preferred_element_type=F32)
        if score_tile is not None:
            mw_ref[...] = m_new
        if out_tile is not None:
            o = (acc / lsum).T
            o_ref[:, q_rows(out_tile), :] = o.reshape(g, tq, dv).astype(o_ref.dtype)

    buf0, buf1 = (s0_ref, m0_ref), (s1_ref, m1_ref)
    passes(0, buf0, None, None)

    def pair(n, carry):
        tile = 2 * n
        passes(tile + 1, buf1, tile, buf0)

        @pl.when(n < tiles // 2 - 1)
        def _():
            passes(tile + 2, buf0, tile + 1, buf1)

        @pl.when(n == tiles // 2 - 1)
        def _():
            passes(None, None, tile + 1, buf1)

        return carry

    lax.fori_loop(0, tiles // 2, pair, 0)


def _attention(q, k, v, *, kv_heads, group, k_head0, v_head0, tq, tk):
    _, t, dq = q.shape
    dv = v.shape[-1]
    nb = t // SEQ
    rows = group * tq
    return pl.pallas_call(
        functools.partial(_attn_kernel, tq=tq, tk=tk),
        grid=(nb, kv_heads),
        in_specs=[pl.BlockSpec((group, SEQ, dq), lambda b, h: (h, b, 0)),
                  pl.BlockSpec((1, SEQ, dq), lambda b, h: (k_head0 + h, b, 0)),
                  pl.BlockSpec((1, SEQ, dv), lambda b, h: (v_head0 + h, b, 0))],
        out_specs=pl.BlockSpec((group, SEQ, dv), lambda b, h: (h, b, 0)),
        out_shape=jax.ShapeDtypeStruct((kv_heads * group, t, dv), BF16),
        scratch_shapes=[pltpu.VMEM((SEQ, rows), F32)] * 2 + [pltpu.VMEM((1, rows), F32)] * 2,
        compiler_params=_params("parallel", "parallel"),
        name="attention",
    )(q, k, v)


def _attn_b_kernel(q0, k0, v0, q1, k1, v1, q2, k2, v2, coef_ref, tab_ref, o_ref,
                   stage_ref, qd_ref, kd_ref, vd_ref, og_ref, eg_ref):
    scale = HEAD_DIM ** -0.5
    inputs = ((q0, k0, v0), (q1, k1, v1), (q2, k2, v2))
    chunk = 64

    def gather_classes(src_ref, dst_ref, dil):
        length = SEQ // dil
        stage_ref[...] = src_ref[0].astype(F32)

        def body(c, carry):
            for r in range(dil):
                src = pl.ds(pl.multiple_of(c * chunk * dil, chunk * dil) + r, chunk, stride=dil)
                dst = pl.ds(pl.multiple_of(r * length + c * chunk, chunk), chunk)
                dst_ref[dst, :] = stage_ref[src, :].astype(BF16)
            return carry

        lax.fori_loop(0, length // chunk, body, 0)

    ones = jnp.ones((B_WIDTH, HEAD_DIM), BF16)

    def tile(g, q, k, v, steps, coef, out_rows):
        s = lax.dot_general(q, k, (((1,), (1,)), ((), ())), preferred_element_type=F32)
        s = s * scale - coef * steps
        m = jnp.max(s, axis=-1, keepdims=True)
        p = jnp.exp(s - m).astype(BF16)
        ol = jnp.dot(p, jnp.concatenate([v, ones], axis=-1), preferred_element_type=F32)
        l = ol[:, HEAD_DIM:]
        og_ref[g, out_rows, :] = ol[:, :HEAD_DIM] / l
        eg_ref[g, out_rows, :] = m + jnp.log(l)

    for g, (_, dil) in enumerate(B_CONFIGS):
        length = SEQ // dil
        nt = length // TQ_B
        coef = coef_ref[0, g:g + 1, 0:1]
        if dil == 1:
            qs, ks, vs = (lambda rows, ref=ref: ref[0, rows, :] for ref in inputs[g])
        else:
            for src, dst in zip(inputs[g], (qd_ref, kd_ref, vd_ref)):
                gather_classes(src, dst, dil)
            qs, ks, vs = (lambda rows, ref=ref: ref[rows, :] for ref in (qd_ref, kd_ref, vd_ref))

        def residue(r, g=g, dil=dil, length=length, nt=nt, coef=coef, qs=qs, ks=ks, vs=vs):
            base = r * length
            for ti in range(nt):
                l0 = ti * TQ_B
                k0_ = min(max(l0 - B_SPAN, 0), length - B_WIDTH)
                steps = tab_ref[(l0 - k0_) // B_SPAN]
                if dil == 1:
                    q_rows, k_rows, out_rows = pl.ds(l0, TQ_B), pl.ds(k0_, B_WIDTH), pl.ds(l0, TQ_B)
                else:
                    q_rows = pl.ds(pl.multiple_of(base + l0, TQ_B), TQ_B)
                    k_rows = pl.ds(pl.multiple_of(base + k0_, B_SPAN), B_WIDTH)
                    out_rows = pl.ds(l0 * dil + r, TQ_B, stride=dil)
                tile(g, qs(q_rows), ks(k_rows), vs(k_rows), steps, coef, out_rows)

        if dil == 1:
            residue(0)
        else:
            per_step = max(1, B_TILES_PER_BLOCK // nt)

            def step(it, carry, residue=residue, per_step=per_step):
                for rr in range(per_step):
                    residue(it * per_step + rr)
                return carry

            lax.fori_loop(0, dil // per_step, step, 0)

    def merge(c, carry):
        rows = pl.ds(pl.multiple_of(c * B_MERGE_ROWS, B_MERGE_ROWS), B_MERGE_ROWS)
        lse = [eg_ref[g, rows, :] for g in range(len(B_CONFIGS))]
        mx = jnp.maximum(jnp.maximum(lse[0], lse[1]), lse[2])
        w = [jnp.exp(e - mx) for e in lse]
        den = w[0] + w[1] + w[2]
        o_ref[0, rows, :] = sum((w[g] / den) * og_ref[g, rows, :] for g in range(len(B_CONFIGS))).astype(o_ref.dtype)
        return carry

    lax.fori_loop(0, SEQ // B_MERGE_ROWS, merge, 0)


def _attn_b(qkv_heads, coef, tab):
    t = qkv_heads.shape[1]
    nb = t // SEQ
    base = A_IN // HEAD_DIM
    in_specs = []
    for g in range(len(B_CONFIGS)):
        for kind in range(3):
            head0 = base + kind * B_HEADS + g * B_SLOTS
            in_specs.append(pl.BlockSpec((1, SEQ, HEAD_DIM), lambda b, j, head0=head0: (head0 + j, b, 0)))
    in_specs.append(pl.BlockSpec((1, 8, V7X_LANES), lambda b, j: (j, 0, 0)))
    in_specs.append(pl.BlockSpec(tab.shape, lambda b, j: (0, 0, 0)))
    return pl.pallas_call(
        _attn_b_kernel,
        grid=(nb, B_SLOTS),
        in_specs=in_specs,
        out_specs=pl.BlockSpec((1, SEQ, HEAD_DIM), lambda b, j: (j, b, 0)),
        out_shape=jax.ShapeDtypeStruct((B_SLOTS, t, HEAD_DIM), BF16),
        scratch_shapes=[pltpu.VMEM((SEQ, HEAD_DIM), F32)]
                       + [pltpu.VMEM((SEQ, HEAD_DIM), BF16)] * 3
                       + [pltpu.VMEM((len(B_CONFIGS), SEQ, HEAD_DIM), F32)] * 2,
        compiler_params=_params("parallel", "parallel"),
        name="attn_b",
    )(*([qkv_heads] * 9), coef, tab)


def _mix_out_kernel(x_ref, oa_ref, ob_ref, oc_ref, gout_ref, w_ref, gn_ref, xo_ref, ho_ref, y_ref, d_ref,
                    *, n_tiles):
    i = pl.program_id(0)

    def group_norm(tiles, col0):
        width = len(tiles) * HEAD_DIM
        ssq = sum(jnp.sum(tl * tl, axis=-1, keepdims=True) for tl in tiles)
        r = lax.rsqrt(ssq * (1.0 / width) + EPS)
        for n, tl in enumerate(tiles):
            c = col0 + n * HEAD_DIM
            y_ref[:, c:c + HEAD_DIM] = ((tl * r) * gout_ref[:, c:c + HEAD_DIM]).astype(BF16)

    def finish():
        xn = x_ref[...] + d_ref[...]
        xo_ref[...] = xn
        ho_ref[...] = _rms(xn, gn_ref[...], D_MODEL).astype(BF16)

    def project():
        group_norm([oa_ref[n].astype(F32) for n in range(A_HEADS)], 0)
        group_norm([ob_ref[n].astype(F32) for n in range(B_SLOTS)], A_OUT)
        group_norm([oc_ref[n].astype(F32) for n in range(C_HEADS)], A_OUT + B_OUT)
        d_ref[...] = jnp.dot(y_ref[...], w_ref[...], preferred_element_type=F32)

    pl.when(i == 0)(project)

    @pl.when((i > 0) & (i < n_tiles))
    def _():
        finish()
        project()

    pl.when(i == n_tiles)(finish)


def _mix_out(x, oa, ob, oc, gain_out, w_out, gain_next):
    t = x.shape[0]
    n_tiles = t // TM_OUT
    cur = lambda i: jnp.minimum(i, n_tiles - 1)
    heads = lambda n: pl.BlockSpec((n, TM_OUT, HEAD_DIM), lambda i: (0, cur(i), 0))
    row = pl.BlockSpec((1, D_MODEL), lambda i: (0, 0))
    tok = pl.BlockSpec((TM_OUT, D_MODEL), lambda i: (jnp.maximum(i - 1, 0), 0))
    return pl.pallas_call(
        functools.partial(_mix_out_kernel, n_tiles=n_tiles),
        grid=(n_tiles + 1,),
        in_specs=[tok, heads(A_HEADS), heads(B_SLOTS), heads(C_HEADS), row,
                  pl.BlockSpec((D_MODEL, D_MODEL), lambda i: (0, 0), pipeline_mode=pl.Buffered(1)), row],
        out_specs=[tok, tok],
        out_shape=[jax.ShapeDtypeStruct((t, D_MODEL), F32),
                   jax.ShapeDtypeStruct((t, D_MODEL), BF16)],
        scratch_shapes=[pltpu.VMEM((TM_OUT, D_MODEL), BF16), pltpu.VMEM((TM_OUT, D_MODEL), F32)],
        compiler_params=_params("arbitrary"),
        name="mix_out",
    )(x, oa, ob, oc, gain_out, w_out, gain_next)


def _rope_tables(pos_a, pos_b):
    inv = ROPE_THETA ** (-np.arange(ROPE_HALF, dtype=np.float64) / ROPE_HALF)

    def one(pos):
        ang = pos.astype(np.float64)[:, None] * inv[None, :]
        return np.cos(ang), np.sin(ang)

    ca, sa = one(pos_a)
    cb, sb = one(pos_b)
    cos = np.concatenate([ca, cb, ca, cb], axis=-1).astype(np.float32)
    sin_signed = np.concatenate([-sa, -sb, sa, sb], axis=-1).astype(np.float32)
    return jnp.asarray(cos), jnp.asarray(sin_signed)


def _band_steps_table():
    i = np.arange(TQ_B)[:, None]
    jj = np.arange(B_WIDTH)[None, :]
    tabs = []
    for off in (0, B_SPAN, 2 * B_SPAN):
        rel = np.abs(off + i - jj)
        tabs.append(np.where(rel <= B_SPAN, rel, -NEG_BIG).astype(np.float32))
    return jnp.asarray(np.stack(tabs))


def _alibi_coefficients():
    slopes = 2.0 ** (-8.0 * np.arange(1, B_HEADS + 1, dtype=np.float64) / B_HEADS)
    dil = np.array([c[1] for c in B_CONFIGS], np.float64)
    coef = (slopes.reshape(len(B_CONFIGS), B_SLOTS) * dil[:, None]).T
    out = np.zeros((B_SLOTS, 8, V7X_LANES), np.float32)
    out[:, :len(B_CONFIGS), :] = coef[:, :, None]
    return jnp.asarray(out)


def _head_gains(gq_a, gk_a, gq_b, gk_b):
    ones = jnp.ones((HEAD_DIM,), F32)
    rows = ([gq_a] * A_HEADS + [gk_a] * A_KV_HEADS + [ones] * A_KV_HEADS
            + [gq_b] * B_HEADS + [gk_b] * B_HEADS + [ones] * B_HEADS)
    return jnp.stack(rows).astype(F32).reshape(AB_HEADS, 1, HEAD_DIM)


def kernel(x, ffn1_norm, ffn1_w_gu, ffn1_w_down, mix_norm, w_in, a_q_norm, a_k_norm, b_q_norm, b_k_norm, c_q_a_norm, c_q_up, c_kv_a_norm, c_kv_up, c_q_norm, c_k_norm, out_norm, w_out, ffn2_norm, ffn2_w_gu, ffn2_w_down):
    nb, s, d = x.shape
    assert (s, d) == (SEQ, D_MODEL)
    depth = w_in.shape[0]
    t = nb * s
    x = x.reshape(t, d)
    row = lambda v: v.reshape(1, -1).astype(F32)

    pos = np.arange(s)
    cos_a, sin_a = _rope_tables(pos // GRID_W, pos % GRID_W)
    cos_c, sin_c = _rope_tables(pos, pos)
    lane_order = np.array(ROPE_LANE_ORDER)

    def rope_tile(v):
        zeros = jnp.zeros(v.shape[:-1] + (V7X_LANES // 2 - ROPE_HALF,), v.dtype)
        return jnp.concatenate([v[..., :ROPE_HALF], zeros, v[..., ROPE_HALF:], zeros], axis=-1)
    tab = _band_steps_table()
    coef = _alibi_coefficients()

    h = _norm(x, row(ffn1_norm[0]))
    for l in range(depth):
        x, h = _ffn(x, h, _cast_layer(ffn1_w_gu, l, CAST_ROWS_GU), _cast_layer(ffn1_w_down, l, CAST_ROWS_DOWN),
                    row(mix_norm[l]))

        w = w_in[l]
        rot = (A_HEADS + A_KV_HEADS) * HEAD_DIM
        w_rot = w[:, :rot].reshape(D_MODEL, -1, HEAD_DIM)[:, :, lane_order].reshape(D_MODEL, rot)
        w_ab = jnp.concatenate([w_rot, w[:, rot:AB_IN]], axis=-1).astype(BF16)
        qkv = _proj_heads(h, w_ab, _head_gains(a_q_norm[l][lane_order], a_k_norm[l][lane_order],
                                               b_q_norm[l], b_k_norm[l]), cos_a, sin_a)
        w_c = jnp.concatenate([w[:, AB_IN:W_IN - C_ROPE], rope_tile(w[:, W_IN - C_ROPE:])], axis=-1).astype(BF16)
        q_up = c_q_up[l].reshape(C_Q_RANK, C_HEADS, C_QK)
        q_up = jnp.concatenate([q_up[:, :, :C_NOPE].reshape(C_Q_RANK, C_HEADS * C_NOPE),
                                rope_tile(q_up[:, :, C_NOPE:]).reshape(C_Q_RANK, C_HEADS * V7X_LANES)],
                               axis=-1).astype(BF16)
        split = lambda gvec: jnp.stack([gvec[:C_NOPE], rope_tile(gvec[C_NOPE:])]).astype(F32)
        qc, kc, vc = _proj_c(h, w_c, row(c_q_a_norm[l]), row(c_kv_a_norm[l]), q_up, c_kv_up[l].astype(BF16),
                             split(c_q_norm[l]), split(c_k_norm[l]), cos_c, sin_c)

        oa = _attention(qkv, qkv, qkv, kv_heads=A_KV_HEADS, group=A_HEADS // A_KV_HEADS,
                        k_head0=A_HEADS, v_head0=A_HEADS + A_KV_HEADS, tq=TQ_A, tk=TK_A)
        oc = _attention(qc, kc, vc, kv_heads=C_HEADS, group=1, k_head0=0, v_head0=0, tq=TQ_C, tk=TK_C)
        ob = _attn_b(qkv, coef, tab)

        x, h = _mix_out(x, oa, ob, oc, row(out_norm[l]), _cast_layer(w_out, l, CAST_ROWS_OUT), row(ffn2_norm[l]))
        w_gu = _cast_layer(ffn2_w_gu, l, CAST_ROWS_GU)
        w_down = _cast_layer(ffn2_w_down, l, CAST_ROWS_DOWN)
        if l + 1 < depth:
            x, h = _ffn(x, h, w_gu, w_down, row(ffn1_norm[l + 1]))
        else:
            x, = _ffn(x, h, w_gu, w_down)
    return x.reshape(nb, s, d)
```
